```python
import math
import jax, jax.numpy as jnp
from jax import lax
import numpy as np

D_MODEL = 1024
BATCH = 2
SEQ = 8192
DEPTH = 4

CTX_LEN = 256
GRID_W = 64
GDN_H = 4
GDN_DK = 128
GDN_DV = 128
GDN_CHUNK = 64
CONV_K = 3
DIFF_H = 4
DIFF_D = 64
Q_BLOCK = 128
ROPE_BASE = 10000.0
NA_H = 8
NA_D = 64
WIN_H = 8
WIN_W = 16
NA_QB_W = 16
NA_KB_W = 32
N_BRANCH = 3
BRANCH_W = 512
GDN_QK = GDN_H * GDN_DK
GDN_V = GDN_H * GDN_DV
DIFF_W = 2 * DIFF_H * DIFF_D
NA_W = NA_H * NA_D
IN_SPLITS = (GDN_QK, GDN_QK, GDN_V, GDN_V, GDN_H, GDN_H, GDN_H, GDN_H,
             DIFF_W, DIFF_W, DIFF_W, NA_W, NA_W, NA_W, N_BRANCH * D_MODEL)
D_IN = sum(IN_SPLITS)
N_EXPERTS = 32
TOP_K = 4
D_EXPERT = 1024
SWIGLU_LIMIT = 7.0
SWIGLU_ALPHA = 1.702
MOE_BLOCK = 128
DN_ALPHA = (2 * DEPTH) ** 0.25
DN_BETA = (8 * DEPTH) ** -0.25
LN_EPS = 1e-5
RMS_EPS = 1e-6
NEG_INF = -1e30

kernel_name = 'hybrid_gdn_diffattn_natten_moe_dit'

F32 = jnp.float32


def _layernorm(x):
    xf = x.astype(F32)
    mu = jnp.mean(xf, -1, keepdims=True)
    var = jnp.mean(jnp.square(xf - mu), -1, keepdims=True)
    return ((xf - mu) * lax.rsqrt(var + LN_EPS)).astype(x.dtype)


def _rmsnorm(x, w):
    xf = x.astype(F32)
    return xf * lax.rsqrt(jnp.mean(jnp.square(xf), -1, keepdims=True) + RMS_EPS) * w.astype(F32)


def _l2norm(x):
    return x * lax.rsqrt(jnp.sum(jnp.square(x), -1, keepdims=True) + RMS_EPS)


def _modulate(x, shift, scale):
    return _layernorm(x) * (1 + scale) + shift


def _post_norm(x, y, g, b):
    return _layernorm(DN_ALPHA * x + y) * g + b


def _split_cols(p):
    return jnp.split(p, np.cumsum(IN_SPLITS)[:-1].tolist(), axis=-1)


def _axial_rope(n_tok, dtype):
    t = jnp.arange(n_tok)
    row = (t // GRID_W).astype(F32)
    col = (t % GRID_W).astype(F32)
    n_freq = DIFF_D // 4
    inv = ROPE_BASE ** (-jnp.arange(n_freq, dtype=F32) / n_freq)
    ang = jnp.concatenate([row[:, None] * inv, col[:, None] * inv], -1)
    ang = jnp.concatenate([ang, ang], -1)
    return jnp.cos(ang).astype(dtype), jnp.sin(ang).astype(dtype)


def _apply_rope(x, cos, sin):
    x1, x2 = jnp.split(x, 2, axis=-1)
    return x * cos + jnp.concatenate([-x2, x1], -1) * sin


def _short_conv(x, w):
    return lax.conv_general_dilated(x, w[:, None, :], window_strides=(1,),
                                    padding=[(CONV_K // 2, CONV_K // 2)],
                                    dimension_numbers=('NWC', 'WIO', 'NWC'),
                                    feature_group_count=x.shape[-1])


def _gdn_qkv(q, k, v, conv_w):
    b, L, _ = q.shape
    qkv = jax.nn.silu(_short_conv(jnp.concatenate([q, k, v], -1), conv_w)).astype(F32)
    q, k, v = jnp.split(qkv, [GDN_QK, 2 * GDN_QK], axis=-1)
    q = _l2norm(q.reshape(b, L, GDN_H, GDN_DK)) * GDN_DK ** -0.5
    k = _l2norm(k.reshape(b, L, GDN_H, GDN_DK))
    return q, k, v.reshape(b, L, GDN_H, GDN_DV)


def _gdn_decay(a, bb, a_log, dt_bias):
    g = -jnp.exp(a_log.astype(F32)) * jax.nn.softplus(a.astype(F32) + dt_bias.astype(F32))
    return g, jax.nn.sigmoid(bb.astype(F32))


def _gdn_chunk_scan(q, k, v, g, beta, s0, with_out):
    b, L, h, _ = q.shape
    dv = v.shape[-1]
    cs = GDN_CHUNK
    n = L // cs

    def blk(t):
        return t.reshape(b, n, cs, h, -1).transpose(0, 3, 1, 2, 4)

    q, k, v = blk(q), blk(k), blk(v)
    g = g.reshape(b, n, cs, h).transpose(0, 3, 1, 2)
    beta = beta.reshape(b, n, cs, h).transpose(0, 3, 1, 2)
    gc = jnp.cumsum(g, axis=-1)
    i = jnp.arange(cs)
    incl = i[:, None] >= i[None, :]
    strict = i[:, None] > i[None, :]
    decay = jnp.where(incl, jnp.exp(jnp.where(incl, gc[..., :, None] - gc[..., None, :], 0.0)), 0.0)
    kb = k * beta[..., None]
    low = jnp.where(strict, jnp.einsum('bhnid,bhnjd->bhnij', kb, k) * decay, 0.0)
    rhs = jnp.concatenate([v * beta[..., None], kb * jnp.exp(gc)[..., None]], axis=-1)
    sol = lax.linalg.triangular_solve(low + jnp.eye(cs, dtype=F32), rhs, left_side=True, lower=True)
    u, w = sol[..., :dv], sol[..., dv:]
    g_last = gc[..., -1]
    k_tail = k * jnp.exp(g_last[..., None] - gc)[..., None]

    def seq(t):
        return jnp.moveaxis(t, 2, 0)

    def state_update(S, u_i, w_i, kt_i, gl_i):
        v_new = u_i - jnp.einsum('bhck,bhkv->bhcv', w_i, S)
        S_new = S * jnp.exp(gl_i)[..., None, None] + jnp.einsum('bhck,bhcv->bhkv', kt_i, v_new)
        return S_new, v_new

    if not with_out:
        def step(S, xs):
            S_new, _ = state_update(S, *xs)
            return S_new, None
        s_fin, _ = lax.scan(step, s0, (seq(u), seq(w), seq(k_tail), seq(g_last)))
        return s_fin, None

    qk = jnp.where(incl, jnp.einsum('bhnid,bhnjd->bhnij', q, k) * decay, 0.0)
    qg = q * jnp.exp(gc)[..., None]

    def step_out(S, xs):
        u_i, w_i, kt_i, gl_i, qk_i, qg_i = xs
        S_new, v_new = state_update(S, u_i, w_i, kt_i, gl_i)
        o_i = jnp.einsum('bhck,bhkv->bhcv', qg_i, S) + jnp.einsum('bhcj,bhjv->bhcv', qk_i, v_new)
        return S_new, o_i

    s_fin, o = lax.scan(step_out, s0, tuple(seq(t) for t in (u, w, k_tail, g_last, qk, qg)))
    o = jnp.moveaxis(o, 0, 2).transpose(0, 2, 3, 1, 4).reshape(b, L, h, dv)
    return s_fin, o


def _gated_rms(o, z, w):
    b, L, h, dv = o.shape
    zf = z.reshape(b, L, h, dv).astype(F32)
    return (_rmsnorm(o, w) * jax.nn.silu(zf)).reshape(b, L, h * dv).astype(z.dtype)


def _gdn_branch(lat, ctx, conv_w, a_log, dt_bias, norm_w, with_ctx_out):
    qx, kx, vx = _gdn_qkv(lat[0], lat[1], lat[2], conv_w)
    qc, kc, vc = _gdn_qkv(ctx[0], ctx[1], ctx[2], conv_w)
    s0 = jnp.zeros((qx.shape[0], GDN_H, GDN_DK, GDN_DV), F32)
    outs_x, outs_c = [], []
    for d in range(2):
        fl = (lambda t: t[:, ::-1]) if d else (lambda t: t)
        gx, bx = _gdn_decay(lat[4 + 2 * d], lat[5 + 2 * d], a_log[d], dt_bias[d])
        gcc, bcc = _gdn_decay(ctx[4 + 2 * d], ctx[5 + 2 * d], a_log[d], dt_bias[d])
        s_ctx, oc = _gdn_chunk_scan(fl(qc), fl(kc), fl(vc), fl(gcc), fl(bcc), s0, with_ctx_out)
        _, ox = _gdn_chunk_scan(fl(qx), fl(kx), fl(vx), fl(gx), fl(bx), s_ctx, True)
        outs_x.append(fl(ox))
        if with_ctx_out:
            outs_c.append(fl(oc))
    yx = _gated_rms(outs_x[0] + outs_x[1], lat[3], norm_w)
    yc = _gated_rms(outs_c[0] + outs_c[1], ctx[3], norm_w) if with_ctx_out else None
    return yx, yc


def _diff_attend(q, k, v, lam):
    s = jnp.einsum('bhmqd,bhmkd->bhmqk', q, k).astype(F32) * DIFF_D ** -0.5
    p = jax.nn.softmax(s, axis=-1)
    a = p[:, :, 0] - lam * p[:, :, 1]
    return jnp.einsum('bhqk,bhke->bhqe', a.astype(v.dtype), v)


def _diff_out(o, norm_w, lam_init, dtype):
    b, h, L, e = o.shape
    y = _rmsnorm(o, norm_w) * (1.0 - lam_init)
    return y.transpose(0, 2, 1, 3).reshape(b, L, h * e).astype(dtype)


def _diff_branch(lat, ctx, lam_vec, lam_init, norm_w, cos, sin, with_ctx_out):
    qx, kx, vx = lat
    qc, kc, vc = ctx
    b, S, _ = qx.shape

    def qk_heads(t):
        return t.reshape(b, t.shape[1], DIFF_H, 2, DIFF_D).transpose(0, 2, 3, 1, 4)

    def v_heads(t):
        return t.reshape(b, t.shape[1], DIFF_H, 2 * DIFF_D).transpose(0, 2, 1, 3)

    lv = lam_vec.astype(F32)
    lam = jnp.exp(jnp.sum(lv[0] * lv[1])) - jnp.exp(jnp.sum(lv[2] * lv[3])) + lam_init
    qx_h = _apply_rope(qk_heads(qx), cos, sin)
    kx_h = _apply_rope(qk_heads(kx), cos, sin)
    kc_h, vc_h = qk_heads(kc), v_heads(vc)
    k_all = jnp.concatenate([kx_h, kc_h], axis=3)
    v_all = jnp.concatenate([v_heads(vx), vc_h], axis=2)
    nb = S // Q_BLOCK
    q_blocks = jnp.moveaxis(qx_h.reshape(b, DIFF_H, 2, nb, Q_BLOCK, DIFF_D), 3, 0)
    o = lax.map(lambda qb: _diff_attend(qb, k_all, v_all, lam), q_blocks)
    o = jnp.moveaxis(o, 0, 2).reshape(b, DIFF_H, S, 2 * DIFF_D)
    yx = _diff_out(o, norm_w, lam_init, qx.dtype)
    yc = _diff_out(_diff_attend(qk_heads(qc), kc_h, vc_h, lam), norm_w, lam_init, qc.dtype) if with_ctx_out else None
    return yx, yc


def _na_branch(lat, ctx, rpb, with_ctx_out):
    qx, kx, vx = lat
    qc, kc, vc = ctx
    b, S, _ = qx.shape
    dt = qx.dtype
    rows = S // GRID_W
    kh = min(WIN_H, rows)
    n_cb = GRID_W // NA_QB_W
    scale = NA_D ** -0.5

    def heads(t):
        return t.reshape(b, t.shape[1], NA_H, NA_D).transpose(0, 2, 1, 3)

    qg = heads(qx).reshape(b, NA_H, rows, GRID_W, NA_D)
    kg = heads(kx).reshape(b, NA_H, rows, GRID_W, NA_D)
    vg = heads(vx).reshape(b, NA_H, rows, GRID_W, NA_D)
    kch, vch = heads(kc), heads(vc)
    q_cols = np.arange(GRID_W).reshape(n_cb, NA_QB_W)
    win_c0 = np.clip(q_cols - WIN_W // 2, 0, GRID_W - WIN_W)
    k_c0 = np.clip(np.arange(n_cb) * NA_QB_W - WIN_W // 2, 0, GRID_W - NA_KB_W)
    k_cols = k_c0[:, None] + np.arange(NA_KB_W)
    col_ok = (k_cols[:, None, :] >= win_c0[:, :, None]) & (k_cols[:, None, :] < win_c0[:, :, None] + WIN_W)
    col_idx = np.clip(k_cols[:, None, :] - q_cols[:, :, None] + WIN_W - 1, 0, 2 * WIN_W - 2)
    n_loc = kh * NA_KB_W
    mask = np.broadcast_to(col_ok[:, :, None, :], (n_cb, NA_QB_W, kh, NA_KB_W)).reshape(n_cb, NA_QB_W, n_loc)

    def gather_band(t, r0):
        band = lax.dynamic_slice_in_dim(t, r0, kh, axis=2)[:, :, :, k_cols]
        return band.transpose(0, 1, 3, 2, 4, 5).reshape(b, NA_H, n_cb, n_loc, NA_D)

    def row_fn(r):
        r0 = jnp.clip(r - kh // 2, 0, rows - kh)
        k_blk, v_blk = gather_band(kg, r0), gather_band(vg, r0)
        q_r = lax.dynamic_index_in_dim(qg, r, axis=2, keepdims=False).reshape(b, NA_H, n_cb, NA_QB_W, NA_D)
        bias = rpb[:, r0 + jnp.arange(kh) - r + WIN_H - 1][:, :, col_idx]
        bias = bias.transpose(0, 2, 3, 1, 4).reshape(NA_H, n_cb, NA_QB_W, n_loc).astype(F32)
        s_loc = jnp.einsum('bhnqd,bhnkd->bhnqk', q_r, k_blk).astype(F32) * scale + bias
        s_loc = jnp.where(mask, s_loc, NEG_INF)
        s_ctx = jnp.einsum('bhnqd,bhkd->bhnqk', q_r, kch).astype(F32) * scale
        p = jax.nn.softmax(jnp.concatenate([s_loc, s_ctx], -1), axis=-1).astype(dt)
        o = (jnp.einsum('bhnqk,bhnkd->bhnqd', p[..., :n_loc], v_blk)
             + jnp.einsum('bhnqk,bhkd->bhnqd', p[..., n_loc:], vch))
        return o.reshape(b, NA_H, GRID_W, NA_D)

    o = lax.map(row_fn, jnp.arange(rows))
    yx = o.transpose(1, 0, 3, 2, 4).reshape(b, S, NA_H * NA_D)
    yc = None
    if with_ctx_out:
        qch = heads(qc)
        s = jnp.einsum('bhqd,bhkd->bhqk', qch, kch).astype(F32) * scale
        p = jax.nn.softmax(s, axis=-1).astype(dt)
        yc = jnp.einsum('bhqk,bhkd->bhqd', p, vch).transpose(0, 2, 1, 3).reshape(b, qc.shape[1], NA_H * NA_D)
    return yx, yc


def _merge(ya, yb, yc, gate_logits, w_branch, w_out):
    ga, gb, gcg = jnp.split(jax.nn.sigmoid(gate_logits.astype(F32)).astype(ya.dtype), N_BRANCH, axis=-1)
    m = ga * (ya @ w_branch[0]) + gb * (yb @ w_branch[1]) + gcg * (yc @ w_branch[2])
    return m @ w_out


def _token_mixer(hx, hc, w_in, conv_w, a_log, dt_bias, gdn_norm_w, lam_vec, lam_init, diff_norm_w,
                 rpb, w_branch, w_out, cos, sin, with_ctx_out):
    px = _split_cols(hx @ w_in)
    pc = _split_cols(hc @ w_in)
    ya_x, ya_c = _gdn_branch(px[:8], pc[:8], conv_w, a_log, dt_bias, gdn_norm_w, with_ctx_out)
    yb_x, yb_c = _diff_branch(px[8:11], pc[8:11], lam_vec, lam_init, diff_norm_w, cos, sin, with_ctx_out)
    yc_x, yc_c = _na_branch(px[11:14], pc[11:14], rpb, with_ctx_out)
    out_x = _merge(ya_x, yb_x, yc_x, px[14], w_branch, w_out)
    out_c = _merge(ya_c, yb_c, yc_c, pc[14], w_branch, w_out) if with_ctx_out else None
    return out_x, out_c


def _moe(h, w_router, b_router, w1, b1, w2, b2):
    T, D = h.shape
    logits = (h @ w_router).astype(F32) + b_router.astype(F32)
    top_v, top_e = lax.top_k(logits, TOP_K)
    gate = jax.nn.softmax(top_v, axis=-1)
    n_asg = T * TOP_K
    flat_e = top_e.reshape(-1)
    flat_tok = jnp.arange(n_asg, dtype=jnp.int32) // TOP_K
    order = jnp.argsort(flat_e)
    e_sorted = flat_e[order]
    counts = jnp.bincount(flat_e, length=N_EXPERTS)
    padded = (counts + MOE_BLOCK - 1) // MOE_BLOCK * MOE_BLOCK
    pad_end = jnp.cumsum(padded)
    pad_start = pad_end - padded
    start = jnp.cumsum(counts) - counts
    dest = pad_start[e_sorted] + jnp.arange(n_asg) - start[e_sorted]
    n_blk = -(-(n_asg + N_EXPERTS * (MOE_BLOCK - 1)) // MOE_BLOCK)
    n_rows = n_blk * MOE_BLOCK
    row_tok = jnp.full((n_rows,), T, jnp.int32).at[dest].set(flat_tok[order])
    row_gate = jnp.zeros((n_rows,), h.dtype).at[dest].set(gate.reshape(-1)[order].astype(h.dtype))
    blk_e = jnp.minimum(jnp.searchsorted(pad_end, jnp.arange(n_blk) * MOE_BLOCK, side='right'), N_EXPERTS - 1)
    h_pad = jnp.concatenate([h, jnp.zeros((1, D), h.dtype)], 0)

    def expert_block(args):
        rows, e = args
        hu = h_pad[rows] @ w1[e] + b1[e]
        x_glu = jnp.minimum(hu[:, :D_EXPERT], SWIGLU_LIMIT)
        x_lin = jnp.clip(hu[:, D_EXPERT:], -SWIGLU_LIMIT, SWIGLU_LIMIT)
        act = x_glu * jax.nn.sigmoid(SWIGLU_ALPHA * x_glu) * (x_lin + 1)
        return act @ w2[e] + b2[e]

    y = lax.map(expert_block, (row_tok.reshape(n_blk, MOE_BLOCK), blk_e))
    y = y.reshape(n_rows, D) * row_gate[:, None]
    return jax.ops.segment_sum(y, row_tok, num_segments=T + 1)[:T]


def setup_inputs(seed: int = 0) -> dict:
    key = jax.random.key(seed)
    ks = jax.random.split(key, 24)

    def nrm(k, shape, s):
        return jax.random.normal(k, shape, F32) * s

    dt = jnp.exp(jax.random.uniform(ks[9], (DEPTH, 2, GDN_H), F32, math.log(1e-3), math.log(1e-1)))
    return {
        'x': nrm(ks[0], (BATCH, SEQ, D_MODEL), 1.0),
        'c': nrm(ks[1], (BATCH, D_MODEL), 1.0),
        'ctx': nrm(ks[2], (BATCH, CTX_LEN, D_MODEL), 1.0),
        'c_ctx': nrm(ks[3], (D_MODEL,), 1.0),
        'w_ada': nrm(ks[4], (DEPTH, D_MODEL, 6 * D_MODEL), D_MODEL ** -0.5),
        'b_ada': nrm(ks[5], (DEPTH, 6 * D_MODEL), 0.02),
        'w_in': nrm(ks[6], (DEPTH, D_MODEL, D_IN), D_MODEL ** -0.5),
        'conv_w': nrm(ks[7], (DEPTH, CONV_K, 2 * GDN_QK + GDN_V), CONV_K ** -0.5),
        'gdn_a_log': jnp.log(jax.random.uniform(ks[8], (DEPTH, 2, GDN_H), F32, 1.0, 16.0)),
        'gdn_dt_bias': dt + jnp.log(-jnp.expm1(-dt)),
        'gdn_norm_w': 1.0 + nrm(ks[10], (DEPTH, GDN_DV), 0.02),
        'diff_lambda': nrm(ks[11], (DEPTH, 4, DIFF_D), 0.1),
        'diff_norm_w': 1.0 + nrm(ks[12], (DEPTH, 2 * DIFF_D), 0.02),
        'na_rpb': nrm(ks[13], (DEPTH, NA_H, 2 * WIN_H - 1, 2 * WIN_W - 1), 0.02),
        'w_branch': nrm(ks[14], (DEPTH, N_BRANCH, BRANCH_W, D_MODEL), BRANCH_W ** -0.5),
        'w_out': nrm(ks[15], (DEPTH, D_MODEL, D_MODEL), D_MODEL ** -0.5 * DN_BETA),
        'ln_g': 1.0 + nrm(ks[16], (DEPTH, 2, D_MODEL), 0.02),
        'ln_b': nrm(ks[17], (DEPTH, 2, D_MODEL), 0.02),
        'w_router': nrm(ks[18], (DEPTH, D_MODEL, N_EXPERTS), D_MODEL ** -0.5),
        'b_router': nrm(ks[19], (DEPTH, N_EXPERTS), 0.01),
        'w_exp1': nrm(ks[20], (DEPTH, N_EXPERTS, D_MODEL, 2 * D_EXPERT), D_MODEL ** -0.5),
        'b_exp1': nrm(ks[21], (DEPTH, N_EXPERTS, 2 * D_EXPERT), 0.01),
        'w_exp2': nrm(ks[22], (DEPTH, N_EXPERTS, D_EXPERT, D_MODEL), D_EXPERT ** -0.5 * DN_BETA),
        'b_exp2': nrm(ks[23], (DEPTH, N_EXPERTS, D_MODEL), 0.01),
    }


def reference(x, c, ctx, c_ctx, w_ada, b_ada, w_in, conv_w, gdn_a_log, gdn_dt_bias, gdn_norm_w,
              diff_lambda, diff_norm_w, na_rpb, w_branch, w_out, ln_g, ln_b, w_router, b_router,
              w_exp1, b_exp1, w_exp2, b_exp2):
    b, S, D = x.shape
    Lc = ctx.shape[1]
    cos, sin = _axial_rope(S, x.dtype)
    for l in range(DEPTH):
        last = l == DEPTH - 1
        lam_init = 0.8 - 0.6 * math.exp(-0.3 * l)
        mod_x = (jax.nn.silu(c) @ w_ada[l] + b_ada[l])[:, None, :]
        mod_c = jax.nn.silu(c_ctx) @ w_ada[l] + b_ada[l]
        sh1, sc1, g1, sh2, sc2, g2 = jnp.split(mod_x, 6, axis=-1)
        csh1, csc1, cg1, csh2, csc2, cg2 = jnp.split(mod_c, 6, axis=-1)
        hx = _modulate(x, sh1, sc1)
        hc = _modulate(ctx, csh1, csc1)
        mx, mc = _token_mixer(hx, hc, w_in[l], conv_w[l], gdn_a_log[l], gdn_dt_bias[l], gdn_norm_w[l],
                              diff_lambda[l], lam_init, diff_norm_w[l], na_rpb[l], w_branch[l], w_out[l],
                              cos, sin, not last)
        x = _post_norm(x, g1 * mx, ln_g[l, 0], ln_b[l, 0])
        hx = _modulate(x, sh2, sc2)
        if last:
            fx = _moe(hx.reshape(-1, D), w_router[l], b_router[l], w_exp1[l], b_exp1[l],
                      w_exp2[l], b_exp2[l]).reshape(b, S, D)
        else:
            ctx = _post_norm(ctx, cg1 * mc, ln_g[l, 0], ln_b[l, 0])
            hc = _modulate(ctx, csh2, csc2)
            f = _moe(jnp.concatenate([hx.reshape(-1, D), hc.reshape(-1, D)], 0), w_router[l], b_router[l],
                     w_exp1[l], b_exp1[l], w_exp2[l], b_exp2[l])
            fx = f[:b * S].reshape(b, S, D)
            ctx = _post_norm(ctx, cg2 * f[b * S:].reshape(b, Lc, D), ln_g[l, 1], ln_b[l, 1])
        x = _post_norm(x, g2 * fx, ln_g[l, 1], ln_b[l, 1])
    return x
```

```python
import functools
import math

import numpy as np
import jax
import jax.numpy as jnp
from jax import lax
from jax.experimental import pallas as pl
from jax.experimental.pallas import tpu as pltpu

F32 = jnp.float32
BF16 = jnp.bfloat16

GRID_W = 64
GDN_H = 4
GDN_DK = 128
GDN_CHUNK = 64
DIFF_H = 4
DIFF_D = 64
ROPE_BASE = 10000.0
NA_H = 8
NA_D = 64
WIN_H = 8
WIN_W = 16
N_EXPERTS = 32
TOP_K = 4
D_EXPERT = 1024
SWIGLU_LIMIT = 7.0
SWIGLU_ALPHA = 1.702
DN_ALPHA = 8.0 ** 0.25
LN_EPS = 1e-5
RMS_EPS = 1e-6
NEG_INF = -1e30

LANES = 128
TOK_TILE = 256
NA_QROWS = 8
NA_KROWS = 16
MOE_ROWS = 256
INV_BASE_LOG2 = 3
VMEM_LIMIT = 56 * 1024 * 1024

C_GQ, C_GK, C_GV, C_GZ = 0, 512, 1024, 1536
C_DQ, C_DK, C_DV = 2048, 2560, 3072
C_NQ, C_NK, C_NV = 3584, 4096, 4608
C_GATE = 5120
C_AB = 8192
P_COLS = 8320


def _cparams(sem):
    return pltpu.CompilerParams(dimension_semantics=sem, vmem_limit_bytes=VMEM_LIMIT)


def _ln(x):
    mu = jnp.mean(x, -1, keepdims=True)
    xc = x - mu
    var = jnp.mean(xc * xc, -1, keepdims=True)
    return xc * lax.rsqrt(var + LN_EPS)


def _sigmoid(x):
    return 1.0 / (1.0 + jnp.exp(-x))


def _dot(a, b):
    return jnp.dot(a, b, preferred_element_type=F32)


def _dot_nt(a, b):
    return lax.dot_general(a, b, (((1,), (1,)), ((), ())), preferred_element_type=F32)


def _dot_tn(a, b):
    return lax.dot_general(a, b, (((0,), (0,)), ((), ())), preferred_element_type=F32)


def _mm(a, b):
    return jnp.dot(a.astype(BF16), b.astype(BF16), preferred_element_type=F32)


def _dot_f32(a, b):
    return jnp.dot(a, b, preferred_element_type=F32, precision=lax.Precision.HIGHEST)


def _ada_kernel(c_ref, w_ref, b_ref, o_ref):
    c = c_ref[...]
    s = (c * _sigmoid(c)).astype(BF16)
    o_ref[0] = _dot(s, w_ref[0].astype(BF16)) + b_ref[0]


def _ada(cmat, w_ada, b_ada):
    depth, d, n = w_ada.shape
    tn = n // 4
    return pl.pallas_call(
        _ada_kernel,
        grid=(depth, n // tn),
        in_specs=[pl.BlockSpec((8, d), lambda l, j: (0, 0)),
                  pl.BlockSpec((1, d, tn), lambda l, j: (l, 0, j)),
                  pl.BlockSpec((1, 1, tn), lambda l, j: (l, 0, j))],
        out_specs=pl.BlockSpec((1, 8, tn), lambda l, j: (l, 0, j)),
        out_shape=jax.ShapeDtypeStruct((depth, 8, n), F32),
        compiler_params=_cparams(("parallel", "parallel")),
        name="ada",
    )(cmat, w_ada, b_ada.reshape(depth, 1, n))


def _norm_kernel(*refs, has_y, has_h):
    refs = list(refs)
    x = refs.pop(0)[...]
    if has_y:
        y_ref, g_ref, lg_ref, lb_ref = refs[:4]
        del refs[:4]
        x = _ln(DN_ALPHA * x + g_ref[...] * y_ref[...]) * lg_ref[...] + lb_ref[...]
    if has_h:
        sh_ref, sc_ref = refs[:2]
        del refs[:2]
    if has_y:
        refs.pop(0)[...] = x
    if has_h:
        refs.pop(0)[...] = (_ln(x) * (1.0 + sc_ref[...]) + sh_ref[...]).astype(BF16)


def _mod_spec(d, nl, k):
    return pl.BlockSpec((None, None, 1, d), lambda b, i: (b, i // nl, 0, k))


def _norm(x, nl, *, y=None, gate=None, ln_g=None, ln_b=None, shift=None):
    bsz, lt, d = x.shape
    has_y = y is not None
    has_h = shift is not None
    tok = pl.BlockSpec((None, TOK_TILE, d), lambda b, i: (b, i, 0))
    vec = pl.BlockSpec((1, d), lambda b, i: (0, 0))
    args, specs, outs, ospecs = [x], [tok], [], []
    if has_y:
        args += [y, gate[0], ln_g.reshape(1, d), ln_b.reshape(1, d)]
        specs += [tok, _mod_spec(d, nl, gate[1]), vec, vec]
        outs.append(jax.ShapeDtypeStruct((bsz, lt, d), F32))
        ospecs.append(tok)
    if has_h:
        args += [shift[0], shift[0]]
        specs += [_mod_spec(d, nl, shift[1]), _mod_spec(d, nl, shift[1] + 1)]
        outs.append(jax.ShapeDtypeStruct((bsz, lt, d), BF16))
        ospecs.append(tok)
    return pl.pallas_call(
        functools.partial(_norm_kernel, has_y=has_y, has_h=has_h),
        grid=(bsz, lt // TOK_TILE),
        in_specs=specs, out_specs=ospecs, out_shape=outs,
        compiler_params=_cparams(("parallel", "parallel")),
        name="norm",
    )(*args)


def _mm_kernel(a_ref, w_ref, o_ref):
    o_ref[...] = _dot(a_ref[...], w_ref[...]).astype(o_ref.dtype)


def _matmul(a, w, tm, tn, out_dtype):
    m, k = a.shape
    n = w.shape[1]
    return pl.pallas_call(
        _mm_kernel,
        grid=(n // tn, m // tm),
        in_specs=[pl.BlockSpec((tm, k), lambda j, i: (i, 0)),
                  pl.BlockSpec((k, tn), lambda j, i: (0, j))],
        out_specs=pl.BlockSpec((tm, tn), lambda j, i: (i, j)),
        out_shape=jax.ShapeDtypeStruct((m, n), out_dtype),
        compiler_params=_cparams(("parallel", "parallel")),
        name="matmul",
    )(a, w)


def _gdn_prep_kernel(x_ref, hp_ref, hn_ref, ab_ref, cw_ref, alog_ref, dtb_ref,
                     qkv_ref, gc_ref, gct_ref, *, nl):
    i = pl.program_id(1)
    x = x_ref[...]
    t = x.shape[0]
    prev_ok = jnp.logical_and(i != 0, i != nl)
    next_ok = jnp.logical_and(i != nl - 1, i != nl)
    prow = jnp.where(prev_ok, hp_ref[7:8, :], 0.0)
    nrow = jnp.where(next_ok, hn_ref[0:1, :], 0.0)
    rid = lax.broadcasted_iota(jnp.int32, x.shape, 0)
    xp = jnp.where(rid == 0, prow, pltpu.roll(x, 1, 0))
    xn = jnp.where(rid == t - 1, nrow, pltpu.roll(x, t - 1, 0))
    w = cw_ref[...]
    y = xp * w[0:1] + x * w[1:2] + xn * w[2:3]
    y = y * _sigmoid(y)
    for g in range(12):
        blk = y[:, g * LANES:(g + 1) * LANES]
        if g < 8:
            blk = blk * lax.rsqrt(jnp.sum(blk * blk, -1, keepdims=True) + RMS_EPS)
        if g < 4:
            blk = blk * (GDN_DK ** -0.5)
        qkv_ref[:, g * LANES:(g + 1) * LANES] = blk

    ab = ab_ref[...]
    lane = lax.broadcasted_iota(jnp.int32, ab.shape, 1)
    xs = ab + dtb_ref[...]
    softplus = jnp.maximum(xs, 0.0) + jnp.log(1.0 + jnp.exp(-jnp.abs(xs)))
    g = -jnp.exp(alog_ref[...]) * softplus
    beta = _sigmoid(ab)
    is_a = (lane % 8) < 4
    gb = jnp.where(lane < 16, jnp.where(is_a, g, beta), 0.0)
    r = lax.broadcasted_iota(jnp.int32, (t, t), 0)
    c = lax.broadcasted_iota(jnp.int32, (t, t), 1)
    same = (r // GDN_CHUNK) == (c // GDN_CHUNK)
    tri_l = jnp.where(jnp.logical_and(same, c <= r), 1.0, 0.0)
    tri_u = jnp.where(jnp.logical_and(same, c >= r), 1.0, 0.0)
    cf = _dot_f32(tri_l, gb)
    cb = _dot_f32(tri_u, gb)
    gc = jnp.where(lane < 4, cf, jnp.where(jnp.logical_and(lane >= 8, lane < 12), cb, gb))
    gc_ref[...] = gc
    gct = gc.T
    for ch in range(t // GDN_CHUNK):
        gct_ref[ch] = gct[0:16, ch * GDN_CHUNK:(ch + 1) * GDN_CHUNK]


def _gdn_prep(p, conv_w, alog_row, dtb_row, nl):
    bsz, lt, _ = p.shape
    nt = lt // TOK_TILE
    w3 = 3 * 512
    rb = TOK_TILE // 8
    return pl.pallas_call(
        functools.partial(_gdn_prep_kernel, nl=nl),
        grid=(bsz, nt),
        in_specs=[pl.BlockSpec((None, TOK_TILE, w3), lambda b, i: (b, i, 0)),
                  pl.BlockSpec((None, 8, w3), lambda b, i: (b, jnp.maximum(i * rb - 1, 0), 0)),
                  pl.BlockSpec((None, 8, w3), lambda b, i: (b, jnp.minimum(i * rb + rb, lt // 8 - 1), 0)),
                  pl.BlockSpec((None, TOK_TILE, LANES), lambda b, i: (b, i, C_AB // LANES)),
                  pl.BlockSpec((3, w3), lambda b, i: (0, 0)),
                  pl.BlockSpec((1, LANES), lambda b, i: (0, 0)),
                  pl.BlockSpec((1, LANES), lambda b, i: (0, 0))],
        out_specs=[pl.BlockSpec((None, TOK_TILE, w3), lambda b, i: (b, i, 0)),
                   pl.BlockSpec((None, TOK_TILE, LANES), lambda b, i: (b, i, 0)),
                   pl.BlockSpec((None, TOK_TILE // GDN_CHUNK, 16, GDN_CHUNK), lambda b, i: (b, i, 0, 0))],
        out_shape=[jax.ShapeDtypeStruct((bsz, lt, w3), F32),
                   jax.ShapeDtypeStruct((bsz, lt, LANES), F32),
                   jax.ShapeDtypeStruct((bsz, lt // GDN_CHUNK, 16, GDN_CHUNK), F32)],
        compiler_params=_cparams(("parallel", "parallel")),
        name="gdn_prep",
    )(p, p, p, p, conv_w, alog_row, dtb_row)


def _gdn_chain(d, h, qkv, gc, gct, s_ref):
    cs = GDN_CHUNK
    q = qkv[:, C_GQ + h * LANES:C_GQ + (h + 1) * LANES]
    k = qkv[:, C_GK + h * LANES:C_GK + (h + 1) * LANES]
    v = qkv[:, C_GV + h * LANES:C_GV + (h + 1) * LANES]
    la = d * 8 + h
    lb = la + 4
    gcol = gc[:, la:la + 1]
    bcol = gc[:, lb:lb + 1]
    grow = gct[la:la + 1, :]
    ri = lax.broadcasted_iota(jnp.int32, (cs, cs), 0)
    ci = lax.broadcasted_iota(jnp.int32, (cs, cs), 1)
    if d == 0:
        incl, strict = ri >= ci, ri > ci
        glast = gcol[cs - 1:cs, :]
    else:
        incl, strict = ri <= ci, ri < ci
        glast = gcol[0:1, :]
    decay = jnp.where(incl, jnp.exp(jnp.where(incl, gcol - grow, 0.0)), 0.0)
    kb = k * bcol
    k16 = k.astype(BF16)
    amat = jnp.where(strict, _dot_nt(kb.astype(BF16), k16) * decay, 0.0)
    qk = _dot_nt(q.astype(BF16), k16) * decay
    eg = jnp.exp(gcol)
    rhs = jnp.concatenate([v * bcol, kb * eg], axis=1)
    eye = ri == ci
    nd = jnp.where((ri >> INV_BASE_LOG2) == (ci >> INV_BASE_LOG2), -amat, 0.0)
    rmat, mmat = nd, nd
    for _ in range(INV_BASE_LOG2 - 1):
        mmat = _mm(mmat, mmat)
        rmat = rmat + mmat + _mm(rmat, mmat)
    tmat = jnp.where(eye, 1.0, rmat)
    lo, hi = (ri, ci) if d == 0 else (ci, ri)
    for lb in range(INV_BASE_LOG2, int(math.log2(cs))):
        off = jnp.logical_and((lo >> lb) == (hi >> lb) + 1, (ri >> (lb + 1)) == (ci >> (lb + 1)))
        tmat = tmat - _mm(tmat, _mm(jnp.where(off, amat, 0.0), tmat))
    sol = rhs + _mm(jnp.where(eye, 0.0, tmat), rhs)
    u, w = sol[:, :LANES], sol[:, LANES:]
    ktail = k * jnp.exp(glast - gcol)
    qg = q * eg
    idx = d * GDN_H + h
    s = s_ref[idx]
    s16 = s.astype(BF16)
    wq = _dot(jnp.concatenate([w, qg], axis=0).astype(BF16), s16)
    vnew = u - wq[:cs]
    v16 = vnew.astype(BF16)
    o = wq[cs:] + _dot(qk.astype(BF16), v16)
    s_ref[idx] = s * jnp.exp(glast) + _dot_tn(ktail.astype(BF16), v16)
    return o


def _gdn_scan_kernel(qf_ref, qb_ref, gf_ref, gb_ref, tf_ref, tb_ref, of_ref, ob_ref, s_ref):
    @pl.when(pl.program_id(1) == 0)
    def _():
        s_ref[...] = jnp.zeros_like(s_ref)

    for d, (q_ref, g_ref, t_ref, o_ref) in enumerate(((qf_ref, gf_ref, tf_ref, of_ref),
                                                      (qb_ref, gb_ref, tb_ref, ob_ref))):
        qkv = q_ref[...]
        gc = g_ref[...]
        gct = t_ref[...]
        for h in range(GDN_H):
            o_ref[:, h * LANES:(h + 1) * LANES] = _gdn_chain(d, h, qkv, gc, gct, s_ref)


def _gdn_scan(qkv, gc, gct, n_lat_chunks):
    bsz, lt, w3 = qkv.shape
    cs = GDN_CHUNK
    nc = lt // cs
    ncc = nc - n_lat_chunks

    def cf(i):
        return jnp.where(i < ncc, n_lat_chunks + i, i - ncc)

    def cb(i):
        return nc - 1 - i

    def tok(c, width):
        return pl.BlockSpec((None, cs, width), lambda b, i: (b, c(i), 0))

    def row(c):
        return pl.BlockSpec((None, None, 16, cs), lambda b, i: (b, c(i), 0, 0))

    return pl.pallas_call(
        _gdn_scan_kernel,
        grid=(bsz, nc),
        in_specs=[tok(cf, w3), tok(cb, w3), tok(cf, LANES), tok(cb, LANES), row(cf), row(cb)],
        out_specs=[tok(cf, 512), tok(cb, 512)],
        out_shape=[jax.ShapeDtypeStruct((bsz, lt, 512), F32)] * 2,
        scratch_shapes=[pltpu.VMEM((2 * GDN_H, GDN_DK, LANES), F32)],
        compiler_params=_cparams(("parallel", "arbitrary")),
        name="gdn_scan",
    )(qkv, qkv, gc, gc, gct, gct)


def _attn_prep_kernel(dq_ref, dk_ref, dv_ref, nq_ref, nk_ref, nv_ref, cos_ref, sin_ref, o_ref):
    cos = cos_ref[...]
    sin = sin_ref[...]
    lane = lax.broadcasted_iota(jnp.int32, cos.shape, 1)
    first = (lane % DIFF_D) < DIFF_D // 2

    def rope(x):
        rot = jnp.where(first, pltpu.roll(x, LANES - DIFF_D // 2, 1), pltpu.roll(x, DIFF_D // 2, 1))
        return x * cos + rot * sin

    for h in range(DIFF_H):
        sl = slice(h * LANES, (h + 1) * LANES)
        o_ref[:, h * LANES:(h + 1) * LANES] = (rope(dq_ref[:, sl]) * (DIFF_D ** -0.5)).astype(BF16)
        o_ref[:, 512 + h * LANES:512 + (h + 1) * LANES] = rope(dk_ref[:, sl]).astype(BF16)
    o_ref[:, 1024:1536] = dv_ref[...].astype(BF16)
    o_ref[:, 1536:2048] = (nq_ref[...] * (NA_D ** -0.5)).astype(BF16)
    o_ref[:, 2048:2560] = nk_ref[...].astype(BF16)
    o_ref[:, 2560:3072] = nv_ref[...].astype(BF16)


def _attn_prep(p, cos, sin):
    bsz, lt, _ = p.shape

    def col(cb):
        return pl.BlockSpec((None, TOK_TILE, 512), lambda b, i: (b, i, cb))

    tab = pl.BlockSpec((TOK_TILE, LANES), lambda b, i: (i, 0))
    return pl.pallas_call(
        _attn_prep_kernel,
        grid=(bsz, lt // TOK_TILE),
        in_specs=[col(C_DQ // 512), col(C_DK // 512), col(C_DV // 512),
                  col(C_NQ // 512), col(C_NK // 512), col(C_NV // 512), tab, tab],
        out_specs=pl.BlockSpec((None, TOK_TILE, 3072), lambda b, i: (b, i, 0)),
        out_shape=jax.ShapeDtypeStruct((bsz, lt, 3072), BF16),
        compiler_params=_cparams(("parallel", "parallel")),
        name="attn_prep",
    )(p, p, p, p, p, p, cos, sin)


def _diff_kernel(*refs, nkv, lam_init, aliased):
    if aliased:
        refs = refs[1:]
    lam_ref, q_ref, k_ref, v_ref, nw_ref, o_ref, m_sc, l_sc, acc_sc = refs
    kv = pl.program_id(3)

    @pl.when(kv == 0)
    def _():
        m_sc[...] = jnp.full_like(m_sc, NEG_INF)
        l_sc[...] = jnp.zeros_like(l_sc)
        acc_sc[...] = jnp.zeros_like(acc_sc)

    q = q_ref[...]
    k = k_ref[...]
    v = v_ref[...]
    for m in range(2):
        s = _dot_nt(q[:, m * DIFF_D:(m + 1) * DIFF_D], k[:, m * DIFF_D:(m + 1) * DIFF_D])
        m_prev = m_sc[m]
        m_new = jnp.maximum(m_prev, jnp.max(s, -1, keepdims=True))
        p = jnp.exp(s - m_new)
        alpha = jnp.exp(m_prev - m_new)
        l_sc[m] = alpha * l_sc[m] + jnp.sum(p, -1, keepdims=True)
        acc_sc[m] = alpha * acc_sc[m] + _dot(p.astype(BF16), v)
        m_sc[m] = m_new

    @pl.when(kv == nkv - 1)
    def _():
        lam = lam_ref[0:1, 0:1]
        o = acc_sc[0] / l_sc[0] - lam * (acc_sc[1] / l_sc[1])
        y = o * lax.rsqrt(jnp.mean(o * o, -1, keepdims=True) + RMS_EPS) * nw_ref[...] * (1.0 - lam_init)
        o_ref[...] = y.astype(BF16)


def _diff_attn(a, lam_row, norm_w, lam_init, *, tq, tkv, q0, nq, k0, nkv, prev=None):
    bsz, lt, _ = a.shape
    aliased = prev is not None
    specs = [pl.BlockSpec((1, LANES), lambda b, h, i, j: (0, 0)),
             pl.BlockSpec((None, tq, LANES), lambda b, h, i, j: (b, q0 + i, h)),
             pl.BlockSpec((None, tkv, LANES), lambda b, h, i, j: (b, k0 + j, 4 + h)),
             pl.BlockSpec((None, tkv, LANES), lambda b, h, i, j: (b, k0 + j, 8 + h)),
             pl.BlockSpec((1, LANES), lambda b, h, i, j: (0, 0))]
    args = [lam_row, a, a, a, norm_w.reshape(1, LANES)]
    if aliased:
        specs = [pl.BlockSpec(memory_space=pl.ANY)] + specs
        args = [prev] + args
    return pl.pallas_call(
        functools.partial(_diff_kernel, nkv=nkv, lam_init=lam_init, aliased=aliased),
        grid=(bsz, DIFF_H, nq, nkv),
        in_specs=specs,
        out_specs=pl.BlockSpec((None, tq, LANES), lambda b, h, i, j: (b, q0 + i, h)),
        out_shape=jax.ShapeDtypeStruct((bsz, lt, 512), BF16),
        scratch_shapes=[pltpu.VMEM((2, tq, 1), F32), pltpu.VMEM((2, tq, 1), F32),
                        pltpu.VMEM((2, tq, LANES), F32)],
        input_output_aliases={0: 0} if aliased else {},
        compiler_params=_cparams(("parallel", "parallel", "parallel", "arbitrary")),
        name="diff_attn",
    )(*args)


def _na_kernel(q_ref, k_ref, v_ref, bias_ref, o_ref, *, s_lat, lc):
    i = pl.program_id(2)
    tq = q_ref.shape[0]
    nk = NA_KROWS * GRID_W
    kstart = jnp.clip(i * tq - (NA_KROWS - NA_QROWS) // 2 * GRID_W, 0, s_lat - nk)
    kstart = pl.multiple_of(kstart, TOK_TILE)
    q = q_ref[...]
    k_loc = k_ref[pl.ds(kstart, nk), :]
    v_loc = v_ref[pl.ds(kstart, nk), :]
    k_ctx = k_ref[s_lat:s_lat + lc, :]
    v_ctx = v_ref[s_lat:s_lat + lc, :]
    outs = []
    for j in range(2):
        sl = slice(j * NA_D, (j + 1) * NA_D)
        qh = q[:, sl]
        s_loc = _dot_nt(qh, k_loc[:, sl]) + bias_ref[j]
        s_ctx = _dot_nt(qh, k_ctx[:, sl])
        m = jnp.maximum(jnp.max(s_loc, -1, keepdims=True), jnp.max(s_ctx, -1, keepdims=True))
        p_loc = jnp.exp(s_loc - m)
        p_ctx = jnp.exp(s_ctx - m)
        l = jnp.sum(p_loc, -1, keepdims=True) + jnp.sum(p_ctx, -1, keepdims=True)
        o = _dot(p_loc.astype(BF16), v_loc[:, sl]) + _dot(p_ctx.astype(BF16), v_ctx[:, sl])
        outs.append(o / l)
    o_ref[...] = jnp.concatenate(outs, axis=1).astype(BF16)


def _na_attn(a, bias, s_lat, lc):
    bsz, lt, _ = a.shape
    tq = NA_QROWS * GRID_W
    nq = s_lat // tq

    def variant(i):
        return jnp.where(i == 0, 0, jnp.where(i == nq - 1, 2, 1))

    return pl.pallas_call(
        functools.partial(_na_kernel, s_lat=s_lat, lc=lc),
        grid=(bsz, NA_H // 2, nq),
        in_specs=[pl.BlockSpec((None, tq, LANES), lambda b, h, i: (b, i, 12 + h)),
                  pl.BlockSpec((None, lt, LANES), lambda b, h, i: (b, 0, 16 + h)),
                  pl.BlockSpec((None, lt, LANES), lambda b, h, i: (b, 0, 20 + h)),
                  pl.BlockSpec((None, 2, tq, NA_KROWS * GRID_W), lambda b, h, i: (variant(i), h, 0, 0))],
        out_specs=pl.BlockSpec((None, tq, LANES), lambda b, h, i: (b, i, h)),
        out_shape=jax.ShapeDtypeStruct((bsz, lt, 512), BF16),
        compiler_params=_cparams(("parallel", "parallel", "arbitrary")),
        name="na_attn",
    )(a, a, a, bias)


def _na_ctx_kernel(prev_ref, q_ref, k_ref, v_ref, o_ref):
    del prev_ref
    q = q_ref[...]
    k = k_ref[...]
    v = v_ref[...]
    outs = []
    for j in range(2):
        sl = slice(j * NA_D, (j + 1) * NA_D)
        s = _dot_nt(q[:, sl], k[:, sl])
        p = jnp.exp(s - jnp.max(s, -1, keepdims=True))
        outs.append(_dot(p.astype(BF16), v[:, sl]) / jnp.sum(p, -1, keepdims=True))
    o_ref[...] = jnp.concatenate(outs, axis=1).astype(BF16)


def _na_ctx_attn(a, prev, s_lat, lc):
    bsz, lt, _ = a.shape
    rb = s_lat // lc

    def blk(c0):
        return pl.BlockSpec((None, lc, LANES), lambda b, h: (b, rb, c0 + h))

    return pl.pallas_call(
        _na_ctx_kernel,
        grid=(bsz, NA_H // 2),
        in_specs=[pl.BlockSpec(memory_space=pl.ANY), blk(12), blk(16), blk(20)],
        out_specs=blk(0),
        out_shape=jax.ShapeDtypeStruct((bsz, lt, 512), BF16),
        input_output_aliases={0: 0},
        compiler_params=_cparams(("parallel", "parallel")),
        name="na_ctx_attn",
    )(prev, a, a, a)


def _na_bias_tables(rpb, rows):
    qr = np.arange(NA_QROWS)[:, None]
    kr = np.arange(NA_KROWS)[None, :]
    qc = np.arange(GRID_W)[:, None]
    kc = np.arange(GRID_W)[None, :]
    w0 = np.clip(qc - WIN_W // 2, 0, GRID_W - WIN_W)
    okc = (kc >= w0) & (kc < w0 + WIN_W)
    dc = np.clip(kc - qc + WIN_W - 1, 0, 2 * WIN_W - 2)
    tabs = []
    half = (NA_KROWS - NA_QROWS) // 2
    for r_start, delta in ((0, 0), (NA_QROWS, -half), (rows - NA_QROWS, -2 * half)):
        r = r_start + qr
        kabs = r_start + delta + kr
        r0 = np.clip(r - WIN_H // 2, 0, rows - WIN_H)
        okr = (kabs >= r0) & (kabs < r0 + WIN_H)
        dr = np.clip(kabs - r + WIN_H - 1, 0, 2 * WIN_H - 2)
        ok = okr[:, None, :, None] & okc[None, :, None, :]
        vals = rpb[:, dr[:, None, :, None], dc[None, :, None, :]]
        tab = jnp.where(ok[None], vals, NEG_INF)
        tabs.append(tab.reshape(rpb.shape[0], NA_QROWS * GRID_W, NA_KROWS * GRID_W))
    return jnp.stack(tabs).astype(F32)


def _merge_kernel(of_ref, ob_ref, z_ref, yb_ref, yc_ref, ga_ref, gb_ref, gc_ref, x_ref,
                  gnw_ref, wb_ref, wo_ref, g1_ref, lg_ref, lb_ref, sh_ref, sc_ref, wr_ref, br_ref,
                  xo_ref, h_ref, lg_out_ref):
    o = of_ref[...] + ob_ref[...]
    z = z_ref[...]
    gnw = gnw_ref[...]
    parts = []
    for h in range(GDN_H):
        sl = slice(h * LANES, (h + 1) * LANES)
        oh = o[:, sl]
        zh = z[:, sl]
        parts.append(oh * lax.rsqrt(jnp.mean(oh * oh, -1, keepdims=True) + RMS_EPS) * gnw * (zh * _sigmoid(zh)))
    ya = jnp.concatenate(parts, axis=1).astype(BF16)
    m = (_sigmoid(ga_ref[...]) * _dot(ya, wb_ref[0])
         + _sigmoid(gb_ref[...]) * _dot(yb_ref[...], wb_ref[1])
         + _sigmoid(gc_ref[...]) * _dot(yc_ref[...], wb_ref[2]))
    mx = _dot(m.astype(BF16), wo_ref[...])
    x = _ln(DN_ALPHA * x_ref[...] + g1_ref[...] * mx) * lg_ref[...] + lb_ref[...]
    xo_ref[...] = x
    hh = _ln(x) * (1.0 + sc_ref[...]) + sh_ref[...]
    h_ref[...] = hh.astype(BF16)
    lg_out_ref[...] = _dot_f32(hh, wr_ref[...]) + br_ref[...]


def _merge(o_f, o_b, p, yb, yc, x, gnw, wb, wo, modsel, nl, ln_g, ln_b, wr, br):
    bsz, lt, d = x.shape

    def tok(width, cb):
        return pl.BlockSpec((None, TOK_TILE, width), lambda b, i: (b, i, cb))

    def const(shape):
        return pl.BlockSpec(shape, lambda b, i: (0,) * len(shape))

    return pl.pallas_call(
        _merge_kernel,
        grid=(bsz, lt // TOK_TILE),
        in_specs=[tok(512, 0), tok(512, 0), tok(512, C_GZ // 512), tok(512, 0), tok(512, 0),
                  tok(d, C_GATE // d), tok(d, C_GATE // d + 1), tok(d, C_GATE // d + 2), tok(d, 0),
                  const((1, LANES)), const((3, 512, d)), const((d, d)),
                  _mod_spec(d, nl, 2), const((1, d)), const((1, d)),
                  _mod_spec(d, nl, 3), _mod_spec(d, nl, 4),
                  const((d, LANES)), const((1, LANES))],
        out_specs=[tok(d, 0), tok(d, 0), tok(LANES, 0)],
        out_shape=[jax.ShapeDtypeStruct((bsz, lt, d), F32),
                   jax.ShapeDtypeStruct((bsz, lt, d), BF16),
                   jax.ShapeDtypeStruct((bsz, lt, LANES), F32)],
        compiler_params=_cparams(("parallel", "parallel")),
        name="merge",
    )(o_f, o_b, p, yb, yc, p, p, p, x, gnw.reshape(1, LANES), wb, wo,
      modsel, ln_g.reshape(1, d), ln_b.reshape(1, d), modsel, modsel, wr, br)


def _ffn_kernel(be_ref, x_ref, w1_ref, b1_ref, w2_ref, b2_ref, o_ref, w1c, w2c):
    i = pl.program_id(0)
    changed = jnp.logical_or(i == 0, be_ref[i] != be_ref[jnp.maximum(i - 1, 0)])

    @pl.when(changed)
    def _():
        w1c[...] = w1_ref[...].astype(BF16)
        w2c[...] = w2_ref[...].astype(BF16)

    hu = _dot(x_ref[...], w1c[...]) + b1_ref[...]
    x_glu = jnp.minimum(hu[:, :D_EXPERT], SWIGLU_LIMIT)
    x_lin = jnp.clip(hu[:, D_EXPERT:], -SWIGLU_LIMIT, SWIGLU_LIMIT)
    act = x_glu * _sigmoid(SWIGLU_ALPHA * x_glu) * (x_lin + 1.0)
    o_ref[...] = _dot(act.astype(BF16), w2c[...]) + b2_ref[...]


def _ffn(blk_e, xs, w1, b1, w2, b2):
    n_rows, d = xs.shape
    n_blk = n_rows // MOE_ROWS
    de2 = w1.shape[-1]
    gs = pltpu.PrefetchScalarGridSpec(
        num_scalar_prefetch=1,
        grid=(n_blk,),
        in_specs=[pl.BlockSpec((MOE_ROWS, d), lambda i, be: (i, 0)),
                  pl.BlockSpec((None, d, de2), lambda i, be: (be[i], 0, 0)),
                  pl.BlockSpec((None, 1, de2), lambda i, be: (be[i], 0, 0)),
                  pl.BlockSpec((None, de2 // 2, d), lambda i, be: (be[i], 0, 0)),
                  pl.BlockSpec((None, 1, d), lambda i, be: (be[i], 0, 0))],
        out_specs=pl.BlockSpec((MOE_ROWS, d), lambda i, be: (i, 0)),
        scratch_shapes=[pltpu.VMEM((d, de2), BF16), pltpu.VMEM((de2 // 2, d), BF16)],
    )
    return pl.pallas_call(
        _ffn_kernel,
        grid_spec=gs,
        out_shape=jax.ShapeDtypeStruct((n_rows, d), F32),
        compiler_params=_cparams(("arbitrary",)),
        name="ffn",
    )(blk_e, xs, w1, b1.reshape(N_EXPERTS, 1, de2), w2, b2.reshape(N_EXPERTS, 1, d))


def _moe(h2, logits, w1, b1, w2, b2):
    t, d = h2.shape
    top_v, top_e = lax.top_k(logits[:, :N_EXPERTS], TOP_K)
    gate = jax.nn.softmax(top_v, axis=-1)
    n_asg = t * TOP_K
    flat_e = top_e.reshape(-1)
    flat_tok = jnp.arange(n_asg, dtype=jnp.int32) // TOP_K
    order = jnp.argsort(flat_e)
    e_sorted = flat_e[order]
    counts = jnp.bincount(flat_e, length=N_EXPERTS)
    padded = (counts + MOE_ROWS - 1) // MOE_ROWS * MOE_ROWS
    pad_end = jnp.cumsum(padded)
    pad_start = pad_end - padded
    start = jnp.cumsum(counts) - counts
    dest = pad_start[e_sorted] + jnp.arange(n_asg) - start[e_sorted]
    n_blk = -(-(n_asg + N_EXPERTS * (MOE_ROWS - 1)) // MOE_ROWS)
    n_rows = n_blk * MOE_ROWS
    row_tok = jnp.full((n_rows,), t, jnp.int32).at[dest].set(flat_tok[order])
    row_gate = jnp.zeros((n_rows,), F32).at[dest].set(gate.reshape(-1)[order])
    blk_e = jnp.minimum(jnp.searchsorted(pad_end, jnp.arange(n_blk) * MOE_ROWS, side='right'),
                        N_EXPERTS - 1).astype(jnp.int32)
    h_pad = jnp.concatenate([h2, jnp.zeros((1, d), h2.dtype)], 0)
    y = _ffn(blk_e, h_pad[row_tok], w1, b1, w2, b2)
    y = y * row_gate[:, None]
    return jax.ops.segment_sum(y, row_tok, num_segments=t + 1)[:t]


def _rope_tables(s_lat, lc):
    t = jnp.arange(s_lat)
    row = (t // GRID_W).astype(F32)
    col = (t % GRID_W).astype(F32)
    n_freq = DIFF_D // 4
    inv = ROPE_BASE ** (-jnp.arange(n_freq, dtype=F32) / n_freq)
    ang = jnp.concatenate([row[:, None] * inv, col[:, None] * inv], -1)
    ang = jnp.concatenate([ang, ang, ang, ang], -1)
    sign = jnp.where((jnp.arange(LANES) % DIFF_D) < DIFF_D // 2, -1.0, 1.0)
    cos = jnp.concatenate([jnp.cos(ang), jnp.ones((lc, LANES), F32)], 0)
    sin = jnp.concatenate([jnp.sin(ang) * sign, jnp.zeros((lc, LANES), F32)], 0)
    return cos.astype(F32), sin.astype(F32)


def kernel(x, c, ctx, c_ctx, w_ada, b_ada, w_in, conv_w, gdn_a_log, gdn_dt_bias, gdn_norm_w, diff_lambda, diff_norm_w, na_rpb, w_branch, w_out, ln_g, ln_b, w_router, b_router, w_exp1, b_exp1, w_exp2, b_exp2):
    bsz, s_lat, d = x.shape
    lc = ctx.shape[1]
    depth = w_ada.shape[0]
    assert lc == TOK_TILE and s_lat % (NA_QROWS * GRID_W) == 0 and d == 1024
    lt = s_lat + lc
    nl = s_lat // TOK_TILE
    rows = s_lat // GRID_W

    xs = jnp.concatenate([x, ctx], axis=1)
    cmat = jnp.zeros((8, d), F32).at[:bsz].set(c).at[bsz].set(c_ctx)
    mod = _ada(cmat, w_ada, b_ada)
    cos, sin = _rope_tables(s_lat, lc)

    cols = np.concatenate([np.arange(0, 2048), np.arange(2064, 2064 + 3072 + 3072), np.arange(2048, 2064)])
    w_in_r = jnp.pad(w_in[:, :, cols], ((0, 0), (0, 0), (0, P_COLS - len(cols)))).astype(BF16)
    wb16 = w_branch.astype(BF16)
    wo16 = w_out.astype(BF16)
    wr_pad = jnp.pad(w_router, ((0, 0), (0, 0), (0, LANES - N_EXPERTS)))
    br_pad = jnp.pad(b_router, ((0, 0), (0, LANES - N_EXPERTS)), constant_values=NEG_INF).reshape(depth, 1, LANES)
    lane_pad = LANES - 4 * GDN_H

    modsels = [jnp.stack([mod[l, :bsz], jnp.broadcast_to(mod[l, bsz], (bsz, 6 * d))], axis=1)[:, :, None, :]
               for l in range(depth)]
    (h1,) = _norm(xs, nl, shift=(modsels[0], 0))
    for l in range(depth):
        lam_init = 0.8 - 0.6 * math.exp(-0.3 * l)
        modsel = modsels[l]
        tm = 512 if (bsz * lt) % 512 == 0 else TOK_TILE
        p = _matmul(h1.reshape(bsz * lt, d), w_in_r[l], tm, P_COLS // 5, F32).reshape(bsz, lt, P_COLS)

        def head_row(v2):
            z4 = jnp.zeros((GDN_H,), F32)
            return jnp.pad(jnp.concatenate([v2[0], z4, v2[1], z4]), (0, lane_pad)).reshape(1, LANES)
        qkv_n, gcs, gct = _gdn_prep(p, conv_w[l], head_row(gdn_a_log[l]), head_row(gdn_dt_bias[l]), nl)
        o_f, o_b = _gdn_scan(qkv_n, gcs, gct, s_lat // GDN_CHUNK)

        a = _attn_prep(p, cos, sin)
        lv = diff_lambda[l].astype(F32)
        lam = jnp.exp(jnp.sum(lv[0] * lv[1])) - jnp.exp(jnp.sum(lv[2] * lv[3])) + lam_init
        lam_row = jnp.full((1, LANES), lam, F32)
        tq = min(512, s_lat)
        tkv = lt // 3 if (lt // 3) % TOK_TILE == 0 else TOK_TILE
        yb = _diff_attn(a, lam_row, diff_norm_w[l], lam_init, tq=tq, tkv=tkv, q0=0, nq=s_lat // tq,
                        k0=0, nkv=lt // tkv)
        yb = _diff_attn(a, lam_row, diff_norm_w[l], lam_init, tq=lc, tkv=lc, q0=s_lat // lc, nq=1,
                        k0=s_lat // lc, nkv=1, prev=yb)
        yc = _na_attn(a, _na_bias_tables(na_rpb[l], rows), s_lat, lc)
        yc = _na_ctx_attn(a, yc, s_lat, lc)

        xs, h2, logits = _merge(o_f, o_b, p, yb, yc, xs, gdn_norm_w[l], wb16[l], wo16[l], modsel, nl,
                                ln_g[l, 0], ln_b[l, 0], wr_pad[l], br_pad[l])
        f = _moe(h2.reshape(bsz * lt, d), logits.reshape(bsz * lt, LANES),
                 w_exp1[l], b_exp1[l], w_exp2[l], b_exp2[l]).reshape(bsz, lt, d)
        if l + 1 < depth:
            xs, h1 = _norm(xs, nl, y=f, gate=(modsel, 5), ln_g=ln_g[l, 1], ln_b=ln_b[l, 1],
                           shift=(modsels[l + 1], 0))
        else:
            (xs,) = _norm(xs, nl, y=f, gate=(modsel, 5), ln_g=ln_g[l, 1], ln_b=ln_b[l, 1])
    return xs[:, :s_lat]
```

```python
import functools
import math

import numpy as np
import jax
import jax.numpy as jnp
from jax import lax
from jax.experimental import pallas as pl
from jax.experimental.pallas import tpu as pltpu

F32 = jnp.float32
BF16 = jnp.bfloat16

GRID_W = 64
GDN_H = 4
GDN_DK = 128
GDN_CHUNK = 64
DIFF_H = 4
DIFF_D = 64
ROPE_BASE = 10000.0
NA_H = 8
NA_D = 64
WIN_H = 8
WIN_W = 16
N_EXPERTS = 32
TOP_K = 4
D_EXPERT = 1024
SWIGLU_LIMIT = 7.0
SWIGLU_ALPHA = 1.702
DN_ALPHA = 8.0 ** 0.25
LN_EPS = 1e-5
RMS_EPS = 1e-6
NEG_INF = -1e30

LANES = 128
TOK_TILE = 256
NA_QROWS = 8
NA_KROWS = 16
MOE_ROWS = 256
INV_BASE_LOG2 = 3
GDN_INTRA_TILE = 128
MOE_TILE = 256
MOE_GBUF_ROWS = MOE_TILE * TOP_K + N_EXPERTS * 8
MOE_RUN_PIECES = tuple(MOE_TILE >> s for s in range(int(math.log2(MOE_TILE)) - 2))
MOE_GAP_PIECES = tuple((MOE_ROWS // 2) >> s for s in range(int(math.log2(MOE_ROWS)) - 3))
VMEM_LIMIT = 56 * 1024 * 1024

C_GQ, C_GK, C_GV, C_GZ = 0, 512, 1024, 1536
C_DQ, C_DK, C_DV = 2048, 2560, 3072
C_NQ, C_NK, C_NV = 3584, 4096, 4608
C_GATE = 5120
C_AB = 8192
P_COLS = 8320


def _cparams(sem):
    return pltpu.CompilerParams(dimension_semantics=sem, vmem_limit_bytes=VMEM_LIMIT)


def _ln(x):
    mu = jnp.mean(x, -1, keepdims=True)
    xc = x - mu
    var = jnp.mean(xc * xc, -1, keepdims=True)
    return xc * lax.rsqrt(var + LN_EPS)


def _sigmoid(x):
    return 1.0 / (1.0 + jnp.exp(-x))


def _dot(a, b):
    return jnp.dot(a, b, preferred_element_type=F32)


def _dot_nt(a, b):
    return lax.dot_general(a, b, (((1,), (1,)), ((), ())), preferred_element_type=F32)


def _dot_tn(a, b):
    return lax.dot_general(a, b, (((0,), (0,)), ((), ())), preferred_element_type=F32)


def _mm(a, b):
    return jnp.dot(a.astype(BF16), b.astype(BF16), preferred_element_type=F32)


def _dot_f32(a, b):
    return jnp.dot(a, b, preferred_element_type=F32, precision=lax.Precision.HIGHEST)


def _ada_kernel(c_ref, w_ref, b_ref, o_ref):
    c = c_ref[...]
    s = (c * _sigmoid(c)).astype(BF16)
    o_ref[0] = _dot(s, w_ref[0].astype(BF16)) + b_ref[0]


def _ada(cmat, w_ada, b_ada):
    depth, d, n = w_ada.shape
    tn = n // 4
    return pl.pallas_call(
        _ada_kernel,
        grid=(depth, n // tn),
        in_specs=[pl.BlockSpec((8, d), lambda l, j: (0, 0)),
                  pl.BlockSpec((1, d, tn), lambda l, j: (l, 0, j)),
                  pl.BlockSpec((1, 1, tn), lambda l, j: (l, 0, j))],
        out_specs=pl.BlockSpec((1, 8, tn), lambda l, j: (l, 0, j)),
        out_shape=jax.ShapeDtypeStruct((depth, 8, n), F32),
        compiler_params=_cparams(("parallel", "parallel")),
        name="ada",
    )(cmat, w_ada, b_ada.reshape(depth, 1, n))


def _norm_kernel(*refs, has_y, has_h):
    refs = list(refs)
    x = refs.pop(0)[...]
    if has_y:
        y_ref, g_ref, lg_ref, lb_ref = refs[:4]
        del refs[:4]
        x = _ln(DN_ALPHA * x + g_ref[...] * y_ref[...]) * lg_ref[...] + lb_ref[...]
    if has_h:
        sh_ref, sc_ref = refs[:2]
        del refs[:2]
    if has_y:
        refs.pop(0)[...] = x
    if has_h:
        refs.pop(0)[...] = (_ln(x) * (1.0 + sc_ref[...]) + sh_ref[...]).astype(BF16)


def _mod_spec(d, nl, k):
    return pl.BlockSpec((None, None, 1, d), lambda b, i: (b, i // nl, 0, k))


def _norm(x, nl, *, y=None, gate=None, ln_g=None, ln_b=None, shift=None):
    bsz, lt, d = x.shape
    has_y = y is not None
    has_h = shift is not None
    tok = pl.BlockSpec((None, TOK_TILE, d), lambda b, i: (b, i, 0))
    vec = pl.BlockSpec((1, d), lambda b, i: (0, 0))
    args, specs, outs, ospecs = [x], [tok], [], []
    if has_y:
        args += [y, gate[0], ln_g.reshape(1, d), ln_b.reshape(1, d)]
        specs += [tok, _mod_spec(d, nl, gate[1]), vec, vec]
        outs.append(jax.ShapeDtypeStruct((bsz, lt, d), F32))
        ospecs.append(tok)
    if has_h:
        args += [shift[0], shift[0]]
        specs += [_mod_spec(d, nl, shift[1]), _mod_spec(d, nl, shift[1] + 1)]
        outs.append(jax.ShapeDtypeStruct((bsz, lt, d), BF16))
        ospecs.append(tok)
    return pl.pallas_call(
        functools.partial(_norm_kernel, has_y=has_y, has_h=has_h),
        grid=(bsz, lt // TOK_TILE),
        in_specs=specs, out_specs=ospecs, out_shape=outs,
        compiler_params=_cparams(("parallel", "parallel")),
        name="norm",
    )(*args)


def _mm_kernel(a_ref, w_ref, o_ref):
    o_ref[...] = _dot(a_ref[...], w_ref[...]).astype(o_ref.dtype)


def _matmul(a, w, tm, tn, out_dtype):
    m, k = a.shape
    n = w.shape[1]
    return pl.pallas_call(
        _mm_kernel,
        grid=(n // tn, m // tm),
        in_specs=[pl.BlockSpec((tm, k), lambda j, i: (i, 0)),
                  pl.BlockSpec((k, tn), lambda j, i: (0, j))],
        out_specs=pl.BlockSpec((tm, tn), lambda j, i: (i, j)),
        out_shape=jax.ShapeDtypeStruct((m, n), out_dtype),
        compiler_params=_cparams(("parallel", "parallel")),
        name="matmul",
    )(a, w)


def _gdn_prep_kernel(x_ref, hp_ref, hn_ref, ab_ref, cw_ref, alog_ref, dtb_ref,
                     qkv_ref, gc_ref, gct_ref, *, nl):
    i = pl.program_id(1)
    x = x_ref[...]
    t = x.shape[0]
    prev_ok = jnp.logical_and(i != 0, i != nl)
    next_ok = jnp.logical_and(i != nl - 1, i != nl)
    prow = jnp.where(prev_ok, hp_ref[7:8, :], 0.0)
    nrow = jnp.where(next_ok, hn_ref[0:1, :], 0.0)
    rid = lax.broadcasted_iota(jnp.int32, x.shape, 0)
    xp = jnp.where(rid == 0, prow, pltpu.roll(x, 1, 0))
    xn = jnp.where(rid == t - 1, nrow, pltpu.roll(x, t - 1, 0))
    w = cw_ref[...]
    y = xp * w[0:1] + x * w[1:2] + xn * w[2:3]
    y = y * _sigmoid(y)
    for g in range(12):
        blk = y[:, g * LANES:(g + 1) * LANES]
        if g < 8:
            blk = blk * lax.rsqrt(jnp.sum(blk * blk, -1, keepdims=True) + RMS_EPS)
        if g < 4:
            blk = blk * (GDN_DK ** -0.5)
        qkv_ref[:, g * LANES:(g + 1) * LANES] = blk

    ab = ab_ref[...]
    lane = lax.broadcasted_iota(jnp.int32, ab.shape, 1)
    xs = ab + dtb_ref[...]
    softplus = jnp.maximum(xs, 0.0) + jnp.log(1.0 + jnp.exp(-jnp.abs(xs)))
    g = -jnp.exp(alog_ref[...]) * softplus
    beta = _sigmoid(ab)
    is_a = (lane % 8) < 4
    gb = jnp.where(lane < 16, jnp.where(is_a, g, beta), 0.0)
    r = lax.broadcasted_iota(jnp.int32, (t, t), 0)
    c = lax.broadcasted_iota(jnp.int32, (t, t), 1)
    same = (r // GDN_CHUNK) == (c // GDN_CHUNK)
    tri_l = jnp.where(jnp.logical_and(same, c <= r), 1.0, 0.0)
    tri_u = jnp.where(jnp.logical_and(same, c >= r), 1.0, 0.0)
    cf = _dot_f32(tri_l, gb)
    cb = _dot_f32(tri_u, gb)
    gc = jnp.where(lane < 4, cf, jnp.where(jnp.logical_and(lane >= 8, lane < 12), cb, gb))
    gc_ref[...] = gc
    gct = gc.T
    for ch in range(t // GDN_CHUNK):
        gct_ref[ch] = gct[0:16, ch * GDN_CHUNK:(ch + 1) * GDN_CHUNK]


def _gdn_prep(p, conv_w, alog_row, dtb_row, nl):
    bsz, lt, _ = p.shape
    nt = lt // TOK_TILE
    w3 = 3 * 512
    rb = TOK_TILE // 8
    return pl.pallas_call(
        functools.partial(_gdn_prep_kernel, nl=nl),
        grid=(bsz, nt),
        in_specs=[pl.BlockSpec((None, TOK_TILE, w3), lambda b, i: (b, i, 0)),
                  pl.BlockSpec((None, 8, w3), lambda b, i: (b, jnp.maximum(i * rb - 1, 0), 0)),
                  pl.BlockSpec((None, 8, w3), lambda b, i: (b, jnp.minimum(i * rb + rb, lt // 8 - 1), 0)),
                  pl.BlockSpec((None, TOK_TILE, LANES), lambda b, i: (b, i, C_AB // LANES)),
                  pl.BlockSpec((3, w3), lambda b, i: (0, 0)),
                  pl.BlockSpec((1, LANES), lambda b, i: (0, 0)),
                  pl.BlockSpec((1, LANES), lambda b, i: (0, 0))],
        out_specs=[pl.BlockSpec((None, TOK_TILE, w3), lambda b, i: (b, i, 0)),
                   pl.BlockSpec((None, TOK_TILE, LANES), lambda b, i: (b, i, 0)),
                   pl.BlockSpec((None, TOK_TILE // GDN_CHUNK, 16, GDN_CHUNK), lambda b, i: (b, i, 0, 0))],
        out_shape=[jax.ShapeDtypeStruct((bsz, lt, w3), F32),
                   jax.ShapeDtypeStruct((bsz, lt, LANES), F32),
                   jax.ShapeDtypeStruct((bsz, lt // GDN_CHUNK, 16, GDN_CHUNK), F32)],
        compiler_params=_cparams(("parallel", "parallel")),
        name="gdn_prep",
    )(p, p, p, p, conv_w, alog_row, dtb_row)


def _gdn_intra_kernel(qkv_ref, gc_ref, gct_ref, u_ref, w_ref, qg_ref, kt_ref, qk_ref, eg_ref):
    cs = GDN_CHUNK
    nch = qkv_ref.shape[0] // cs
    chains = [(d, c, h) for d in range(2) for c in range(nch) for h in range(GDN_H)]
    ri = lax.broadcasted_iota(jnp.int32, (cs, cs), 0)
    ci = lax.broadcasted_iota(jnp.int32, (cs, cs), 1)
    eye = ri == ci
    base = (ri >> INV_BASE_LOG2) == (ci >> INV_BASE_LOG2)
    incl = (ri >= ci, ri <= ci)
    strict = (ri > ci, ri < ci)
    qkv = qkv_ref[...]
    gc = gc_ref[...]

    def part(col0, c, h):
        return qkv[c * cs:(c + 1) * cs, col0 + h * LANES:col0 + (h + 1) * LANES]

    ch_keys = [(c, h) for c in range(nch) for h in range(GDN_H)]
    q = {key: part(C_GQ, *key) for key in ch_keys}
    k = {key: part(C_GK, *key) for key in ch_keys}
    v = {key: part(C_GV, *key) for key in ch_keys}
    q16 = {key: q[key].astype(BF16) for key in ch_keys}
    k16 = {key: k[key].astype(BF16) for key in ch_keys}
    gcol = [gc[c * cs:(c + 1) * cs, 8 * d + h:8 * d + h + 1] for d, c, h in chains]
    bcol = [gc[c * cs:(c + 1) * cs, 8 * d + 4 + h:8 * d + 5 + h] for d, c, h in chains]
    grow = [gct_ref[c, 8 * d + h:8 * d + h + 1, :] for d, c, h in chains]
    glast = [g[cs - 1:cs, :] if d == 0 else g[0:1, :] for g, (d, c, h) in zip(gcol, chains)]
    decay = [jnp.where(incl[d], jnp.exp(jnp.where(incl[d], gcol[n] - grow[n], 0.0)), 0.0)
             for n, (d, c, h) in enumerate(chains)]
    kb = [k[(c, h)] * bcol[n] for n, (d, c, h) in enumerate(chains)]
    amat = [jnp.where(strict[d], _dot_nt(kb[n].astype(BF16), k16[(c, h)]) * decay[n], 0.0)
            for n, (d, c, h) in enumerate(chains)]
    qk = [_dot_nt(q16[(c, h)], k16[(c, h)]) * decay[n] for n, (d, c, h) in enumerate(chains)]
    eg = [jnp.exp(g) for g in gcol]
    rhs = [jnp.concatenate([v[(c, h)] * bcol[n], kb[n] * eg[n]], axis=1) for n, (d, c, h) in enumerate(chains)]
    rmat = [jnp.where(base, -a, 0.0) for a in amat]
    mmat = rmat
    for _ in range(INV_BASE_LOG2 - 1):
        mmat = [_mm(m, m) for m in mmat]
        prod = [_mm(r, m) for r, m in zip(rmat, mmat)]
        rmat = [r + m + p for r, m, p in zip(rmat, mmat, prod)]
    tmat = [jnp.where(eye, 1.0, r) for r in rmat]
    for lb in range(INV_BASE_LOG2, int(math.log2(cs))):
        same_pair = (ri >> (lb + 1)) == (ci >> (lb + 1))
        off = (jnp.logical_and((ri >> lb) == (ci >> lb) + 1, same_pair),
               jnp.logical_and((ci >> lb) == (ri >> lb) + 1, same_pair))
        inner = [_mm(jnp.where(off[d], amat[n], 0.0), tmat[n]) for n, (d, c, h) in enumerate(chains)]
        outer = [_mm(t, x) for t, x in zip(tmat, inner)]
        tmat = [t - x for t, x in zip(tmat, outer)]
    corr = [_mm(jnp.where(eye, 0.0, t), r) for t, r in zip(tmat, rhs)]
    for n, (d, c, h) in enumerate(chains):
        rows = slice(c * cs, (c + 1) * cs)
        cols = slice(h * LANES, (h + 1) * LANES)
        sol = rhs[n] + corr[n]
        u_ref[d, rows, cols] = sol[:, :LANES]
        w_ref[d, rows, cols] = sol[:, LANES:].astype(BF16)
        qg_ref[d, rows, cols] = (q[(c, h)] * eg[n]).astype(BF16)
        kt_ref[d, rows, cols] = (k[(c, h)] * jnp.exp(glast[n] - gcol[n])).astype(BF16)
        qk_ref[d, c, h] = qk[n].astype(BF16)
        eg_ref[d, c, h:h + 1, :] = jnp.broadcast_to(jnp.exp(glast[n]), (1, LANES))


def _gdn_intra(qkv, gc, gct):
    bsz, lt, w3 = qkv.shape
    cs = GDN_CHUNK
    nc = lt // cs
    nch = GDN_INTRA_TILE // cs

    def tok(dt):
        return (pl.BlockSpec((None, 2, GDN_INTRA_TILE, 512), lambda b, i: (b, 0, i, 0)),
                jax.ShapeDtypeStruct((bsz, 2, lt, 512), dt))

    outs = [tok(F32), tok(BF16), tok(BF16), tok(BF16),
            (pl.BlockSpec((None, 2, nch, GDN_H, cs, cs), lambda b, i: (b, 0, i, 0, 0, 0)),
             jax.ShapeDtypeStruct((bsz, 2, nc, GDN_H, cs, cs), BF16)),
            (pl.BlockSpec((None, 2, nch, GDN_H, LANES), lambda b, i: (b, 0, i, 0, 0)),
             jax.ShapeDtypeStruct((bsz, 2, nc, GDN_H, LANES), F32))]
    return pl.pallas_call(
        _gdn_intra_kernel,
        grid=(bsz, lt // GDN_INTRA_TILE),
        in_specs=[pl.BlockSpec((None, GDN_INTRA_TILE, w3), lambda b, i: (b, i, 0)),
                  pl.BlockSpec((None, GDN_INTRA_TILE, LANES), lambda b, i: (b, i, 0)),
                  pl.BlockSpec((None, nch, 16, cs), lambda b, i: (b, i, 0, 0))],
        out_specs=[o[0] for o in outs],
        out_shape=[o[1] for o in outs],
        compiler_params=_cparams(("parallel", "parallel")),
        name="gdn_intra",
    )(qkv, gc, gct)


def _gdn_scan_kernel(*refs):
    s_ref = refs[-1]
    cs = GDN_CHUNK

    @pl.when(pl.program_id(1) == 0)
    def _():
        s_ref[...] = jnp.zeros_like(s_ref)

    chains = [(d, h) for d in range(2) for h in range(GDN_H)]
    src = (refs[0:6], refs[6:12])
    outs = refs[12:14]

    def cols(h):
        return slice(h * LANES, (h + 1) * LANES)

    s = [s_ref[d * GDN_H + h] for d, h in chains]
    wq = [_dot(jnp.concatenate([src[d][1][:, cols(h)], src[d][2][:, cols(h)]], axis=0), s[n].astype(BF16))
          for n, (d, h) in enumerate(chains)]
    v16 = [(src[d][0][:, cols(h)] - wq[n][:cs]).astype(BF16) for n, (d, h) in enumerate(chains)]
    o = [wq[n][cs:] + _dot(src[d][4][h], v16[n]) for n, (d, h) in enumerate(chains)]
    s_new = [s[n] * src[d][5][h:h + 1, :] + _dot_tn(src[d][3][:, cols(h)], v16[n])
             for n, (d, h) in enumerate(chains)]
    for n, (d, h) in enumerate(chains):
        outs[d][:, cols(h)] = o[n]
        s_ref[d * GDN_H + h] = s_new[n]


def _gdn_scan(u, w, qg, kt, qk, eg, n_lat_chunks):
    bsz, _, lt, _ = u.shape
    cs = GDN_CHUNK
    nc = lt // cs
    ncc = nc - n_lat_chunks

    def cf(i):
        return jnp.where(i < ncc, n_lat_chunks + i, i - ncc)

    def cb(i):
        return nc - 1 - i

    def specs(d, c):
        tok = pl.BlockSpec((None, None, cs, 512), lambda b, i: (b, d, c(i), 0))
        return [tok, tok, tok, tok,
                pl.BlockSpec((None, None, None, GDN_H, cs, cs), lambda b, i: (b, d, c(i), 0, 0, 0)),
                pl.BlockSpec((None, None, None, GDN_H, LANES), lambda b, i: (b, d, c(i), 0, 0))]

    def out(c):
        return pl.BlockSpec((None, cs, 512), lambda b, i: (b, c(i), 0))

    args = (u, w, qg, kt, qk, eg)
    return pl.pallas_call(
        _gdn_scan_kernel,
        grid=(bsz, nc),
        in_specs=specs(0, cf) + specs(1, cb),
        out_specs=[out(cf), out(cb)],
        out_shape=[jax.ShapeDtypeStruct((bsz, lt, 512), F32)] * 2,
        scratch_shapes=[pltpu.VMEM((2 * GDN_H, GDN_DK, LANES), F32)],
        compiler_params=_cparams(("parallel", "arbitrary")),
        name="gdn_scan",
    )(*args, *args)


def _attn_prep_kernel(dq_ref, dk_ref, dv_ref, nq_ref, nk_ref, nv_ref, cos_ref, sin_ref, o_ref):
    cos = cos_ref[...]
    sin = sin_ref[...]
    lane = lax.broadcasted_iota(jnp.int32, cos.shape, 1)
    first = (lane % DIFF_D) < DIFF_D // 2

    def rope(x):
        rot = jnp.where(first, pltpu.roll(x, LANES - DIFF_D // 2, 1), pltpu.roll(x, DIFF_D // 2, 1))
        return x * cos + rot * sin

    for h in range(DIFF_H):
        sl = slice(h * LANES, (h + 1) * LANES)
        o_ref[:, h * LANES:(h + 1) * LANES] = (rope(dq_ref[:, sl]) * (DIFF_D ** -0.5)).astype(BF16)
        o_ref[:, 512 + h * LANES:512 + (h + 1) * LANES] = rope(dk_ref[:, sl]).astype(BF16)
    o_ref[:, 1024:1536] = dv_ref[...].astype(BF16)
    o_ref[:, 1536:2048] = (nq_ref[...] * (NA_D ** -0.5)).astype(BF16)
    o_ref[:, 2048:2560] = nk_ref[...].astype(BF16)
    o_ref[:, 2560:3072] = nv_ref[...].astype(BF16)


def _attn_prep(p, cos, sin):
    bsz, lt, _ = p.shape

    def col(cb):
        return pl.BlockSpec((None, TOK_TILE, 512), lambda b, i: (b, i, cb))

    tab = pl.BlockSpec((TOK_TILE, LANES), lambda b, i: (i, 0))
    return pl.pallas_call(
        _attn_prep_kernel,
        grid=(bsz, lt // TOK_TILE),
        in_specs=[col(C_DQ // 512), col(C_DK // 512), col(C_DV // 512),
                  col(C_NQ // 512), col(C_NK // 512), col(C_NV // 512), tab, tab],
        out_specs=pl.BlockSpec((None, TOK_TILE, 3072), lambda b, i: (b, i, 0)),
        out_shape=jax.ShapeDtypeStruct((bsz, lt, 3072), BF16),
        compiler_params=_cparams(("parallel", "parallel")),
        name="attn_prep",
    )(p, p, p, p, p, p, cos, sin)


def _diff_kernel(lam_ref, q_ref, k_ref, v_ref, nw_ref, o_ref, m_sc, l_sc, acc_sc, *, nkv, lam_init):
    kv = pl.program_id(3)

    @pl.when(kv == 0)
    def _():
        m_sc[...] = jnp.full_like(m_sc, NEG_INF)
        l_sc[...] = jnp.zeros_like(l_sc)
        acc_sc[...] = jnp.zeros_like(acc_sc)

    q = q_ref[...]
    k = k_ref[...]
    v = v_ref[...]
    maps = range(2)
    s = [_dot_nt(q[:, m * DIFF_D:(m + 1) * DIFF_D], k[:, m * DIFF_D:(m + 1) * DIFF_D]) for m in maps]
    m_prev = [m_sc[m] for m in maps]
    m_new = [jnp.maximum(m_prev[m], jnp.max(s[m], -1, keepdims=True)) for m in maps]
    p = [jnp.exp(s[m] - m_new[m]) for m in maps]
    alpha = [jnp.exp(m_prev[m] - m_new[m]) for m in maps]
    pv = [_dot(p[m].astype(BF16), v) for m in maps]
    for m in maps:
        l_sc[m] = alpha[m] * l_sc[m] + jnp.sum(p[m], -1, keepdims=True)
        acc_sc[m] = alpha[m] * acc_sc[m] + pv[m]
        m_sc[m] = m_new[m]

    @pl.when(kv == nkv - 1)
    def _():
        lam = lam_ref[0:1, 0:1]
        o = acc_sc[0] / l_sc[0] - lam * (acc_sc[1] / l_sc[1])
        y = o * lax.rsqrt(jnp.mean(o * o, -1, keepdims=True) + RMS_EPS) * nw_ref[...] * (1.0 - lam_init)
        o_ref[...] = y.astype(BF16)


def _diff_attn(a, lam_row, norm_w, lam_init, *, tq, tkv, q0, nq, k0, nkv):
    bsz = a.shape[0]
    return pl.pallas_call(
        functools.partial(_diff_kernel, nkv=nkv, lam_init=lam_init),
        grid=(bsz, DIFF_H, nq, nkv),
        in_specs=[pl.BlockSpec((1, LANES), lambda b, h, i, j: (0, 0)),
                  pl.BlockSpec((None, tq, LANES), lambda b, h, i, j: (b, q0 + i, h)),
                  pl.BlockSpec((None, tkv, LANES), lambda b, h, i, j: (b, k0 + j, 4 + h)),
                  pl.BlockSpec((None, tkv, LANES), lambda b, h, i, j: (b, k0 + j, 8 + h)),
                  pl.BlockSpec((1, LANES), lambda b, h, i, j: (0, 0))],
        out_specs=pl.BlockSpec((None, tq, LANES), lambda b, h, i, j: (b, i, h)),
        out_shape=jax.ShapeDtypeStruct((bsz, nq * tq, 512), BF16),
        scratch_shapes=[pltpu.VMEM((2, tq, 1), F32), pltpu.VMEM((2, tq, 1), F32),
                        pltpu.VMEM((2, tq, LANES), F32)],
        compiler_params=_cparams(("parallel", "parallel", "parallel", "arbitrary")),
        name="diff_attn",
    )(lam_row, a, a, a, norm_w.reshape(1, LANES))


def _na_kernel(q_ref, k_ref, v_ref, bias_ref, o_ref, *, s_lat, lc):
    i = pl.program_id(2)
    tq = q_ref.shape[0]
    nk = NA_KROWS * GRID_W
    kstart = jnp.clip(i * tq - (NA_KROWS - NA_QROWS) // 2 * GRID_W, 0, s_lat - nk)
    kstart = pl.multiple_of(kstart, TOK_TILE)
    q = q_ref[...]
    k_loc = k_ref[pl.ds(kstart, nk), :]
    v_loc = v_ref[pl.ds(kstart, nk), :]
    k_ctx = k_ref[s_lat:s_lat + lc, :]
    v_ctx = v_ref[s_lat:s_lat + lc, :]
    outs = []
    for j in range(2):
        sl = slice(j * NA_D, (j + 1) * NA_D)
        qh = q[:, sl]
        s_loc = _dot_nt(qh, k_loc[:, sl]) + bias_ref[j]
        s_ctx = _dot_nt(qh, k_ctx[:, sl])
        m = jnp.maximum(jnp.max(s_loc, -1, keepdims=True), jnp.max(s_ctx, -1, keepdims=True))
        p_loc = jnp.exp(s_loc - m)
        p_ctx = jnp.exp(s_ctx - m)
        l = jnp.sum(p_loc, -1, keepdims=True) + jnp.sum(p_ctx, -1, keepdims=True)
        o = _dot(p_loc.astype(BF16), v_loc[:, sl]) + _dot(p_ctx.astype(BF16), v_ctx[:, sl])
        outs.append(o / l)
    o_ref[...] = jnp.concatenate(outs, axis=1).astype(BF16)


def _na_attn(a, bias, s_lat, lc):
    bsz, lt, _ = a.shape
    tq = NA_QROWS * GRID_W
    nq = s_lat // tq

    def variant(i):
        return jnp.where(i == 0, 0, jnp.where(i == nq - 1, 2, 1))

    return pl.pallas_call(
        functools.partial(_na_kernel, s_lat=s_lat, lc=lc),
        grid=(bsz, NA_H // 2, nq),
        in_specs=[pl.BlockSpec((None, tq, LANES), lambda b, h, i: (b, i, 12 + h)),
                  pl.BlockSpec((None, lt, LANES), lambda b, h, i: (b, 0, 16 + h)),
                  pl.BlockSpec((None, lt, LANES), lambda b, h, i: (b, 0, 20 + h)),
                  pl.BlockSpec((None, 2, tq, NA_KROWS * GRID_W), lambda b, h, i: (variant(i), h, 0, 0))],
        out_specs=pl.BlockSpec((None, tq, LANES), lambda b, h, i: (b, i, h)),
        out_shape=jax.ShapeDtypeStruct((bsz, s_lat, 512), BF16),
        compiler_params=_cparams(("parallel", "parallel", "arbitrary")),
        name="na_attn",
    )(a, a, a, bias)


def _na_ctx_kernel(q_ref, k_ref, v_ref, o_ref):
    q = q_ref[...]
    k = k_ref[...]
    v = v_ref[...]
    outs = []
    for j in range(2):
        sl = slice(j * NA_D, (j + 1) * NA_D)
        s = _dot_nt(q[:, sl], k[:, sl])
        p = jnp.exp(s - jnp.max(s, -1, keepdims=True))
        outs.append(_dot(p.astype(BF16), v[:, sl]) / jnp.sum(p, -1, keepdims=True))
    o_ref[...] = jnp.concatenate(outs, axis=1).astype(BF16)


def _na_ctx_attn(a, s_lat, lc):
    bsz = a.shape[0]
    rb = s_lat // lc

    def blk(c0):
        return pl.BlockSpec((None, lc, LANES), lambda b, h: (b, rb, c0 + h))

    return pl.pallas_call(
        _na_ctx_kernel,
        grid=(bsz, NA_H // 2),
        in_specs=[blk(12), blk(16), blk(20)],
        out_specs=pl.BlockSpec((None, lc, LANES), lambda b, h: (b, 0, h)),
        out_shape=jax.ShapeDtypeStruct((bsz, lc, 512), BF16),
        compiler_params=_cparams(("parallel", "parallel")),
        name="na_ctx_attn",
    )(a, a, a)


def _na_bias_tables(rpb, rows):
    qr = np.arange(NA_QROWS)[:, None]
    kr = np.arange(NA_KROWS)[None, :]
    qc = np.arange(GRID_W)[:, None]
    kc = np.arange(GRID_W)[None, :]
    w0 = np.clip(qc - WIN_W // 2, 0, GRID_W - WIN_W)
    okc = (kc >= w0) & (kc < w0 + WIN_W)
    dc = np.clip(kc - qc + WIN_W - 1, 0, 2 * WIN_W - 2)
    half = (NA_KROWS - NA_QROWS) // 2
    n_dr, n_dc = 2 * WIN_H - 1, 2 * WIN_W - 1
    sel_c = (dc[None] == np.arange(n_dc)[:, None, None]) & okc[None]
    sel_r = np.zeros((3, NA_QROWS, NA_KROWS, n_dr), bool)
    for vi, (r_start, delta) in enumerate(((0, 0), (NA_QROWS, -half), (rows - NA_QROWS, -2 * half))):
        r = r_start + qr
        kabs = r_start + delta + kr
        r0 = np.clip(r - WIN_H // 2, 0, rows - WIN_H)
        okr = (kabs >= r0) & (kabs < r0 + WIN_H)
        dr = np.clip(kabs - r + WIN_H - 1, 0, n_dr - 1)
        sel_r[vi] = (dr[..., None] == np.arange(n_dr)) & okr[..., None]
    toep = jnp.einsum('lhaj,jqk->lhaqk', rpb, sel_c.astype(np.float32), precision=lax.Precision.HIGHEST)
    tab = jnp.einsum('vrsa,lhaqk->lvhrqsk', sel_r.astype(np.float32), toep, precision=lax.Precision.HIGHEST)
    ok = sel_r.any(-1)[None, :, None, :, None, :, None] & okc[None, None, None, None, :, None, :]
    tab = jnp.where(ok, tab, NEG_INF)
    return tab.reshape(rpb.shape[0], 3, NA_H, NA_QROWS * GRID_W, NA_KROWS * GRID_W).astype(F32)


def _merge_kernel(of_ref, ob_ref, z_ref, ybl_ref, ybc_ref, ycl_ref, ycc_ref, ga_ref, gb_ref, gc_ref, x_ref,
                  gnw_ref, wb_ref, wo_ref, g1_ref, lg_ref, lb_ref, sh_ref, sc_ref, wr_ref, br_ref,
                  xo_ref, h_ref, lg_out_ref, *, nl):
    is_ctx = pl.program_id(1) >= nl
    yb = jnp.where(is_ctx, ybc_ref[...], ybl_ref[...])
    yc = jnp.where(is_ctx, ycc_ref[...], ycl_ref[...])
    o = of_ref[...] + ob_ref[...]
    z = z_ref[...]
    gnw = gnw_ref[...]
    parts = []
    for h in range(GDN_H):
        sl = slice(h * LANES, (h + 1) * LANES)
        oh = o[:, sl]
        zh = z[:, sl]
        parts.append(oh * lax.rsqrt(jnp.mean(oh * oh, -1, keepdims=True) + RMS_EPS) * gnw * (zh * _sigmoid(zh)))
    ya = jnp.concatenate(parts, axis=1).astype(BF16)
    m = (_sigmoid(ga_ref[...]) * _dot(ya, wb_ref[0])
         + _sigmoid(gb_ref[...]) * _dot(yb, wb_ref[1])
         + _sigmoid(gc_ref[...]) * _dot(yc, wb_ref[2]))
    mx = _dot(m.astype(BF16), wo_ref[...])
    x = _ln(DN_ALPHA * x_ref[...] + g1_ref[...] * mx) * lg_ref[...] + lb_ref[...]
    xo_ref[...] = x
    hh = _ln(x) * (1.0 + sc_ref[...]) + sh_ref[...]
    h_ref[...] = hh.astype(BF16)
    lg_out_ref[...] = _dot_f32(hh, wr_ref[...]) + br_ref[...]


def _merge(o_f, o_b, p, yb, yb_ctx, yc, yc_ctx, x, gnw, wb, wo, modsel, nl, ln_g, ln_b, wr, br):
    bsz, lt, d = x.shape

    def tok(width, cb):
        return pl.BlockSpec((None, TOK_TILE, width), lambda b, i: (b, i, cb))

    def const(shape):
        return pl.BlockSpec(shape, lambda b, i: (0,) * len(shape))

    lat = pl.BlockSpec((None, TOK_TILE, 512), lambda b, i: (b, jnp.minimum(i, nl - 1), 0))
    cxt = pl.BlockSpec((None, TOK_TILE, 512), lambda b, i: (b, 0, 0))
    return pl.pallas_call(
        functools.partial(_merge_kernel, nl=nl),
        grid=(bsz, lt // TOK_TILE),
        in_specs=[tok(512, 0), tok(512, 0), tok(512, C_GZ // 512), lat, cxt, lat, cxt,
                  tok(d, C_GATE // d), tok(d, C_GATE // d + 1), tok(d, C_GATE // d + 2), tok(d, 0),
                  const((1, LANES)), const((3, 512, d)), const((d, d)),
                  _mod_spec(d, nl, 2), const((1, d)), const((1, d)),
                  _mod_spec(d, nl, 3), _mod_spec(d, nl, 4),
                  const((d, LANES)), const((1, LANES))],
        out_specs=[tok(d, 0), tok(d, 0), tok(LANES, 0)],
        out_shape=[jax.ShapeDtypeStruct((bsz, lt, d), F32),
                   jax.ShapeDtypeStruct((bsz, lt, d), BF16),
                   jax.ShapeDtypeStruct((bsz, lt, LANES), F32)],
        compiler_params=_cparams(("parallel", "parallel")),
        name="merge",
    )(o_f, o_b, p, yb, yb_ctx, yc, yc_ctx, p, p, p, x, gnw.reshape(1, LANES), wb, wo,
      modsel, ln_g.reshape(1, d), ln_b.reshape(1, d), modsel, modsel, wr, br)


def _ffn_kernel(be_ref, nu_ref, x_ref, w1_ref, b1_ref, w2_ref, b2_ref, o_ref, w1c, w2c):
    i = pl.program_id(0)

    @pl.when(i < nu_ref[0])
    def _():
        changed = jnp.logical_or(i == 0, be_ref[i] != be_ref[jnp.maximum(i - 1, 0)])

        @pl.when(changed)
        def _():
            w1c[...] = w1_ref[...].astype(BF16)
            w2c[...] = w2_ref[...].astype(BF16)

        hu = _dot(x_ref[...].astype(BF16), w1c[...]) + b1_ref[...]
        x_glu = jnp.minimum(hu[:, :D_EXPERT], SWIGLU_LIMIT)
        x_lin = jnp.clip(hu[:, D_EXPERT:], -SWIGLU_LIMIT, SWIGLU_LIMIT)
        act = x_glu * _sigmoid(SWIGLU_ALPHA * x_glu) * (x_lin + 1.0)
        o_ref[...] = _dot(act.astype(BF16), w2c[...]) + b2_ref[...]

    @pl.when(i >= nu_ref[0])
    def _():
        o_ref[...] = jnp.zeros_like(o_ref)


def _ffn(blk_e, n_used, xs, w1, b1, w2, b2):
    n_rows, d = xs.shape
    n_blk = n_rows // MOE_ROWS
    de2 = w1.shape[-1]
    gs = pltpu.PrefetchScalarGridSpec(
        num_scalar_prefetch=2,
        grid=(n_blk,),
        in_specs=[pl.BlockSpec((MOE_ROWS, d), lambda i, be, nu: (i, 0)),
                  pl.BlockSpec((None, d, de2), lambda i, be, nu: (be[i], 0, 0)),
                  pl.BlockSpec((None, 1, de2), lambda i, be, nu: (be[i], 0, 0)),
                  pl.BlockSpec((None, de2 // 2, d), lambda i, be, nu: (be[i], 0, 0)),
                  pl.BlockSpec((None, 1, d), lambda i, be, nu: (be[i], 0, 0))],
        out_specs=pl.BlockSpec((MOE_ROWS, d), lambda i, be, nu: (i, 0)),
        scratch_shapes=[pltpu.VMEM((d, de2), BF16), pltpu.VMEM((de2 // 2, d), BF16)],
    )
    return pl.pallas_call(
        _ffn_kernel,
        grid_spec=gs,
        out_shape=jax.ShapeDtypeStruct((n_rows, d), F32),
        compiler_params=_cparams(("arbitrary",)),
        name="ffn",
    )(blk_e, n_used, xs, w1, b1.reshape(N_EXPERTS, 1, de2), w2, b2.reshape(N_EXPERTS, 1, d))


def _route_kernel(lg_ref, rc_ref, rt_ref, n8_ref):
    l = lg_ref[...]
    tt = l.shape[0]
    lane = lax.broadcasted_iota(jnp.int32, l.shape, 1)
    picks, vals = [], []
    for _ in range(TOP_K):
        m = jnp.max(l, -1, keepdims=True)
        idx = jnp.min(jnp.where(l == m, lane, LANES), -1, keepdims=True)
        oh = lane == idx
        picks.append(oh)
        vals.append(m)
        l = jnp.where(oh, -jnp.inf, l)
    sel = sum(jnp.where(oh, 1.0, 0.0) for oh in picks)
    cnt = jnp.sum(sel, 0, keepdims=True)
    n8 = jnp.floor((cnt + 7.0) * 0.125) * 8.0
    r = lax.broadcasted_iota(jnp.int32, (LANES, LANES), 0)
    c = lax.broadcasted_iota(jnp.int32, (LANES, LANES), 1)
    off8 = _dot_f32(jnp.broadcast_to(n8, (8, LANES)), jnp.where(r < c, 1.0, 0.0))[0:1]
    tr = lax.broadcasted_iota(jnp.int32, (tt, tt), 0)
    tc = lax.broadcasted_iota(jnp.int32, (tt, tt), 1)
    rank = _dot(jnp.where(tc < tr, 1.0, 0.0).astype(BF16), sel.astype(BF16))
    slot = off8 + rank
    es = [jnp.exp(v - vals[0]) for v in vals]
    den = sum(es)
    rc = jnp.zeros(l.shape, F32)
    for kk in range(TOP_K):
        loc = jnp.sum(jnp.where(picks[kk], slot, 0.0), -1, keepdims=True)
        rc = jnp.where(lane == kk, loc, rc)
        rc = jnp.where(lane == TOP_K + kk, es[kk] / den, rc)
    rc_ref[...] = rc
    rt_ref[...] = rc.T[0:8, :]
    n8_ref[...] = jnp.broadcast_to(n8, (8, LANES))


def _route(logits):
    t = logits.shape[0]
    nt = t // MOE_TILE
    return pl.pallas_call(
        _route_kernel,
        grid=(nt,),
        in_specs=[pl.BlockSpec((MOE_TILE, LANES), lambda i: (i, 0))],
        out_specs=[pl.BlockSpec((MOE_TILE, LANES), lambda i: (i, 0)),
                   pl.BlockSpec((None, 8, MOE_TILE), lambda i: (i, 0, 0)),
                   pl.BlockSpec((None, 8, LANES), lambda i: (i, 0, 0))],
        out_shape=[jax.ShapeDtypeStruct((t, LANES), F32),
                   jax.ShapeDtypeStruct((nt, 8, MOE_TILE), F32),
                   jax.ShapeDtypeStruct((nt, 8, LANES), F32)],
        compiler_params=_cparams(("parallel",)),
        name="moe_route",
    )(logits)


def _run_copies(n, pieces, make_copy, wait):
    done = jnp.int32(0)
    for size in pieces:
        take = (n & size) != 0

        @pl.when(take)
        def _():
            cp = make_copy(done, size)
            if wait:
                cp.wait()
            else:
                cp.start()

        done = done + jnp.where(take, size, 0)


def _tile_runs(gs_ref, n8_ref, tile, make_copy):
    for wait in (False, True):
        def body(e, local):
            n = n8_ref[tile * N_EXPERTS + e]
            g = gs_ref[tile * N_EXPERTS + e]
            _run_copies(n, MOE_RUN_PIECES,
                        lambda done, size: make_copy(pl.multiple_of(local + done, 8),
                                                     pl.multiple_of(g + done, 8), size), wait)
            return local + n
        lax.fori_loop(0, N_EXPERTS, body, jnp.int32(0))


def _dispatch_kernel(gs_ref, n8_ref, go_ref, gn_ref, h_ref, rt_ref, xs_ref, gbuf, zbuf, sem):
    i = pl.program_id(0)

    @pl.when(i == 0)
    def _():
        zbuf[...] = jnp.zeros_like(zbuf)
        zrows = zbuf.shape[0]
        for wait in (False, True):
            def body(e, carry):
                _run_copies(gn_ref[e], MOE_GAP_PIECES,
                            lambda done, size: pltpu.make_async_copy(
                                zbuf.at[pl.ds(0, size), :],
                                xs_ref.at[pl.ds(pl.multiple_of(go_ref[e] + done, 8), size), :], sem), wait)
                return carry
            lax.fori_loop(0, N_EXPERTS, body, 0)

            def tail(j, carry):
                cp = pltpu.make_async_copy(
                    zbuf, xs_ref.at[pl.ds(pl.multiple_of(go_ref[N_EXPERTS] + j * zrows, 8), zrows), :], sem)
                if wait:
                    cp.wait()
                else:
                    cp.start()
                return carry
            lax.fori_loop(0, gn_ref[N_EXPERTS], tail, 0)

    loc = rt_ref[0:TOP_K, :].astype(jnp.int32)
    row = lax.broadcasted_iota(jnp.int32, (gbuf.shape[0], loc.shape[1]), 0)
    hit = row == loc[0:1, :]
    for kk in range(1, TOP_K):
        hit = jnp.logical_or(hit, row == loc[kk:kk + 1, :])
    gbuf[...] = _dot(jnp.where(hit, 1.0, 0.0).astype(BF16), h_ref[...])
    _tile_runs(gs_ref, n8_ref, i, lambda local, g, size: pltpu.make_async_copy(
        gbuf.at[pl.ds(local, size), :], xs_ref.at[pl.ds(g, size), :], sem))


def _dispatch(gstart, n8, gap_off, gap_n, h2, rt, n_rows):
    t, d = h2.shape
    nt = t // MOE_TILE
    gs = pltpu.PrefetchScalarGridSpec(
        num_scalar_prefetch=4,
        grid=(nt,),
        in_specs=[pl.BlockSpec((MOE_TILE, d), lambda i, *_: (i, 0)),
                  pl.BlockSpec((None, 8, MOE_TILE), lambda i, *_: (i, 0, 0))],
        out_specs=pl.BlockSpec(memory_space=pl.ANY),
        scratch_shapes=[pltpu.VMEM((MOE_GBUF_ROWS, d), F32), pltpu.VMEM((MOE_ROWS // 2, d), F32),
                        pltpu.SemaphoreType.DMA(())],
    )
    return pl.pallas_call(
        _dispatch_kernel,
        grid_spec=gs,
        out_shape=jax.ShapeDtypeStruct((n_rows, d), F32),
        compiler_params=_cparams(("arbitrary",)),
        name="moe_dispatch",
    )(gstart, n8, gap_off, gap_n, h2, rt)


def _combine_kernel(gs_ref, n8_ref, ys_ref, rc_ref, f_ref, ybuf, sem):
    i = pl.program_id(0)

    @pl.when(i == 0)
    def _():
        ybuf[...] = jnp.zeros_like(ybuf)

    _tile_runs(gs_ref, n8_ref, i, lambda local, g, size: pltpu.make_async_copy(
        ys_ref.at[pl.ds(g, size), :], ybuf.at[pl.ds(local, size), :], sem))
    rc = rc_ref[...]
    col = lax.broadcasted_iota(jnp.int32, (rc.shape[0], ybuf.shape[0]), 1)
    wgt = jnp.zeros(col.shape, F32)
    for kk in range(TOP_K):
        wgt = jnp.where(col == rc[:, kk:kk + 1].astype(jnp.int32), rc[:, TOP_K + kk:TOP_K + kk + 1], wgt)
    y = ybuf[...]
    w_hi = wgt.astype(BF16)
    w_lo = (wgt - w_hi.astype(F32)).astype(BF16)
    y_hi = y.astype(BF16)
    y_lo = (y - y_hi.astype(F32)).astype(BF16)
    f_ref[...] = _dot(w_hi, y_hi) + _dot(w_hi, y_lo) + _dot(w_lo, y_hi)


def _combine(gstart, n8, ys, rc):
    t = rc.shape[0]
    d = ys.shape[1]
    nt = t // MOE_TILE
    gs = pltpu.PrefetchScalarGridSpec(
        num_scalar_prefetch=2,
        grid=(nt,),
        in_specs=[pl.BlockSpec(memory_space=pl.ANY),
                  pl.BlockSpec((MOE_TILE, LANES), lambda i, *_: (i, 0))],
        out_specs=pl.BlockSpec((MOE_TILE, d), lambda i, *_: (i, 0)),
        scratch_shapes=[pltpu.VMEM((MOE_GBUF_ROWS, d), F32), pltpu.SemaphoreType.DMA(())],
    )
    return pl.pallas_call(
        _combine_kernel,
        grid_spec=gs,
        out_shape=jax.ShapeDtypeStruct((t, d), F32),
        compiler_params=_cparams(("arbitrary",)),
        name="moe_combine",
    )(gstart, n8, ys, rc)


def _moe(h2, logits, w1, b1, w2, b2):
    t, d = h2.shape
    nt = t // MOE_TILE
    rc, rt, n8f = _route(logits)
    n8 = n8f[:, 0, :N_EXPERTS].astype(jnp.int32)
    e_rows = jnp.sum(n8, 0)
    e_pad = (e_rows + MOE_ROWS - 1) // MOE_ROWS * MOE_ROWS
    e_end = jnp.cumsum(e_pad)
    e_start = e_end - e_pad
    gstart = (e_start[None, :] + jnp.cumsum(n8, 0) - n8).reshape(-1)
    n_rows = -(-(t * TOP_K + nt * N_EXPERTS * 7 + N_EXPERTS * (MOE_ROWS - 1)) // MOE_ROWS) * MOE_ROWS
    n_used = (e_end[-1:] // MOE_ROWS).astype(jnp.int32)
    blk = jnp.minimum(jnp.arange(n_rows // MOE_ROWS, dtype=jnp.int32), n_used - 1) * MOE_ROWS
    blk_e = jnp.minimum(jnp.sum(blk[:, None] >= e_end[None, :], axis=1), N_EXPERTS - 1).astype(jnp.int32)
    n8_flat = n8.reshape(-1)
    gap_off = jnp.concatenate([e_start + e_rows, e_end[-1:]]).astype(jnp.int32)
    gap_n = jnp.concatenate([e_pad - e_rows, (n_rows - e_end[-1:]) // (MOE_ROWS // 2)]).astype(jnp.int32)
    xs = _dispatch(gstart.astype(jnp.int32), n8_flat, gap_off, gap_n, h2, rt, n_rows)
    ys = _ffn(blk_e, n_used, xs, w1, b1, w2, b2)
    return _combine(gstart, n8_flat, ys, rc)


def _rope_tables(s_lat, lc):
    t = jnp.arange(s_lat)
    row = (t // GRID_W).astype(F32)
    col = (t % GRID_W).astype(F32)
    n_freq = DIFF_D // 4
    inv = ROPE_BASE ** (-jnp.arange(n_freq, dtype=F32) / n_freq)
    ang = jnp.concatenate([row[:, None] * inv, col[:, None] * inv], -1)
    ang = jnp.concatenate([ang, ang, ang, ang], -1)
    sign = jnp.where((jnp.arange(LANES) % DIFF_D) < DIFF_D // 2, -1.0, 1.0)
    cos = jnp.concatenate([jnp.cos(ang), jnp.ones((lc, LANES), F32)], 0)
    sin = jnp.concatenate([jnp.sin(ang) * sign, jnp.zeros((lc, LANES), F32)], 0)
    return cos.astype(F32), sin.astype(F32)


def kernel(x, c, ctx, c_ctx, w_ada, b_ada, w_in, conv_w, gdn_a_log, gdn_dt_bias, gdn_norm_w, diff_lambda, diff_norm_w, na_rpb, w_branch, w_out, ln_g, ln_b, w_router, b_router, w_exp1, b_exp1, w_exp2, b_exp2):
    bsz, s_lat, d = x.shape
    lc = ctx.shape[1]
    depth = w_ada.shape[0]
    assert lc == TOK_TILE and s_lat % (NA_QROWS * GRID_W) == 0 and d == 1024
    lt = s_lat + lc
    nl = s_lat // TOK_TILE
    rows = s_lat // GRID_W

    xs = jnp.concatenate([x, ctx], axis=1)
    cmat = jnp.zeros((8, d), F32).at[:bsz].set(c).at[bsz].set(c_ctx)
    mod = _ada(cmat, w_ada, b_ada)
    cos, sin = _rope_tables(s_lat, lc)

    cols = np.concatenate([np.arange(0, 2048), np.arange(2064, 2064 + 3072 + 3072), np.arange(2048, 2064)])
    w_in_r = jnp.pad(w_in[:, :, cols], ((0, 0), (0, 0), (0, P_COLS - len(cols)))).astype(BF16)
    wb16 = w_branch.astype(BF16)
    wo16 = w_out.astype(BF16)
    wr_pad = jnp.pad(w_router, ((0, 0), (0, 0), (0, LANES - N_EXPERTS)))
    br_pad = jnp.pad(b_router, ((0, 0), (0, LANES - N_EXPERTS)), constant_values=NEG_INF).reshape(depth, 1, LANES)
    lane_pad = LANES - 4 * GDN_H
    na_bias = _na_bias_tables(na_rpb, rows)

    modsels = [jnp.stack([mod[l, :bsz], jnp.broadcast_to(mod[l, bsz], (bsz, 6 * d))], axis=1)[:, :, None, :]
               for l in range(depth)]
    (h1,) = _norm(xs, nl, shift=(modsels[0], 0))
    for l in range(depth):
        lam_init = 0.8 - 0.6 * math.exp(-0.3 * l)
        modsel = modsels[l]
        tm = 512 if (bsz * lt) % 512 == 0 else TOK_TILE
        p = _matmul(h1.reshape(bsz * lt, d), w_in_r[l], tm, P_COLS // 5, F32).reshape(bsz, lt, P_COLS)

        def head_row(v2):
            z4 = jnp.zeros((GDN_H,), F32)
            return jnp.pad(jnp.concatenate([v2[0], z4, v2[1], z4]), (0, lane_pad)).reshape(1, LANES)
        qkv_n, gcs, gct = _gdn_prep(p, conv_w[l], head_row(gdn_a_log[l]), head_row(gdn_dt_bias[l]), nl)
        o_f, o_b = _gdn_scan(*_gdn_intra(qkv_n, gcs, gct), s_lat // GDN_CHUNK)

        a = _attn_prep(p, cos, sin)
        lv = diff_lambda[l].astype(F32)
        lam = jnp.exp(jnp.sum(lv[0] * lv[1])) - jnp.exp(jnp.sum(lv[2] * lv[3])) + lam_init
        lam_row = jnp.full((1, LANES), lam, F32)
        tq = min(512, s_lat)
        tkv = lt // 3 if (lt // 3) % TOK_TILE == 0 else TOK_TILE
        yb = _diff_attn(a, lam_row, diff_norm_w[l], lam_init, tq=tq, tkv=tkv, q0=0, nq=s_lat // tq,
                        k0=0, nkv=lt // tkv)
        yb_ctx = _diff_attn(a, lam_row, diff_norm_w[l], lam_init, tq=lc, tkv=lc, q0=s_lat // lc, nq=1,
                            k0=s_lat // lc, nkv=1)
        yc = _na_attn(a, na_bias[l], s_lat, lc)
        yc_ctx = _na_ctx_attn(a, s_lat, lc)

        xs, h2, logits = _merge(o_f, o_b, p, yb, yb_ctx, yc, yc_ctx, xs, gdn_norm_w[l], wb16[l], wo16[l],
                                modsel, nl, ln_g[l, 0], ln_b[l, 0], wr_pad[l], br_pad[l])
        f = _moe(h2.reshape(bsz * lt, d), logits.reshape(bsz * lt, LANES),
                 w_exp1[l], b_exp1[l], w_exp2[l], b_exp2[l]).reshape(bsz, lt, d)
        if l + 1 < depth:
            xs, h1 = _norm(xs, nl, y=f, gate=(modsel, 5), ln_g=ln_g[l, 1], ln_b=ln_b[l, 1],
                           shift=(modsels[l + 1], 0))
        else:
            (xs,) = _norm(xs, nl, y=f, gate=(modsel, 5), ln_g=ln_g[l, 1], ln_b=ln_b[l, 1])
    return xs[:, :s_lat]
```

```python
import functools
import math

import numpy as np
import jax
import jax.numpy as jnp
from jax import lax
from jax.experimental import pallas as pl
from jax.experimental.pallas import tpu as pltpu

F32 = jnp.float32
BF16 = jnp.bfloat16

GRID_W = 64
GDN_H = 4
GDN_DK = 128
GDN_CHUNK = 64
DIFF_H = 4
DIFF_D = 64
ROPE_BASE = 10000.0
NA_H = 8
NA_D = 64
WIN_H = 8
WIN_W = 16
N_EXPERTS = 32
TOP_K = 4
D_EXPERT = 1024
SWIGLU_LIMIT = 7.0
SWIGLU_ALPHA = 1.702
DN_ALPHA = 8.0 ** 0.25
LN_EPS = 1e-5
RMS_EPS = 1e-6
NEG_INF = -1e30
LOG2E = math.log2(math.e)

LANES = 128
TOK_TILE = 256
NA_QROWS = 8
NA_KROWS = 16
MOE_ROWS = 256
INV_BASE_LOG2 = 3
GDN_INTRA_TILE = 128
GDN_SCAN_CHUNKS = 2
FFN_COL_CHUNK = 256
DIFF_COL_TILE = 256
MOE_TILE = 256
MOE_GBUF_ROWS = MOE_TILE * TOP_K + N_EXPERTS * 8
MOE_RUN_PIECES = tuple(MOE_TILE >> s for s in range(int(math.log2(MOE_TILE)) - 2))
MOE_GAP_PIECES = tuple((MOE_ROWS // 2) >> s for s in range(int(math.log2(MOE_ROWS)) - 3))
VMEM_LIMIT = 56 * 1024 * 1024

C_GQ, C_GK, C_GV, C_GZ = 0, 512, 1024, 1536
C_DQ, C_DK, C_DV = 2048, 2560, 3072
C_NQ, C_NK, C_NV = 3584, 4096, 4608
C_GATE = 5120
C_AB = 8192
P_COLS = 8320


def _cparams(sem):
    return pltpu.CompilerParams(dimension_semantics=sem, vmem_limit_bytes=VMEM_LIMIT)


def _ln(x):
    mu = jnp.mean(x, -1, keepdims=True)
    xc = x - mu
    var = jnp.mean(xc * xc, -1, keepdims=True)
    return xc * lax.rsqrt(var + LN_EPS)


def _sigmoid(x):
    return 1.0 / (1.0 + jnp.exp(-x))


def _dot(a, b):
    return jnp.dot(a, b, preferred_element_type=F32)


def _dot_nt(a, b):
    return lax.dot_general(a, b, (((1,), (1,)), ((), ())), preferred_element_type=F32)


def _dot_tn(a, b):
    return lax.dot_general(a, b, (((0,), (0,)), ((), ())), preferred_element_type=F32)


def _mm(a, b):
    return jnp.dot(a.astype(BF16), b.astype(BF16), preferred_element_type=F32)


def _dot_3pass(a, b):
    a_hi = a.astype(BF16)
    b_hi = b.astype(BF16)
    a_lo = (a - a_hi.astype(F32)).astype(BF16)
    b_lo = (b - b_hi.astype(F32)).astype(BF16)
    return _dot(a_hi, b_hi) + _dot(a_hi, b_lo) + _dot(a_lo, b_hi)


def _dot_f32(a, b):
    return jnp.dot(a, b, preferred_element_type=F32, precision=lax.Precision.HIGHEST)


def _ada_kernel(c_ref, w_ref, b_ref, o_ref):
    c = c_ref[...]
    s = (c * _sigmoid(c)).astype(BF16)
    o_ref[0] = _dot(s, w_ref[0].astype(BF16)) + b_ref[0]


def _ada(cmat, w_ada, b_ada):
    depth, d, n = w_ada.shape
    tn = n // 4
    return pl.pallas_call(
        _ada_kernel,
        grid=(depth, n // tn),
        in_specs=[pl.BlockSpec((8, d), lambda l, j: (0, 0)),
                  pl.BlockSpec((1, d, tn), lambda l, j: (l, 0, j)),
                  pl.BlockSpec((1, 1, tn), lambda l, j: (l, 0, j))],
        out_specs=pl.BlockSpec((1, 8, tn), lambda l, j: (l, 0, j)),
        out_shape=jax.ShapeDtypeStruct((depth, 8, n), F32),
        compiler_params=_cparams(("parallel", "parallel")),
        name="ada",
    )(cmat, w_ada, b_ada.reshape(depth, 1, n))


def _norm_kernel(*refs, has_y, has_h):
    refs = list(refs)
    x = refs.pop(0)[...]
    if has_y:
        y_ref, g_ref, lg_ref, lb_ref = refs[:4]
        del refs[:4]
        x = _ln(DN_ALPHA * x + g_ref[...] * y_ref[...]) * lg_ref[...] + lb_ref[...]
    if has_h:
        sh_ref, sc_ref = refs[:2]
        del refs[:2]
    if has_y:
        refs.pop(0)[...] = x
    if has_h:
        refs.pop(0)[...] = (_ln(x) * (1.0 + sc_ref[...]) + sh_ref[...]).astype(BF16)


def _mod_spec(d, nl, k):
    return pl.BlockSpec((None, None, 1, d), lambda b, i: (b, i // nl, 0, k))


def _norm(x, nl, *, y=None, gate=None, ln_g=None, ln_b=None, shift=None):
    bsz, lt, d = x.shape
    has_y = y is not None
    has_h = shift is not None
    tok = pl.BlockSpec((None, TOK_TILE, d), lambda b, i: (b, i, 0))
    vec = pl.BlockSpec((1, d), lambda b, i: (0, 0))
    args, specs, outs, ospecs = [x], [tok], [], []
    if has_y:
        args += [y, gate[0], ln_g.reshape(1, d), ln_b.reshape(1, d)]
        specs += [tok, _mod_spec(d, nl, gate[1]), vec, vec]
        outs.append(jax.ShapeDtypeStruct((bsz, lt, d), F32))
        ospecs.append(tok)
    if has_h:
        args += [shift[0], shift[0]]
        specs += [_mod_spec(d, nl, shift[1]), _mod_spec(d, nl, shift[1] + 1)]
        outs.append(jax.ShapeDtypeStruct((bsz, lt, d), BF16))
        ospecs.append(tok)
    return pl.pallas_call(
        functools.partial(_norm_kernel, has_y=has_y, has_h=has_h),
        grid=(bsz, lt // TOK_TILE),
        in_specs=specs, out_specs=ospecs, out_shape=outs,
        compiler_params=_cparams(("parallel", "parallel")),
        name="norm",
    )(*args)


def _mm_kernel(a_ref, w_ref, o_ref):
    o_ref[...] = _dot(a_ref[...], w_ref[...]).astype(o_ref.dtype)


def _matmul(a, w, layer, tm, tn, out_dtype):
    m, k = a.shape
    n = w.shape[2]
    return pl.pallas_call(
        _mm_kernel,
        grid=(n // tn, m // tm),
        in_specs=[pl.BlockSpec((tm, k), lambda j, i: (i, 0)),
                  pl.BlockSpec((None, k, tn), lambda j, i: (layer, 0, j))],
        out_specs=pl.BlockSpec((tm, tn), lambda j, i: (i, j)),
        out_shape=jax.ShapeDtypeStruct((m, n), out_dtype),
        compiler_params=_cparams(("parallel", "parallel")),
        name="matmul",
    )(a, w)


def _gdn_prep_kernel(x_ref, hp_ref, hn_ref, ab_ref, cw_ref, alog_ref, dtb_ref,
                     qkv_ref, gc_ref, gct_ref, *, nl):
    i = pl.program_id(1)
    x = x_ref[...]
    t = x.shape[0]
    prev_ok = jnp.logical_and(i != 0, i != nl)
    next_ok = jnp.logical_and(i != nl - 1, i != nl)
    prow = jnp.where(prev_ok, hp_ref[7:8, :], 0.0)
    nrow = jnp.where(next_ok, hn_ref[0:1, :], 0.0)
    rid = lax.broadcasted_iota(jnp.int32, x.shape, 0)
    xp = jnp.where(rid == 0, prow, pltpu.roll(x, 1, 0))
    xn = jnp.where(rid == t - 1, nrow, pltpu.roll(x, t - 1, 0))
    w = cw_ref[...]
    y = xp * w[0:1] + x * w[1:2] + xn * w[2:3]
    y = y * _sigmoid(y)
    for g in range(12):
        blk = y[:, g * LANES:(g + 1) * LANES]
        if g < 8:
            blk = blk * lax.rsqrt(jnp.sum(blk * blk, -1, keepdims=True) + RMS_EPS)
        if g < 4:
            blk = blk * (GDN_DK ** -0.5)
        qkv_ref[:, g * LANES:(g + 1) * LANES] = blk

    ab = ab_ref[...]
    lane = lax.broadcasted_iota(jnp.int32, ab.shape, 1)
    xs = ab + dtb_ref[...]
    softplus = jnp.maximum(xs, 0.0) + jnp.log(1.0 + jnp.exp(-jnp.abs(xs)))
    g = -jnp.exp(alog_ref[...]) * softplus
    beta = _sigmoid(ab)
    is_a = (lane % 8) < 4
    gb = jnp.where(lane < 16, jnp.where(is_a, g, beta), 0.0)
    r = lax.broadcasted_iota(jnp.int32, (t, t), 0)
    c = lax.broadcasted_iota(jnp.int32, (t, t), 1)
    same = (r // GDN_CHUNK) == (c // GDN_CHUNK)
    tri_l = jnp.where(jnp.logical_and(same, c <= r), 1.0, 0.0)
    tri_u = jnp.where(jnp.logical_and(same, c >= r), 1.0, 0.0)
    cf = _dot_f32(tri_l, gb)
    cb = _dot_f32(tri_u, gb)
    gc = jnp.where(lane < 4, cf, jnp.where(jnp.logical_and(lane >= 8, lane < 12), cb, gb))
    gc_ref[...] = gc
    gct = gc.T
    for ch in range(t // GDN_CHUNK):
        gct_ref[ch] = gct[0:16, ch * GDN_CHUNK:(ch + 1) * GDN_CHUNK]


def _gdn_prep(p, conv_w, alog_row, dtb_row, nl):
    bsz, lt, _ = p.shape
    nt = lt // TOK_TILE
    w3 = 3 * 512
    rb = TOK_TILE // 8
    return pl.pallas_call(
        functools.partial(_gdn_prep_kernel, nl=nl),
        grid=(bsz, nt),
        in_specs=[pl.BlockSpec((None, TOK_TILE, w3), lambda b, i: (b, i, 0)),
                  pl.BlockSpec((None, 8, w3), lambda b, i: (b, jnp.maximum(i * rb - 1, 0), 0)),
                  pl.BlockSpec((None, 8, w3), lambda b, i: (b, jnp.minimum(i * rb + rb, lt // 8 - 1), 0)),
                  pl.BlockSpec((None, TOK_TILE, LANES), lambda b, i: (b, i, C_AB // LANES)),
                  pl.BlockSpec((3, w3), lambda b, i: (0, 0)),
                  pl.BlockSpec((1, LANES), lambda b, i: (0, 0)),
                  pl.BlockSpec((1, LANES), lambda b, i: (0, 0))],
        out_specs=[pl.BlockSpec((None, TOK_TILE, w3), lambda b, i: (b, i, 0)),
                   pl.BlockSpec((None, TOK_TILE, LANES), lambda b, i: (b, i, 0)),
                   pl.BlockSpec((None, TOK_TILE // GDN_CHUNK, 16, GDN_CHUNK), lambda b, i: (b, i, 0, 0))],
        out_shape=[jax.ShapeDtypeStruct((bsz, lt, w3), F32),
                   jax.ShapeDtypeStruct((bsz, lt, LANES), F32),
                   jax.ShapeDtypeStruct((bsz, lt // GDN_CHUNK, 16, GDN_CHUNK), F32)],
        compiler_params=_cparams(("parallel", "parallel")),
        name="gdn_prep",
    )(p, p, p, p, conv_w, alog_row, dtb_row)


def _gdn_intra_kernel(qkv_ref, gc_ref, gct_ref, u_ref, w_ref, qg_ref, kt_ref, qk_ref, eg_ref):
    cs = GDN_CHUNK
    nch = qkv_ref.shape[0] // cs
    chains = [(d, c, h) for d in range(2) for c in range(nch) for h in range(GDN_H)]
    ri = lax.broadcasted_iota(jnp.int32, (cs, cs), 0)
    ci = lax.broadcasted_iota(jnp.int32, (cs, cs), 1)
    eye = ri == ci
    base = (ri >> INV_BASE_LOG2) == (ci >> INV_BASE_LOG2)
    incl = (ri >= ci, ri <= ci)
    strict = (ri > ci, ri < ci)
    qkv = qkv_ref[...]
    gc = gc_ref[...]

    def part(col0, c, h):
        return qkv[c * cs:(c + 1) * cs, col0 + h * LANES:col0 + (h + 1) * LANES]

    ch_keys = [(c, h) for c in range(nch) for h in range(GDN_H)]
    q = {key: part(C_GQ, *key) for key in ch_keys}
    k = {key: part(C_GK, *key) for key in ch_keys}
    v = {key: part(C_GV, *key) for key in ch_keys}
    q16 = {key: q[key].astype(BF16) for key in ch_keys}
    k16 = {key: k[key].astype(BF16) for key in ch_keys}
    gcol = [gc[c * cs:(c + 1) * cs, 8 * d + h:8 * d + h + 1] for d, c, h in chains]
    bcol = [gc[c * cs:(c + 1) * cs, 8 * d + 4 + h:8 * d + 5 + h] for d, c, h in chains]
    grow = [gct_ref[c, 8 * d + h:8 * d + h + 1, :] for d, c, h in chains]
    glast = [g[cs - 1:cs, :] if d == 0 else g[0:1, :] for g, (d, c, h) in zip(gcol, chains)]
    decay = [jnp.where(incl[d], jnp.exp(jnp.where(incl[d], gcol[n] - grow[n], 0.0)), 0.0)
             for n, (d, c, h) in enumerate(chains)]
    kb = [k[(c, h)] * bcol[n] for n, (d, c, h) in enumerate(chains)]
    amat = [jnp.where(strict[d], _dot_nt(kb[n].astype(BF16), k16[(c, h)]) * decay[n], 0.0)
            for n, (d, c, h) in enumerate(chains)]
    qk = [_dot_nt(q16[(c, h)], k16[(c, h)]) * decay[n] for n, (d, c, h) in enumerate(chains)]
    eg = [jnp.exp(g) for g in gcol]
    rhs = [jnp.concatenate([v[(c, h)] * bcol[n], kb[n] * eg[n]], axis=1) for n, (d, c, h) in enumerate(chains)]
    rmat = [jnp.where(base, -a, 0.0) for a in amat]
    mmat = rmat
    for _ in range(INV_BASE_LOG2 - 1):
        mmat = [_mm(m, m) for m in mmat]
        prod = [_mm(r, m) for r, m in zip(rmat, mmat)]
        rmat = [r + m + p for r, m, p in zip(rmat, mmat, prod)]
    tmat = [jnp.where(eye, 1.0, r) for r in rmat]
    for lb in range(INV_BASE_LOG2, int(math.log2(cs))):
        same_pair = (ri >> (lb + 1)) == (ci >> (lb + 1))
        off = (jnp.logical_and((ri >> lb) == (ci >> lb) + 1, same_pair),
               jnp.logical_and((ci >> lb) == (ri >> lb) + 1, same_pair))
        inner = [_mm(jnp.where(off[d], amat[n], 0.0), tmat[n]) for n, (d, c, h) in enumerate(chains)]
        outer = [_mm(t, x) for t, x in zip(tmat, inner)]
        tmat = [t - x for t, x in zip(tmat, outer)]
    corr = [_mm(jnp.where(eye, 0.0, t), r) for t, r in zip(tmat, rhs)]
    for n, (d, c, h) in enumerate(chains):
        rows = slice(c * cs, (c + 1) * cs)
        cols = slice(h * LANES, (h + 1) * LANES)
        sol = rhs[n] + corr[n]
        u_ref[d, rows, cols] = sol[:, :LANES]
        w_ref[d, rows, cols] = sol[:, LANES:].astype(BF16)
        qg_ref[d, rows, cols] = (q[(c, h)] * eg[n]).astype(BF16)
        kt_ref[d, rows, cols] = (k[(c, h)] * jnp.exp(glast[n] - gcol[n])).astype(BF16)
        qk_ref[d, c, h] = qk[n].astype(BF16)
        eg_ref[d, c, h:h + 1, :] = jnp.broadcast_to(jnp.exp(glast[n]), (1, LANES))


def _gdn_intra(qkv, gc, gct):
    bsz, lt, w3 = qkv.shape
    cs = GDN_CHUNK
    nc = lt // cs
    nch = GDN_INTRA_TILE // cs

    def tok(dt):
        return (pl.BlockSpec((None, 2, GDN_INTRA_TILE, 512), lambda b, i: (b, 0, i, 0)),
                jax.ShapeDtypeStruct((bsz, 2, lt, 512), dt))

    outs = [tok(F32), tok(BF16), tok(BF16), tok(BF16),
            (pl.BlockSpec((None, 2, nch, GDN_H, cs, cs), lambda b, i: (b, 0, i, 0, 0, 0)),
             jax.ShapeDtypeStruct((bsz, 2, nc, GDN_H, cs, cs), BF16)),
            (pl.BlockSpec((None, 2, nch, GDN_H, LANES), lambda b, i: (b, 0, i, 0, 0)),
             jax.ShapeDtypeStruct((bsz, 2, nc, GDN_H, LANES), F32))]
    return pl.pallas_call(
        _gdn_intra_kernel,
        grid=(bsz, lt // GDN_INTRA_TILE),
        in_specs=[pl.BlockSpec((None, GDN_INTRA_TILE, w3), lambda b, i: (b, i, 0)),
                  pl.BlockSpec((None, GDN_INTRA_TILE, LANES), lambda b, i: (b, i, 0)),
                  pl.BlockSpec((None, nch, 16, cs), lambda b, i: (b, i, 0, 0))],
        out_specs=[o[0] for o in outs],
        out_shape=[o[1] for o in outs],
        compiler_params=_cparams(("parallel", "parallel")),
        name="gdn_intra",
    )(qkv, gc, gct)


def _gdn_scan_kernel(*refs):
    s_ref = refs[-1]
    cs = GDN_CHUNK

    @pl.when(pl.program_id(1) == 0)
    def _():
        s_ref[...] = jnp.zeros_like(s_ref)

    chains = [(d, h) for d in range(2) for h in range(GDN_H)]
    src = (refs[0:6], refs[6:12])
    outs = refs[12:14]
    per_step = outs[0].shape[0] // cs

    def cols(h):
        return slice(h * LANES, (h + 1) * LANES)

    s = [s_ref[d * GDN_H + h] for d, h in chains]
    for sub in range(per_step):
        cidx = (sub, per_step - 1 - sub)
        rows = [slice(c * cs, (c + 1) * cs) for c in cidx]
        wq = [_dot(jnp.concatenate([src[d][1][rows[d], cols(h)], src[d][2][rows[d], cols(h)]], axis=0),
                   s[n].astype(BF16)) for n, (d, h) in enumerate(chains)]
        v16 = [(src[d][0][rows[d], cols(h)] - wq[n][:cs]).astype(BF16) for n, (d, h) in enumerate(chains)]
        o = [wq[n][cs:] + _dot(src[d][4][cidx[d], h], v16[n]) for n, (d, h) in enumerate(chains)]
        s = [s[n] * src[d][5][cidx[d], h:h + 1, :] + _dot_tn(src[d][3][rows[d], cols(h)], v16[n])
             for n, (d, h) in enumerate(chains)]
        for n, (d, h) in enumerate(chains):
            outs[d][rows[d], cols(h)] = o[n]
    for n, (d, h) in enumerate(chains):
        s_ref[d * GDN_H + h] = s[n]


def _gdn_scan(u, w, qg, kt, qk, eg, n_lat_chunks):
    bsz, _, lt, _ = u.shape
    per = GDN_SCAN_CHUNKS
    cs = GDN_CHUNK * per
    nc = lt // cs
    n_lat = n_lat_chunks // per
    ncc = nc - n_lat

    def cf(i):
        return jnp.where(i < ncc, n_lat + i, i - ncc)

    def cb(i):
        return nc - 1 - i

    def specs(d, c):
        tok = pl.BlockSpec((None, None, cs, 512), lambda b, i: (b, d, c(i), 0))
        return [tok, tok, tok, tok,
                pl.BlockSpec((None, None, per, GDN_H, GDN_CHUNK, GDN_CHUNK), lambda b, i: (b, d, c(i), 0, 0, 0)),
                pl.BlockSpec((None, None, per, GDN_H, LANES), lambda b, i: (b, d, c(i), 0, 0))]

    def out(c):
        return pl.BlockSpec((None, cs, 512), lambda b, i: (b, c(i), 0))

    args = (u, w, qg, kt, qk, eg)
    return pl.pallas_call(
        _gdn_scan_kernel,
        grid=(bsz, nc),
        in_specs=specs(0, cf) + specs(1, cb),
        out_specs=[out(cf), out(cb)],
        out_shape=[jax.ShapeDtypeStruct((bsz, lt, 512), F32)] * 2,
        scratch_shapes=[pltpu.VMEM((2 * GDN_H, GDN_DK, LANES), F32)],
        compiler_params=_cparams(("parallel", "arbitrary")),
        name="gdn_scan",
    )(*args, *args)


def _attn_prep_kernel(dq_ref, dk_ref, dv_ref, nq_ref, nk_ref, nv_ref, cos_ref, sin_ref, o_ref):
    cos = cos_ref[...]
    sin = sin_ref[...]
    lane = lax.broadcasted_iota(jnp.int32, cos.shape, 1)
    first = (lane % DIFF_D) < DIFF_D // 2

    def rope(x):
        rot = jnp.where(first, pltpu.roll(x, LANES - DIFF_D // 2, 1), pltpu.roll(x, DIFF_D // 2, 1))
        return x * cos + rot * sin

    for h in range(DIFF_H):
        sl = slice(h * LANES, (h + 1) * LANES)
        o_ref[:, h * LANES:(h + 1) * LANES] = (rope(dq_ref[:, sl]) * (DIFF_D ** -0.5 * LOG2E)).astype(BF16)
        o_ref[:, 512 + h * LANES:512 + (h + 1) * LANES] = rope(dk_ref[:, sl]).astype(BF16)
    o_ref[:, 1024:1536] = dv_ref[...].astype(BF16)
    o_ref[:, 1536:2048] = (nq_ref[...] * (NA_D ** -0.5)).astype(BF16)
    o_ref[:, 2048:2560] = nk_ref[...].astype(BF16)
    o_ref[:, 2560:3072] = nv_ref[...].astype(BF16)


def _attn_prep(p, cos, sin):
    bsz, lt, _ = p.shape

    def col(cb):
        return pl.BlockSpec((None, TOK_TILE, 512), lambda b, i: (b, i, cb))

    tab = pl.BlockSpec((TOK_TILE, LANES), lambda b, i: (i, 0))
    return pl.pallas_call(
        _attn_prep_kernel,
        grid=(bsz, lt // TOK_TILE),
        in_specs=[col(C_DQ // 512), col(C_DK // 512), col(C_DV // 512),
                  col(C_NQ // 512), col(C_NK // 512), col(C_NV // 512), tab, tab],
        out_specs=pl.BlockSpec((None, TOK_TILE, 3072), lambda b, i: (b, i, 0)),
        out_shape=jax.ShapeDtypeStruct((bsz, lt, 3072), BF16),
        compiler_params=_cparams(("parallel", "parallel")),
        name="attn_prep",
    )(p, p, p, p, p, p, cos, sin)


def _diff_kernel(lam_ref, q_ref, k_ref, v_ref, nw_ref, o_ref, m_sc, l_sc, acc_sc, *, nkv, lam_init):
    kv = pl.program_id(3)

    @pl.when(kv == 0)
    def _():
        m_sc[...] = jnp.full_like(m_sc, NEG_INF)
        l_sc[...] = jnp.zeros_like(l_sc)
        acc_sc[...] = jnp.zeros_like(acc_sc)

    q = q_ref[...]
    k = k_ref[...]
    v = v_ref[...]
    maps = range(2)
    ct = min(DIFF_COL_TILE, k.shape[0])
    tiles = range(k.shape[0] // ct)

    def fold(acc, x, op):
        for c0 in range(0, x.shape[1], LANES):
            piece = x[:, c0:c0 + LANES]
            acc = piece if acc is None else op(acc, piece)
        return acc

    s = [[_dot_nt(q[:, m * DIFF_D:(m + 1) * DIFF_D], k[j * ct:(j + 1) * ct, m * DIFF_D:(m + 1) * DIFF_D])
          for j in tiles] for m in maps]
    m_new, alpha = [], []
    for m in maps:
        mx = None
        for j in tiles:
            mx = fold(mx, s[m][j], jnp.maximum)
        m_prev = m_sc[m]
        m_new.append(jnp.maximum(m_prev, jnp.max(mx, -1, keepdims=True)))
        alpha.append(jnp.exp2(m_prev - m_new[m]))
    for m in maps:
        sm, pv = None, None
        for j in tiles:
            p = jnp.exp2(s[m][j] - m_new[m])
            sm = fold(sm, p, jnp.add)
            part = _dot(p.astype(BF16), v[j * ct:(j + 1) * ct, :])
            pv = part if pv is None else pv + part
        l_sc[m] = alpha[m] * l_sc[m] + jnp.sum(sm, -1, keepdims=True)
        acc_sc[m] = alpha[m] * acc_sc[m] + pv
        m_sc[m] = m_new[m]

    @pl.when(kv == nkv - 1)
    def _():
        lam = lam_ref[0:1, 0:1]
        o = acc_sc[0] / l_sc[0] - lam * (acc_sc[1] / l_sc[1])
        y = o * lax.rsqrt(jnp.mean(o * o, -1, keepdims=True) + RMS_EPS) * nw_ref[...] * (1.0 - lam_init)
        o_ref[...] = y.astype(BF16)


def _diff_attn(a, lam_row, norm_w, lam_init, *, tq, tkv, q0, nq, k0, nkv):
    bsz = a.shape[0]
    return pl.pallas_call(
        functools.partial(_diff_kernel, nkv=nkv, lam_init=lam_init),
        grid=(bsz, DIFF_H, nq, nkv),
        in_specs=[pl.BlockSpec((1, LANES), lambda b, h, i, j: (0, 0)),
                  pl.BlockSpec((None, tq, LANES), lambda b, h, i, j: (b, q0 + i, h)),
                  pl.BlockSpec((None, tkv, LANES), lambda b, h, i, j: (b, k0 + j, 4 + h)),
                  pl.BlockSpec((None, tkv, LANES), lambda b, h, i, j: (b, k0 + j, 8 + h)),
                  pl.BlockSpec((1, LANES), lambda b, h, i, j: (0, 0))],
        out_specs=pl.BlockSpec((None, tq, LANES), lambda b, h, i, j: (b, i, h)),
        out_shape=jax.ShapeDtypeStruct((bsz, nq * tq, 512), BF16),
        scratch_shapes=[pltpu.VMEM((2, tq, 1), F32), pltpu.VMEM((2, tq, 1), F32),
                        pltpu.VMEM((2, tq, LANES), F32)],
        compiler_params=_cparams(("parallel", "parallel", "parallel", "arbitrary")),
        name="diff_attn",
    )(lam_row, a, a, a, norm_w.reshape(1, LANES))


def _na_kernel(q_ref, k_ref, v_ref, bias_ref, o_ref, *, s_lat, lc):
    i = pl.program_id(2)
    tq = q_ref.shape[0]
    nk = NA_KROWS * GRID_W
    kstart = jnp.clip(i * tq - (NA_KROWS - NA_QROWS) // 2 * GRID_W, 0, s_lat - nk)
    kstart = pl.multiple_of(kstart, TOK_TILE)
    q = q_ref[...]
    k_loc = k_ref[pl.ds(kstart, nk), :]
    v_loc = v_ref[pl.ds(kstart, nk), :]
    k_ctx = k_ref[s_lat:s_lat + lc, :]
    v_ctx = v_ref[s_lat:s_lat + lc, :]
    outs = []
    for j in range(2):
        sl = slice(j * NA_D, (j + 1) * NA_D)
        qh = q[:, sl]
        s_loc = _dot_nt(qh, k_loc[:, sl]) + bias_ref[j]
        s_ctx = _dot_nt(qh, k_ctx[:, sl])
        m = jnp.maximum(jnp.max(s_loc, -1, keepdims=True), jnp.max(s_ctx, -1, keepdims=True))
        p_loc = jnp.exp(s_loc - m)
        p_ctx = jnp.exp(s_ctx - m)
        l = jnp.sum(p_loc, -1, keepdims=True) + jnp.sum(p_ctx, -1, keepdims=True)
        o = _dot(p_loc.astype(BF16), v_loc[:, sl]) + _dot(p_ctx.astype(BF16), v_ctx[:, sl])
        outs.append(o / l)
    o_ref[...] = jnp.concatenate(outs, axis=1).astype(BF16)


def _na_attn(a, bias, layer, s_lat, lc):
    bsz, lt, _ = a.shape
    tq = NA_QROWS * GRID_W
    nq = s_lat // tq

    def variant(i):
        return jnp.where(i == 0, 0, jnp.where(i == nq - 1, 2, 1))

    return pl.pallas_call(
        functools.partial(_na_kernel, s_lat=s_lat, lc=lc),
        grid=(bsz, NA_H // 2, nq),
        in_specs=[pl.BlockSpec((None, tq, LANES), lambda b, h, i: (b, i, 12 + h)),
                  pl.BlockSpec((None, lt, LANES), lambda b, h, i: (b, 0, 16 + h)),
                  pl.BlockSpec((None, lt, LANES), lambda b, h, i: (b, 0, 20 + h)),
                  pl.BlockSpec((None, None, 2, tq, NA_KROWS * GRID_W),
                               lambda b, h, i: (layer, variant(i), h, 0, 0))],
        out_specs=pl.BlockSpec((None, tq, LANES), lambda b, h, i: (b, i, h)),
        out_shape=jax.ShapeDtypeStruct((bsz, s_lat, 512), BF16),
        compiler_params=_cparams(("parallel", "parallel", "arbitrary")),
        name="na_attn",
    )(a, a, a, bias)


def _na_ctx_kernel(q_ref, k_ref, v_ref, o_ref):
    q = q_ref[...]
    k = k_ref[...]
    v = v_ref[...]
    outs = []
    for j in range(2):
        sl = slice(j * NA_D, (j + 1) * NA_D)
        s = _dot_nt(q[:, sl], k[:, sl])
        p = jnp.exp(s - jnp.max(s, -1, keepdims=True))
        outs.append(_dot(p.astype(BF16), v[:, sl]) / jnp.sum(p, -1, keepdims=True))
    o_ref[...] = jnp.concatenate(outs, axis=1).astype(BF16)


def _na_ctx_attn(a, s_lat, lc):
    bsz = a.shape[0]
    rb = s_lat // lc

    def blk(c0):
        return pl.BlockSpec((None, lc, LANES), lambda b, h: (b, rb, c0 + h))

    return pl.pallas_call(
        _na_ctx_kernel,
        grid=(bsz, NA_H // 2),
        in_specs=[blk(12), blk(16), blk(20)],
        out_specs=pl.BlockSpec((None, lc, LANES), lambda b, h: (b, 0, h)),
        out_shape=jax.ShapeDtypeStruct((bsz, lc, 512), BF16),
        compiler_params=_cparams(("parallel", "parallel")),
        name="na_ctx_attn",
    )(a, a, a)


def _na_bias_tables(rpb, rows):
    qr = np.arange(NA_QROWS)[:, None]
    kr = np.arange(NA_KROWS)[None, :]
    qc = np.arange(GRID_W)[:, None]
    kc = np.arange(GRID_W)[None, :]
    w0 = np.clip(qc - WIN_W // 2, 0, GRID_W - WIN_W)
    okc = (kc >= w0) & (kc < w0 + WIN_W)
    dc = np.clip(kc - qc + WIN_W - 1, 0, 2 * WIN_W - 2)
    half = (NA_KROWS - NA_QROWS) // 2
    n_dr, n_dc = 2 * WIN_H - 1, 2 * WIN_W - 1
    sel_c = (dc[None] == np.arange(n_dc)[:, None, None]) & okc[None]
    sel_r = np.zeros((3, NA_QROWS, NA_KROWS, n_dr), bool)
    for vi, (r_start, delta) in enumerate(((0, 0), (NA_QROWS, -half), (rows - NA_QROWS, -2 * half))):
        r = r_start + qr
        kabs = r_start + delta + kr
        r0 = np.clip(r - WIN_H // 2, 0, rows - WIN_H)
        okr = (kabs >= r0) & (kabs < r0 + WIN_H)
        dr = np.clip(kabs - r + WIN_H - 1, 0, n_dr - 1)
        sel_r[vi] = (dr[..., None] == np.arange(n_dr)) & okr[..., None]
    toep = jnp.einsum('lhaj,jqk->lhaqk', rpb, sel_c.astype(np.float32), precision=lax.Precision.HIGHEST)
    tab = jnp.einsum('vrsa,lhaqk->lvhrqsk', sel_r.astype(np.float32), toep, precision=lax.Precision.HIGHEST)
    ok = sel_r.any(-1)[None, :, None, :, None, :, None] & okc[None, None, None, None, :, None, :]
    tab = jnp.where(ok, tab, NEG_INF)
    return tab.reshape(rpb.shape[0], 3, NA_H, NA_QROWS * GRID_W, NA_KROWS * GRID_W).astype(F32)


def _merge_kernel(of_ref, ob_ref, z_ref, ybl_ref, ybc_ref, ycl_ref, ycc_ref, ga_ref, gb_ref, gc_ref, x_ref,
                  gnw_ref, wb_ref, wo_ref, g1_ref, lg_ref, lb_ref, sh_ref, sc_ref, wr_ref, br_ref,
                  xo_ref, h_ref, lg_out_ref, *, nl):
    is_ctx = pl.program_id(1) >= nl
    yb = jnp.where(is_ctx, ybc_ref[...], ybl_ref[...])
    yc = jnp.where(is_ctx, ycc_ref[...], ycl_ref[...])
    o = of_ref[...] + ob_ref[...]
    z = z_ref[...]
    gnw = gnw_ref[...]
    parts = []
    for h in range(GDN_H):
        sl = slice(h * LANES, (h + 1) * LANES)
        oh = o[:, sl]
        zh = z[:, sl]
        parts.append(oh * lax.rsqrt(jnp.mean(oh * oh, -1, keepdims=True) + RMS_EPS) * gnw * (zh * _sigmoid(zh)))
    ya = jnp.concatenate(parts, axis=1).astype(BF16)
    m = (_sigmoid(ga_ref[...]) * _dot(ya, wb_ref[0])
         + _sigmoid(gb_ref[...]) * _dot(yb, wb_ref[1])
         + _sigmoid(gc_ref[...]) * _dot(yc, wb_ref[2]))
    mx = _dot(m.astype(BF16), wo_ref[...])
    x = _ln(DN_ALPHA * x_ref[...] + g1_ref[...] * mx) * lg_ref[...] + lb_ref[...]
    xo_ref[...] = x
    hh = _ln(x) * (1.0 + sc_ref[...]) + sh_ref[...]
    h_ref[...] = hh.astype(BF16)
    lg_out_ref[...] = _dot_3pass(hh, wr_ref[...]) + br_ref[...]


def _merge(o_f, o_b, p, yb, yb_ctx, yc, yc_ctx, x, gnw, wb, wo, modsel, nl, ln_g, ln_b, wr, br):
    bsz, lt, d = x.shape

    def tok(width, cb):
        return pl.BlockSpec((None, TOK_TILE, width), lambda b, i: (b, i, cb))

    def const(shape):
        return pl.BlockSpec(shape, lambda b, i: (0,) * len(shape))

    lat = pl.BlockSpec((None, TOK_TILE, 512), lambda b, i: (b, jnp.minimum(i, nl - 1), 0))
    cxt = pl.BlockSpec((None, TOK_TILE, 512), lambda b, i: (b, 0, 0))
    return pl.pallas_call(
        functools.partial(_merge_kernel, nl=nl),
        grid=(bsz, lt // TOK_TILE),
        in_specs=[tok(512, 0), tok(512, 0), tok(512, C_GZ // 512), lat, cxt, lat, cxt,
                  tok(d, C_GATE // d), tok(d, C_GATE // d + 1), tok(d, C_GATE // d + 2), tok(d, 0),
                  const((1, LANES)), const((3, 512, d)), const((d, d)),
                  _mod_spec(d, nl, 2), const((1, d)), const((1, d)),
                  _mod_spec(d, nl, 3), _mod_spec(d, nl, 4),
                  const((d, LANES)), const((1, LANES))],
        out_specs=[tok(d, 0), tok(d, 0), tok(LANES, 0)],
        out_shape=[jax.ShapeDtypeStruct((bsz, lt, d), F32),
                   jax.ShapeDtypeStruct((bsz, lt, d), BF16),
                   jax.ShapeDtypeStruct((bsz, lt, LANES), F32)],
        compiler_params=_cparams(("parallel", "parallel")),
        name="merge",
    )(o_f, o_b, p, yb, yb_ctx, yc, yc_ctx, p, p, p, x, gnw.reshape(1, LANES), wb, wo,
      modsel, ln_g.reshape(1, d), ln_b.reshape(1, d), modsel, modsel, wr, br)


def _ffn_kernel(be_ref, nu_ref, x_ref, w1_ref, b1_ref, w2_ref, b2_ref, o_ref, w1c, w2c):
    i = pl.program_id(0)

    @pl.when(i < nu_ref[0])
    def _():
        changed = jnp.logical_or(i == 0, be_ref[i] != be_ref[jnp.maximum(i - 1, 0)])

        @pl.when(changed)
        def _():
            w1c[...] = w1_ref[...].astype(BF16)
            w2c[...] = w2_ref[...].astype(BF16)

        x = x_ref[...].astype(BF16)
        y = None
        for c0 in range(0, D_EXPERT, FFN_COL_CHUNK):
            c1 = c0 + FFN_COL_CHUNK
            hg = _dot(x, w1c[:, c0:c1]) + b1_ref[:, c0:c1]
            hl = _dot(x, w1c[:, D_EXPERT + c0:D_EXPERT + c1]) + b1_ref[:, D_EXPERT + c0:D_EXPERT + c1]
            x_glu = jnp.minimum(hg, SWIGLU_LIMIT)
            x_lin = jnp.clip(hl, -SWIGLU_LIMIT, SWIGLU_LIMIT)
            act = x_glu * _sigmoid(SWIGLU_ALPHA * x_glu) * (x_lin + 1.0)
            part = _dot(act.astype(BF16), w2c[c0:c1, :])
            y = part if y is None else y + part
        o_ref[...] = y + b2_ref[...]

    @pl.when(i >= nu_ref[0])
    def _():
        o_ref[...] = jnp.zeros_like(o_ref)


def _ffn(blk_e, n_used, xs, w1, b1, w2, b2, layer):
    n_rows, d = xs.shape
    n_blk = n_rows // MOE_ROWS
    depth, n_exp, _, de2 = w1.shape
    gs = pltpu.PrefetchScalarGridSpec(
        num_scalar_prefetch=2,
        grid=(n_blk,),
        in_specs=[pl.BlockSpec((MOE_ROWS, d), lambda i, be, nu: (i, 0)),
                  pl.BlockSpec((None, None, d, de2), lambda i, be, nu: (layer, be[i], 0, 0)),
                  pl.BlockSpec((None, None, 1, de2), lambda i, be, nu: (layer, be[i], 0, 0)),
                  pl.BlockSpec((None, None, de2 // 2, d), lambda i, be, nu: (layer, be[i], 0, 0)),
                  pl.BlockSpec((None, None, 1, d), lambda i, be, nu: (layer, be[i], 0, 0))],
        out_specs=pl.BlockSpec((MOE_ROWS, d), lambda i, be, nu: (i, 0)),
        scratch_shapes=[pltpu.VMEM((d, de2), BF16), pltpu.VMEM((de2 // 2, d), BF16)],
    )
    return pl.pallas_call(
        _ffn_kernel,
        grid_spec=gs,
        out_shape=jax.ShapeDtypeStruct((n_rows, d), F32),
        compiler_params=_cparams(("arbitrary",)),
        name="ffn",
    )(blk_e, n_used, xs, w1, b1.reshape(depth, n_exp, 1, de2), w2, b2.reshape(depth, n_exp, 1, d))


def _route_kernel(lg_ref, rc_ref, rt_ref, n8_ref):
    l = lg_ref[...]
    tt = l.shape[0]
    lane = lax.broadcasted_iota(jnp.int32, l.shape, 1)
    picks, vals = [], []
    for _ in range(TOP_K):
        m = jnp.max(l, -1, keepdims=True)
        idx = jnp.min(jnp.where(l == m, lane, LANES), -1, keepdims=True)
        oh = lane == idx
        picks.append(oh)
        vals.append(m)
        l = jnp.where(oh, -jnp.inf, l)
    sel = sum(jnp.where(oh, 1.0, 0.0) for oh in picks)
    cnt = jnp.sum(sel, 0, keepdims=True)
    n8 = jnp.floor((cnt + 7.0) * 0.125) * 8.0
    r = lax.broadcasted_iota(jnp.int32, (LANES, LANES), 0)
    c = lax.broadcasted_iota(jnp.int32, (LANES, LANES), 1)
    off8 = _dot_f32(jnp.broadcast_to(n8, (8, LANES)), jnp.where(r < c, 1.0, 0.0))[0:1]
    tr = lax.broadcasted_iota(jnp.int32, (tt, tt), 0)
    tc = lax.broadcasted_iota(jnp.int32, (tt, tt), 1)
    rank = _dot(jnp.where(tc < tr, 1.0, 0.0).astype(BF16), sel.astype(BF16))
    slot = off8 + rank
    es = [jnp.exp(v - vals[0]) for v in vals]
    den = sum(es)
    rc = jnp.zeros(l.shape, F32)
    for kk in range(TOP_K):
        loc = jnp.sum(jnp.where(picks[kk], slot, 0.0), -1, keepdims=True)
        rc = jnp.where(lane == kk, loc, rc)
        rc = jnp.where(lane == TOP_K + kk, es[kk] / den, rc)
    rc_ref[...] = rc
    rt_ref[...] = rc.T[0:8, :]
    n8_ref[...] = jnp.broadcast_to(n8, (8, LANES))


def _route(logits):
    t = logits.shape[0]
    nt = t // MOE_TILE
    return pl.pallas_call(
        _route_kernel,
        grid=(nt,),
        in_specs=[pl.BlockSpec((MOE_TILE, LANES), lambda i: (i, 0))],
        out_specs=[pl.BlockSpec((MOE_TILE, LANES), lambda i: (i, 0)),
                   pl.BlockSpec((None, 8, MOE_TILE), lambda i: (i, 0, 0)),
                   pl.BlockSpec((None, 8, LANES), lambda i: (i, 0, 0))],
        out_shape=[jax.ShapeDtypeStruct((t, LANES), F32),
                   jax.ShapeDtypeStruct((nt, 8, MOE_TILE), F32),
                   jax.ShapeDtypeStruct((nt, 8, LANES), F32)],
        compiler_params=_cparams(("parallel",)),
        name="moe_route",
    )(logits)


def _run_copies(n, pieces, make_copy, wait):
    done = jnp.int32(0)
    for size in pieces:
        take = (n & size) != 0

        @pl.when(take)
        def _():
            cp = make_copy(done, size)
            if wait:
                cp.wait()
            else:
                cp.start()

        done = done + jnp.where(take, size, 0)


def _tile_runs(gs_ref, n8_ref, tile, make_copy, wait):
    def body(e, local):
        n = n8_ref[tile * N_EXPERTS + e]
        g = gs_ref[tile * N_EXPERTS + e]
        _run_copies(n, MOE_RUN_PIECES,
                    lambda done, size: make_copy(pl.multiple_of(local + done, 8),
                                                 pl.multiple_of(g + done, 8), size), wait)
        return local + n
    lax.fori_loop(0, N_EXPERTS, body, jnp.int32(0))


def _dispatch_kernel(gs_ref, n8_ref, go_ref, gn_ref, h_ref, rt_ref, xs_ref, gbuf, zbuf, sems):
    i = pl.program_id(0)
    slot = i % 2
    sem = sems.at[0]

    @pl.when(i == 0)
    def _():
        zbuf[...] = jnp.zeros_like(zbuf)
        zrows = zbuf.shape[0]
        for wait in (False, True):
            def body(e, carry):
                _run_copies(gn_ref[e], MOE_GAP_PIECES,
                            lambda done, size: pltpu.make_async_copy(
                                zbuf.at[pl.ds(0, size), :],
                                xs_ref.at[pl.ds(pl.multiple_of(go_ref[e] + done, 8), size), :], sem), wait)
                return carry
            lax.fori_loop(0, N_EXPERTS, body, 0)

            def tail(j, carry):
                cp = pltpu.make_async_copy(
                    zbuf, xs_ref.at[pl.ds(pl.multiple_of(go_ref[N_EXPERTS] + j * zrows, 8), zrows), :], sem)
                if wait:
                    cp.wait()
                else:
                    cp.start()
                return carry
            lax.fori_loop(0, gn_ref[N_EXPERTS], tail, 0)

    loc = rt_ref[0:TOP_K, :].astype(jnp.int32)
    row = lax.broadcasted_iota(jnp.int32, (gbuf.shape[1], loc.shape[1]), 0)
    hit = row == loc[0:1, :]
    for kk in range(1, TOP_K):
        hit = jnp.logical_or(hit, row == loc[kk:kk + 1, :])
    gbuf[slot] = _dot(jnp.where(hit, 1.0, 0.0).astype(BF16), h_ref[...])

    def runs(tile, sl, wait):
        _tile_runs(gs_ref, n8_ref, tile, lambda local, g, size: pltpu.make_async_copy(
            gbuf.at[sl, pl.ds(local, size), :], xs_ref.at[pl.ds(g, size), :], sems.at[sl]), wait)

    runs(i, slot, False)

    @pl.when(i > 0)
    def _():
        runs(i - 1, 1 - slot, True)

    @pl.when(i == pl.num_programs(0) - 1)
    def _():
        runs(i, slot, True)


def _dispatch(gstart, n8, gap_off, gap_n, h2, rt, n_rows):
    t, d = h2.shape
    nt = t // MOE_TILE
    gs = pltpu.PrefetchScalarGridSpec(
        num_scalar_prefetch=4,
        grid=(nt,),
        in_specs=[pl.BlockSpec((MOE_TILE, d), lambda i, *_: (i, 0)),
                  pl.BlockSpec((None, 8, MOE_TILE), lambda i, *_: (i, 0, 0))],
        out_specs=pl.BlockSpec(memory_space=pl.ANY),
        scratch_shapes=[pltpu.VMEM((2, MOE_GBUF_ROWS, d), F32), pltpu.VMEM((MOE_ROWS // 2, d), F32),
                        pltpu.SemaphoreType.DMA((2,))],
    )
    return pl.pallas_call(
        _dispatch_kernel,
        grid_spec=gs,
        out_shape=jax.ShapeDtypeStruct((n_rows, d), F32),
        compiler_params=_cparams(("arbitrary",)),
        name="moe_dispatch",
    )(gstart, n8, gap_off, gap_n, h2, rt)


def _combine_kernel(gs_ref, n8_ref, ys_ref, rc_ref, f_ref, ybuf, sems):
    i = pl.program_id(0)
    slot = i % 2

    def runs(tile, sl, wait):
        _tile_runs(gs_ref, n8_ref, tile, lambda local, g, size: pltpu.make_async_copy(
            ys_ref.at[pl.ds(g, size), :], ybuf.at[sl, pl.ds(local, size), :], sems.at[sl]), wait)

    @pl.when(i == 0)
    def _():
        ybuf[...] = jnp.zeros_like(ybuf)
        runs(i, slot, False)

    runs(i, slot, True)

    @pl.when(i + 1 < pl.num_programs(0))
    def _():
        runs(i + 1, 1 - slot, False)

    rc = rc_ref[...]
    col = lax.broadcasted_iota(jnp.int32, (rc.shape[0], ybuf.shape[1]), 1)
    wgt = jnp.zeros(col.shape, F32)
    for kk in range(TOP_K):
        wgt = jnp.where(col == rc[:, kk:kk + 1].astype(jnp.int32), rc[:, TOP_K + kk:TOP_K + kk + 1], wgt)
    y = ybuf[slot]
    w_hi = wgt.astype(BF16)
    w_lo = (wgt - w_hi.astype(F32)).astype(BF16)
    y_hi = y.astype(BF16)
    y_lo = (y - y_hi.astype(F32)).astype(BF16)
    f_ref[...] = _dot(w_hi, y_hi) + _dot(w_hi, y_lo) + _dot(w_lo, y_hi)


def _combine(gstart, n8, ys, rc):
    t = rc.shape[0]
    d = ys.shape[1]
    nt = t // MOE_TILE
    gs = pltpu.PrefetchScalarGridSpec(
        num_scalar_prefetch=2,
        grid=(nt,),
        in_specs=[pl.BlockSpec(memory_space=pl.ANY),
                  pl.BlockSpec((MOE_TILE, LANES), lambda i, *_: (i, 0))],
        out_specs=pl.BlockSpec((MOE_TILE, d), lambda i, *_: (i, 0)),
        scratch_shapes=[pltpu.VMEM((2, MOE_GBUF_ROWS, d), F32), pltpu.SemaphoreType.DMA((2,))],
    )
    return pl.pallas_call(
        _combine_kernel,
        grid_spec=gs,
        out_shape=jax.ShapeDtypeStruct((t, d), F32),
        compiler_params=_cparams(("arbitrary",)),
        name="moe_combine",
    )(gstart, n8, ys, rc)


def _moe(h2, logits, w1, b1, w2, b2, layer):
    t, d = h2.shape
    nt = t // MOE_TILE
    rc, rt, n8f = _route(logits)
    n8 = n8f[:, 0, :N_EXPERTS].astype(jnp.int32)
    e_rows = jnp.sum(n8, 0)
    e_pad = (e_rows + MOE_ROWS - 1) // MOE_ROWS * MOE_ROWS
    e_end = jnp.cumsum(e_pad)
    e_start = e_end - e_pad
    gstart = (e_start[None, :] + jnp.cumsum(n8, 0) - n8).reshape(-1)
    n_rows = -(-(t * TOP_K + nt * N_EXPERTS * 7 + N_EXPERTS * (MOE_ROWS - 1)) // MOE_ROWS) * MOE_ROWS
    n_used = (e_end[-1:] // MOE_ROWS).astype(jnp.int32)
    blk = jnp.minimum(jnp.arange(n_rows // MOE_ROWS, dtype=jnp.int32), n_used - 1) * MOE_ROWS
    blk_e = jnp.minimum(jnp.sum(blk[:, None] >= e_end[None, :], axis=1), N_EXPERTS - 1).astype(jnp.int32)
    n8_flat = n8.reshape(-1)
    gap_off = jnp.concatenate([e_start + e_rows, e_end[-1:]]).astype(jnp.int32)
    gap_n = jnp.concatenate([e_pad - e_rows, (n_rows - e_end[-1:]) // (MOE_ROWS // 2)]).astype(jnp.int32)
    xs = _dispatch(gstart.astype(jnp.int32), n8_flat, gap_off, gap_n, h2, rt, n_rows)
    ys = _ffn(blk_e, n_used, xs, w1, b1, w2, b2, layer)
    return _combine(gstart, n8_flat, ys, rc)


def _rope_tables(s_lat, lc):
    t = jnp.arange(s_lat)
    row = (t // GRID_W).astype(F32)
    col = (t % GRID_W).astype(F32)
    n_freq = DIFF_D // 4
    inv = ROPE_BASE ** (-jnp.arange(n_freq, dtype=F32) / n_freq)
    ang = jnp.concatenate([row[:, None] * inv, col[:, None] * inv], -1)
    ang = jnp.concatenate([ang, ang, ang, ang], -1)
    sign = jnp.where((jnp.arange(LANES) % DIFF_D) < DIFF_D // 2, -1.0, 1.0)
    cos = jnp.concatenate([jnp.cos(ang), jnp.ones((lc, LANES), F32)], 0)
    sin = jnp.concatenate([jnp.sin(ang) * sign, jnp.zeros((lc, LANES), F32)], 0)
    return cos.astype(F32), sin.astype(F32)


def kernel(x, c, ctx, c_ctx, w_ada, b_ada, w_in, conv_w, gdn_a_log, gdn_dt_bias, gdn_norm_w, diff_lambda, diff_norm_w, na_rpb, w_branch, w_out, ln_g, ln_b, w_router, b_router, w_exp1, b_exp1, w_exp2, b_exp2):
    bsz, s_lat, d = x.shape
    lc = ctx.shape[1]
    depth = w_ada.shape[0]
    assert lc == TOK_TILE and s_lat % (NA_QROWS * GRID_W) == 0 and d == 1024
    lt = s_lat + lc
    nl = s_lat // TOK_TILE
    rows = s_lat // GRID_W

    xs = jnp.concatenate([x, ctx], axis=1)
    cmat = jnp.zeros((8, d), F32).at[:bsz].set(c).at[bsz].set(c_ctx)
    mod = _ada(cmat, w_ada, b_ada)
    cos, sin = _rope_tables(s_lat, lc)

    cols = np.concatenate([np.arange(0, 2048), np.arange(2064, 2064 + 3072 + 3072), np.arange(2048, 2064)])
    w_in_r = jnp.pad(w_in[:, :, cols], ((0, 0), (0, 0), (0, P_COLS - len(cols)))).astype(BF16)
    wb16 = w_branch.astype(BF16)
    wo16 = w_out.astype(BF16)
    wr_pad = jnp.pad(w_router, ((0, 0), (0, 0), (0, LANES - N_EXPERTS)))
    br_pad = jnp.pad(b_router, ((0, 0), (0, LANES - N_EXPERTS)), constant_values=NEG_INF).reshape(depth, 1, LANES)
    lane_pad = LANES - 4 * GDN_H
    na_bias = _na_bias_tables(na_rpb, rows)

    modsels = [jnp.stack([mod[l, :bsz], jnp.broadcast_to(mod[l, bsz], (bsz, 6 * d))], axis=1)[:, :, None, :]
               for l in range(depth)]
    (h1,) = _norm(xs, nl, shift=(modsels[0], 0))
    for l in range(depth):
        lam_init = 0.8 - 0.6 * math.exp(-0.3 * l)
        modsel = modsels[l]
        tm = 512 if (bsz * lt) % 512 == 0 else TOK_TILE
        p = _matmul(h1.reshape(bsz * lt, d), w_in_r, l, tm, P_COLS // 5, F32).reshape(bsz, lt, P_COLS)

        def head_row(v2):
            z4 = jnp.zeros((GDN_H,), F32)
            return jnp.pad(jnp.concatenate([v2[0], z4, v2[1], z4]), (0, lane_pad)).reshape(1, LANES)
        qkv_n, gcs, gct = _gdn_prep(p, conv_w[l], head_row(gdn_a_log[l]), head_row(gdn_dt_bias[l]), nl)
        o_f, o_b = _gdn_scan(*_gdn_intra(qkv_n, gcs, gct), s_lat // GDN_CHUNK)

        a = _attn_prep(p, cos, sin)
        lv = diff_lambda[l].astype(F32)
        lam = jnp.exp(jnp.sum(lv[0] * lv[1])) - jnp.exp(jnp.sum(lv[2] * lv[3])) + lam_init
        lam_row = jnp.full((1, LANES), lam, F32)
        tq = min(512, s_lat)
        tkv = lt // 3 if (lt // 3) % TOK_TILE == 0 else TOK_TILE
        yb = _diff_attn(a, lam_row, diff_norm_w[l], lam_init, tq=tq, tkv=tkv, q0=0, nq=s_lat // tq,
                        k0=0, nkv=lt // tkv)
        yb_ctx = _diff_attn(a, lam_row, diff_norm_w[l], lam_init, tq=lc, tkv=lc, q0=s_lat // lc, nq=1,
                            k0=s_lat // lc, nkv=1)
        yc = _na_attn(a, na_bias, l, s_lat, lc)
        yc_ctx = _na_ctx_attn(a, s_lat, lc)

        xs, h2, logits = _merge(o_f, o_b, p, yb, yb_ctx, yc, yc_ctx, xs, gdn_norm_w[l], wb16[l], wo16[l],
                                modsel, nl, ln_g[l, 0], ln_b[l, 0], wr_pad[l], br_pad[l])
        f = _moe(h2.reshape(bsz * lt, d), logits.reshape(bsz * lt, LANES),
                 w_exp1, b_exp1, w_exp2, b_exp2, l).reshape(bsz, lt, d)
        if l + 1 < depth:
            xs, h1 = _norm(xs, nl, y=f, gate=(modsel, 5), ln_g=ln_g[l, 1], ln_b=ln_b[l, 1],
                           shift=(modsels[l + 1], 0))
        else:
            (xs,) = _norm(xs, nl, y=f, gate=(modsel, 5), ln_g=ln_g[l, 1], ln_b=ln_b[l, 1])
    return xs[:, :s_lat]
```

```python
import functools
import math

import numpy as np
import jax
import jax.numpy as jnp
from jax import lax
from jax.experimental import pallas as pl
from jax.experimental.pallas import tpu as pltpu

F32 = jnp.float32
BF16 = jnp.bfloat16

GRID_W = 64
GDN_H = 4
GDN_DK = 128
GDN_CHUNK = 64
DIFF_H = 4
DIFF_D = 64
ROPE_BASE = 10000.0
NA_H = 8
NA_D = 64
WIN_H = 8
WIN_W = 16
N_EXPERTS = 32
TOP_K = 4
D_EXPERT = 1024
SWIGLU_LIMIT = 7.0
SWIGLU_ALPHA = 1.702
DN_ALPHA = 8.0 ** 0.25
LN_EPS = 1e-5
RMS_EPS = 1e-6
NEG_INF = -1e30
LOG2E = math.log2(math.e)

LANES = 128
TOK_TILE = 256
NA_QROWS = 8
NA_KROWS = 16
MOE_ROWS = 512
INV_BASE_LOG2 = 3
GDN_INTRA_TILE = 128
GDN_SCAN_CHUNKS = 2
FFN_COL_CHUNK = 1024
DIFF_COL_TILE = 256
MOE_TILE = 256
MOE_GBUF_ROWS = MOE_TILE * TOP_K + N_EXPERTS * 8
MOE_RUN_PIECES = tuple(MOE_TILE >> s for s in range(int(math.log2(MOE_TILE)) - 2))
MOE_GAP_PIECES = tuple((MOE_ROWS // 2) >> s for s in range(int(math.log2(MOE_ROWS)) - 3))
VMEM_LIMIT = 56 * 1024 * 1024

C_GQ, C_GK, C_GV, C_GZ = 0, 512, 1024, 1536
C_DQ, C_DK, C_DV = 2048, 2560, 3072
C_NQ, C_NK, C_NV = 3584, 4096, 4608
C_GATE = 5120
C_AB = 8192


def _cparams(sem):
    return pltpu.CompilerParams(dimension_semantics=sem, vmem_limit_bytes=VMEM_LIMIT)


def _ln(x):
    mu = jnp.mean(x, -1, keepdims=True)
    xc = x - mu
    var = jnp.mean(xc * xc, -1, keepdims=True)
    return xc * lax.rsqrt(var + LN_EPS)


def _sigmoid(x):
    return 1.0 / (1.0 + jnp.exp(-x))


def _dot(a, b):
    return jnp.dot(a, b, preferred_element_type=F32)


def _dot_nt(a, b):
    return lax.dot_general(a, b, (((1,), (1,)), ((), ())), preferred_element_type=F32)


def _dot_tn(a, b):
    return lax.dot_general(a, b, (((0,), (0,)), ((), ())), preferred_element_type=F32)


def _mm(a, b):
    return jnp.dot(a.astype(BF16), b.astype(BF16), preferred_element_type=F32)


def _dot_3pass(a, b):
    a_hi = a.astype(BF16)
    b_hi = b.astype(BF16)
    a_lo = (a - a_hi.astype(F32)).astype(BF16)
    b_lo = (b - b_hi.astype(F32)).astype(BF16)
    return _dot(a_hi, b_hi) + _dot(a_hi, b_lo) + _dot(a_lo, b_hi)


def _dot_f32(a, b):
    return jnp.dot(a, b, preferred_element_type=F32, precision=lax.Precision.HIGHEST)


def _ada_kernel(c_ref, w_ref, b_ref, o_ref):
    c = c_ref[...]
    s = (c * _sigmoid(c)).astype(BF16)
    o_ref[0] = _dot(s, w_ref[0].astype(BF16)) + b_ref[0]


def _ada(cmat, w_ada, b_ada):
    depth, d, n = w_ada.shape
    tn = n // 4
    return pl.pallas_call(
        _ada_kernel,
        grid=(depth, n // tn),
        in_specs=[pl.BlockSpec((8, d), lambda l, j: (0, 0)),
                  pl.BlockSpec((1, d, tn), lambda l, j: (l, 0, j)),
                  pl.BlockSpec((1, 1, tn), lambda l, j: (l, 0, j))],
        out_specs=pl.BlockSpec((1, 8, tn), lambda l, j: (l, 0, j)),
        out_shape=jax.ShapeDtypeStruct((depth, 8, n), F32),
        compiler_params=_cparams(("parallel", "parallel")),
        name="ada",
    )(cmat, w_ada, b_ada.reshape(depth, 1, n))


def _norm_kernel(*refs, has_y, has_h):
    refs = list(refs)
    x = refs.pop(0)[...]
    if has_y:
        y_ref, g_ref, lg_ref, lb_ref = refs[:4]
        del refs[:4]
        x = _ln(DN_ALPHA * x + g_ref[...] * y_ref[...]) * lg_ref[...] + lb_ref[...]
    if has_h:
        sh_ref, sc_ref = refs[:2]
        del refs[:2]
    if has_y:
        refs.pop(0)[...] = x
    if has_h:
        refs.pop(0)[...] = (_ln(x) * (1.0 + sc_ref[...]) + sh_ref[...]).astype(BF16)


def _mod_spec(d, nl, k):
    return pl.BlockSpec((None, None, 1, d), lambda b, i: (b, i // nl, 0, k))


def _norm(x, nl, *, y=None, gate=None, ln_g=None, ln_b=None, shift=None):
    bsz, lt, d = x.shape
    has_y = y is not None
    has_h = shift is not None
    tok = pl.BlockSpec((None, TOK_TILE, d), lambda b, i: (b, i, 0))
    vec = pl.BlockSpec((1, d), lambda b, i: (0, 0))
    args, specs, outs, ospecs = [x], [tok], [], []
    if has_y:
        args += [y, gate[0], ln_g.reshape(1, d), ln_b.reshape(1, d)]
        specs += [tok, _mod_spec(d, nl, gate[1]), vec, vec]
        outs.append(jax.ShapeDtypeStruct((bsz, lt, d), F32))
        ospecs.append(tok)
    if has_h:
        args += [shift[0], shift[0]]
        specs += [_mod_spec(d, nl, shift[1]), _mod_spec(d, nl, shift[1] + 1)]
        outs.append(jax.ShapeDtypeStruct((bsz, lt, d), BF16))
        ospecs.append(tok)
    return pl.pallas_call(
        functools.partial(_norm_kernel, has_y=has_y, has_h=has_h),
        grid=(bsz, lt // TOK_TILE),
        in_specs=specs, out_specs=ospecs, out_shape=outs,
        compiler_params=_cparams(("parallel", "parallel")),
        name="norm",
    )(*args)


def _mm_kernel(a_ref, w_ref, o_ref):
    o_ref[...] = _dot(a_ref[...], w_ref[...]).astype(o_ref.dtype)


def _matmul(a, w, layer, tm, tn, out_dtype):
    m, k = a.shape
    n = w.shape[2]
    return pl.pallas_call(
        _mm_kernel,
        grid=(n // tn, m // tm),
        in_specs=[pl.BlockSpec((tm, k), lambda j, i: (i, 0)),
                  pl.BlockSpec((None, k, tn), lambda j, i: (layer, 0, j))],
        out_specs=pl.BlockSpec((tm, tn), lambda j, i: (i, j)),
        out_shape=jax.ShapeDtypeStruct((m, n), out_dtype),
        compiler_params=_cparams(("parallel", "parallel")),
        name="matmul",
    )(a, w)


def _gdn_prep_kernel(x_ref, hp_ref, hn_ref, ab_ref, cw_ref, alog_ref, dtb_ref,
                     qkv_ref, gc_ref, gct_ref, *, nl):
    i = pl.program_id(1)
    x = x_ref[...].astype(F32)
    t = x.shape[0]
    prev_ok = jnp.logical_and(i != 0, i != nl)
    next_ok = jnp.logical_and(i != nl - 1, i != nl)
    halo = hp_ref.shape[0]
    prow = jnp.where(prev_ok, hp_ref[halo - 1:halo, :].astype(F32), 0.0)
    nrow = jnp.where(next_ok, hn_ref[0:1, :].astype(F32), 0.0)
    rid = lax.broadcasted_iota(jnp.int32, x.shape, 0)
    xp = jnp.where(rid == 0, prow, pltpu.roll(x, 1, 0))
    xn = jnp.where(rid == t - 1, nrow, pltpu.roll(x, t - 1, 0))
    w = cw_ref[...]
    y = xp * w[0:1] + x * w[1:2] + xn * w[2:3]
    y = y * _sigmoid(y)
    for g in range(12):
        blk = y[:, g * LANES:(g + 1) * LANES]
        if g < 8:
            blk = blk * lax.rsqrt(jnp.sum(blk * blk, -1, keepdims=True) + RMS_EPS)
        if g < 4:
            blk = blk * (GDN_DK ** -0.5)
        qkv_ref[:, g * LANES:(g + 1) * LANES] = blk

    ab = ab_ref[...]
    lane = lax.broadcasted_iota(jnp.int32, ab.shape, 1)
    xs = ab + dtb_ref[...]
    softplus = jnp.maximum(xs, 0.0) + jnp.log(1.0 + jnp.exp(-jnp.abs(xs)))
    g = -jnp.exp(alog_ref[...]) * softplus
    beta = _sigmoid(ab)
    is_a = (lane % 8) < 4
    gb = jnp.where(lane < 16, jnp.where(is_a, g, beta), 0.0)
    r = lax.broadcasted_iota(jnp.int32, (t, t), 0)
    c = lax.broadcasted_iota(jnp.int32, (t, t), 1)
    same = (r // GDN_CHUNK) == (c // GDN_CHUNK)
    tri_l = jnp.where(jnp.logical_and(same, c <= r), 1.0, 0.0)
    tri_u = jnp.where(jnp.logical_and(same, c >= r), 1.0, 0.0)
    cf = _dot_f32(tri_l, gb)
    cb = _dot_f32(tri_u, gb)
    gc = jnp.where(lane < 4, cf, jnp.where(jnp.logical_and(lane >= 8, lane < 12), cb, gb))
    gc_ref[...] = gc
    gct = gc.T
    for ch in range(t // GDN_CHUNK):
        gct_ref[ch] = gct[0:16, ch * GDN_CHUNK:(ch + 1) * GDN_CHUNK]


def _gdn_prep(p, pab, conv_w, alog_row, dtb_row, nl):
    bsz, lt, _ = p.shape
    nt = lt // TOK_TILE
    w3 = 3 * 512
    halo = 16
    rb = TOK_TILE // halo
    return pl.pallas_call(
        functools.partial(_gdn_prep_kernel, nl=nl),
        grid=(bsz, nt),
        in_specs=[pl.BlockSpec((None, TOK_TILE, w3), lambda b, i: (b, i, 0)),
                  pl.BlockSpec((None, halo, w3), lambda b, i: (b, jnp.maximum(i * rb - 1, 0), 0)),
                  pl.BlockSpec((None, halo, w3), lambda b, i: (b, jnp.minimum(i * rb + rb, lt // halo - 1), 0)),
                  pl.BlockSpec((None, TOK_TILE, LANES), lambda b, i: (b, i, 0)),
                  pl.BlockSpec((3, w3), lambda b, i: (0, 0)),
                  pl.BlockSpec((1, LANES), lambda b, i: (0, 0)),
                  pl.BlockSpec((1, LANES), lambda b, i: (0, 0))],
        out_specs=[pl.BlockSpec((None, TOK_TILE, w3), lambda b, i: (b, i, 0)),
                   pl.BlockSpec((None, TOK_TILE, LANES), lambda b, i: (b, i, 0)),
                   pl.BlockSpec((None, TOK_TILE // GDN_CHUNK, 16, GDN_CHUNK), lambda b, i: (b, i, 0, 0))],
        out_shape=[jax.ShapeDtypeStruct((bsz, lt, w3), F32),
                   jax.ShapeDtypeStruct((bsz, lt, LANES), F32),
                   jax.ShapeDtypeStruct((bsz, lt // GDN_CHUNK, 16, GDN_CHUNK), F32)],
        compiler_params=_cparams(("parallel", "parallel")),
        name="gdn_prep",
    )(p, p, p, pab, conv_w, alog_row, dtb_row)


def _gdn_intra_kernel(qkv_ref, gc_ref, gct_ref, u_ref, w_ref, qg_ref, kt_ref, qk_ref, eg_ref):
    cs = GDN_CHUNK
    nch = qkv_ref.shape[0] // cs
    chains = [(d, c, h) for d in range(2) for c in range(nch) for h in range(GDN_H)]
    ri = lax.broadcasted_iota(jnp.int32, (cs, cs), 0)
    ci = lax.broadcasted_iota(jnp.int32, (cs, cs), 1)
    eye = ri == ci
    base = (ri >> INV_BASE_LOG2) == (ci >> INV_BASE_LOG2)
    incl = (ri >= ci, ri <= ci)
    strict = (ri > ci, ri < ci)
    qkv = qkv_ref[...]
    gc = gc_ref[...]

    def part(col0, c, h):
        return qkv[c * cs:(c + 1) * cs, col0 + h * LANES:col0 + (h + 1) * LANES]

    ch_keys = [(c, h) for c in range(nch) for h in range(GDN_H)]
    q = {key: part(C_GQ, *key) for key in ch_keys}
    k = {key: part(C_GK, *key) for key in ch_keys}
    v = {key: part(C_GV, *key) for key in ch_keys}
    q16 = {key: q[key].astype(BF16) for key in ch_keys}
    k16 = {key: k[key].astype(BF16) for key in ch_keys}
    gcol = [gc[c * cs:(c + 1) * cs, 8 * d + h:8 * d + h + 1] for d, c, h in chains]
    bcol = [gc[c * cs:(c + 1) * cs, 8 * d + 4 + h:8 * d + 5 + h] for d, c, h in chains]
    grow = [gct_ref[c, 8 * d + h:8 * d + h + 1, :] for d, c, h in chains]
    glast = [g[cs - 1:cs, :] if d == 0 else g[0:1, :] for g, (d, c, h) in zip(gcol, chains)]
    decay = [jnp.where(incl[d], jnp.exp(jnp.where(incl[d], gcol[n] - grow[n], 0.0)), 0.0)
             for n, (d, c, h) in enumerate(chains)]
    kb = [k[(c, h)] * bcol[n] for n, (d, c, h) in enumerate(chains)]
    amat = [jnp.where(strict[d], _dot_nt(kb[n].astype(BF16), k16[(c, h)]) * decay[n], 0.0)
            for n, (d, c, h) in enumerate(chains)]
    qk = [_dot_nt(q16[(c, h)], k16[(c, h)]) * decay[n] for n, (d, c, h) in enumerate(chains)]
    eg = [jnp.exp(g) for g in gcol]
    rhs = [jnp.concatenate([v[(c, h)] * bcol[n], kb[n] * eg[n]], axis=1) for n, (d, c, h) in enumerate(chains)]
    rmat = [jnp.where(base, -a, 0.0) for a in amat]
    mmat = rmat
    for _ in range(INV_BASE_LOG2 - 1):
        mmat = [_mm(m, m) for m in mmat]
        prod = [_mm(r, m) for r, m in zip(rmat, mmat)]
        rmat = [r + m + p for r, m, p in zip(rmat, mmat, prod)]
    tmat = [jnp.where(eye, 1.0, r) for r in rmat]
    for lb in range(INV_BASE_LOG2, int(math.log2(cs))):
        same_pair = (ri >> (lb + 1)) == (ci >> (lb + 1))
        off = (jnp.logical_and((ri >> lb) == (ci >> lb) + 1, same_pair),
               jnp.logical_and((ci >> lb) == (ri >> lb) + 1, same_pair))
        inner = [_mm(jnp.where(off[d], amat[n], 0.0), tmat[n]) for n, (d, c, h) in enumerate(chains)]
        outer = [_mm(t, x) for t, x in zip(tmat, inner)]
        tmat = [t - x for t, x in zip(tmat, outer)]
    corr = [_mm(jnp.where(eye, 0.0, t), r) for t, r in zip(tmat, rhs)]
    for n, (d, c, h) in enumerate(chains):
        rows = slice(c * cs, (c + 1) * cs)
        cols = slice(h * LANES, (h + 1) * LANES)
        sol = rhs[n] + corr[n]
        u_ref[d, rows, cols] = sol[:, :LANES]
        w_ref[d, rows, cols] = sol[:, LANES:].astype(BF16)
        qg_ref[d, rows, cols] = (q[(c, h)] * eg[n]).astype(BF16)
        kt_ref[d, rows, cols] = (k[(c, h)] * jnp.exp(glast[n] - gcol[n])).astype(BF16)
        qk_ref[d, c, h] = qk[n].astype(BF16)
        eg_ref[d, c, h:h + 1, :] = jnp.broadcast_to(jnp.exp(glast[n]), (1, LANES))


def _gdn_intra(qkv, gc, gct):
    bsz, lt, w3 = qkv.shape
    cs = GDN_CHUNK
    nc = lt // cs
    nch = GDN_INTRA_TILE // cs

    def tok(dt):
        return (pl.BlockSpec((None, 2, GDN_INTRA_TILE, 512), lambda b, i: (b, 0, i, 0)),
                jax.ShapeDtypeStruct((bsz, 2, lt, 512), dt))

    outs = [tok(F32), tok(BF16), tok(BF16), tok(BF16),
            (pl.BlockSpec((None, 2, nch, GDN_H, cs, cs), lambda b, i: (b, 0, i, 0, 0, 0)),
             jax.ShapeDtypeStruct((bsz, 2, nc, GDN_H, cs, cs), BF16)),
            (pl.BlockSpec((None, 2, nch, GDN_H, LANES), lambda b, i: (b, 0, i, 0, 0)),
             jax.ShapeDtypeStruct((bsz, 2, nc, GDN_H, LANES), F32))]
    return pl.pallas_call(
        _gdn_intra_kernel,
        grid=(bsz, lt // GDN_INTRA_TILE),
        in_specs=[pl.BlockSpec((None, GDN_INTRA_TILE, w3), lambda b, i: (b, i, 0)),
                  pl.BlockSpec((None, GDN_INTRA_TILE, LANES), lambda b, i: (b, i, 0)),
                  pl.BlockSpec((None, nch, 16, cs), lambda b, i: (b, i, 0, 0))],
        out_specs=[o[0] for o in outs],
        out_shape=[o[1] for o in outs],
        compiler_params=_cparams(("parallel", "parallel")),
        name="gdn_intra",
    )(qkv, gc, gct)


def _gdn_scan_kernel(*refs):
    s_ref = refs[-1]
    cs = GDN_CHUNK

    @pl.when(pl.program_id(1) == 0)
    def _():
        s_ref[...] = jnp.zeros_like(s_ref)

    chains = [(d, h) for d in range(2) for h in range(GDN_H)]
    src = (refs[0:6], refs[6:12])
    outs = refs[12:14]
    per_step = outs[0].shape[0] // cs

    def cols(h):
        return slice(h * LANES, (h + 1) * LANES)

    s = [s_ref[d * GDN_H + h] for d, h in chains]
    for sub in range(per_step):
        cidx = (sub, per_step - 1 - sub)
        rows = [slice(c * cs, (c + 1) * cs) for c in cidx]
        wq = [_dot(jnp.concatenate([src[d][1][rows[d], cols(h)], src[d][2][rows[d], cols(h)]], axis=0),
                   s[n].astype(BF16)) for n, (d, h) in enumerate(chains)]
        v16 = [(src[d][0][rows[d], cols(h)] - wq[n][:cs]).astype(BF16) for n, (d, h) in enumerate(chains)]
        o = [wq[n][cs:] + _dot(src[d][4][cidx[d], h], v16[n]) for n, (d, h) in enumerate(chains)]
        s = [s[n] * src[d][5][cidx[d], h:h + 1, :] + _dot_tn(src[d][3][rows[d], cols(h)], v16[n])
             for n, (d, h) in enumerate(chains)]
        for n, (d, h) in enumerate(chains):
            outs[d][rows[d], cols(h)] = o[n]
    for n, (d, h) in enumerate(chains):
        s_ref[d * GDN_H + h] = s[n]


def _gdn_scan(u, w, qg, kt, qk, eg, n_lat_chunks):
    bsz, _, lt, _ = u.shape
    per = GDN_SCAN_CHUNKS
    cs = GDN_CHUNK * per
    nc = lt // cs
    n_lat = n_lat_chunks // per
    ncc = nc - n_lat

    def cf(i):
        return jnp.where(i < ncc, n_lat + i, i - ncc)

    def cb(i):
        return nc - 1 - i

    def specs(d, c):
        tok = pl.BlockSpec((None, None, cs, 512), lambda b, i: (b, d, c(i), 0))
        return [tok, tok, tok, tok,
                pl.BlockSpec((None, None, per, GDN_H, GDN_CHUNK, GDN_CHUNK), lambda b, i: (b, d, c(i), 0, 0, 0)),
                pl.BlockSpec((None, None, per, GDN_H, LANES), lambda b, i: (b, d, c(i), 0, 0))]

    def out(c):
        return pl.BlockSpec((None, cs, 512), lambda b, i: (b, c(i), 0))

    args = (u, w, qg, kt, qk, eg)
    return pl.pallas_call(
        _gdn_scan_kernel,
        grid=(bsz, nc),
        in_specs=specs(0, cf) + specs(1, cb),
        out_specs=[out(cf), out(cb)],
        out_shape=[jax.ShapeDtypeStruct((bsz, lt, 512), F32)] * 2,
        scratch_shapes=[pltpu.VMEM((2 * GDN_H, GDN_DK, LANES), F32)],
        compiler_params=_cparams(("parallel", "arbitrary")),
        name="gdn_scan",
    )(*args, *args)


def _attn_prep_kernel(dq_ref, dk_ref, dv_ref, nq_ref, nk_ref, nv_ref, cos_ref, sin_ref, o_ref):
    cos = cos_ref[...]
    sin = sin_ref[...]
    lane = lax.broadcasted_iota(jnp.int32, cos.shape, 1)
    first = (lane % DIFF_D) < DIFF_D // 2

    def rope(x):
        rot = jnp.where(first, pltpu.roll(x, LANES - DIFF_D // 2, 1), pltpu.roll(x, DIFF_D // 2, 1))
        return x * cos + rot * sin

    for h in range(DIFF_H):
        sl = slice(h * LANES, (h + 1) * LANES)
        o_ref[:, h * LANES:(h + 1) * LANES] = (rope(dq_ref[:, sl].astype(F32))
                                               * (DIFF_D ** -0.5 * LOG2E)).astype(BF16)
        o_ref[:, 512 + h * LANES:512 + (h + 1) * LANES] = rope(dk_ref[:, sl].astype(F32)).astype(BF16)
    o_ref[:, 1024:1536] = dv_ref[...].astype(BF16)
    o_ref[:, 1536:2048] = (nq_ref[...].astype(F32) * (NA_D ** -0.5)).astype(BF16)
    o_ref[:, 2048:2560] = nk_ref[...].astype(BF16)
    o_ref[:, 2560:3072] = nv_ref[...].astype(BF16)


def _attn_prep(p, cos, sin):
    bsz, lt, _ = p.shape

    def col(cb):
        return pl.BlockSpec((None, TOK_TILE, 512), lambda b, i: (b, i, cb))

    tab = pl.BlockSpec((TOK_TILE, LANES), lambda b, i: (i, 0))
    return pl.pallas_call(
        _attn_prep_kernel,
        grid=(bsz, lt // TOK_TILE),
        in_specs=[col(C_DQ // 512), col(C_DK // 512), col(C_DV // 512),
                  col(C_NQ // 512), col(C_NK // 512), col(C_NV // 512), tab, tab],
        out_specs=pl.BlockSpec((None, TOK_TILE, 3072), lambda b, i: (b, i, 0)),
        out_shape=jax.ShapeDtypeStruct((bsz, lt, 3072), BF16),
        compiler_params=_cparams(("parallel", "parallel")),
        name="attn_prep",
    )(p, p, p, p, p, p, cos, sin)


def _diff_kernel(lam_ref, q_ref, k_ref, v_ref, nw_ref, o_ref, m_sc, l_sc, acc_sc, *, nkv, lam_init):
    kv = pl.program_id(3)

    @pl.when(kv == 0)
    def _():
        m_sc[...] = jnp.full_like(m_sc, NEG_INF)
        l_sc[...] = jnp.zeros_like(l_sc)
        acc_sc[...] = jnp.zeros_like(acc_sc)

    q = q_ref[...]
    k = k_ref[...]
    v = v_ref[...]
    maps = range(2)
    ct = min(DIFF_COL_TILE, k.shape[0])
    tiles = range(k.shape[0] // ct)

    def fold(acc, x, op):
        for c0 in range(0, x.shape[1], LANES):
            piece = x[:, c0:c0 + LANES]
            acc = piece if acc is None else op(acc, piece)
        return acc

    s = [[_dot_nt(q[:, m * DIFF_D:(m + 1) * DIFF_D], k[j * ct:(j + 1) * ct, m * DIFF_D:(m + 1) * DIFF_D])
          for j in tiles] for m in maps]
    m_new, alpha = [], []
    for m in maps:
        mx = None
        for j in tiles:
            mx = fold(mx, s[m][j], jnp.maximum)
        m_prev = m_sc[m]
        m_new.append(jnp.maximum(m_prev, jnp.max(mx, -1, keepdims=True)))
        alpha.append(jnp.exp2(m_prev - m_new[m]))
    for m in maps:
        sm, pv = None, None
        for j in tiles:
            p = jnp.exp2(s[m][j] - m_new[m])
            sm = fold(sm, p, jnp.add)
            part = _dot(p.astype(BF16), v[j * ct:(j + 1) * ct, :])
            pv = part if pv is None else pv + part
        l_sc[m] = alpha[m] * l_sc[m] + jnp.sum(sm, -1, keepdims=True)
        acc_sc[m] = alpha[m] * acc_sc[m] + pv
        m_sc[m] = m_new[m]

    @pl.when(kv == nkv - 1)
    def _():
        lam = lam_ref[0:1, 0:1]
        o = acc_sc[0] / l_sc[0] - lam * (acc_sc[1] / l_sc[1])
        y = o * lax.rsqrt(jnp.mean(o * o, -1, keepdims=True) + RMS_EPS) * nw_ref[...] * (1.0 - lam_init)
        o_ref[...] = y.astype(BF16)


def _diff_attn(a, lam_row, norm_w, lam_init, *, tq, tkv, q0, nq, k0, nkv):
    bsz = a.shape[0]
    return pl.pallas_call(
        functools.partial(_diff_kernel, nkv=nkv, lam_init=lam_init),
        grid=(bsz, DIFF_H, nq, nkv),
        in_specs=[pl.BlockSpec((1, LANES), lambda b, h, i, j: (0, 0)),
                  pl.BlockSpec((None, tq, LANES), lambda b, h, i, j: (b, q0 + i, h)),
                  pl.BlockSpec((None, tkv, LANES), lambda b, h, i, j: (b, k0 + j, 4 + h)),
                  pl.BlockSpec((None, tkv, LANES), lambda b, h, i, j: (b, k0 + j, 8 + h)),
                  pl.BlockSpec((1, LANES), lambda b, h, i, j: (0, 0))],
        out_specs=pl.BlockSpec((None, tq, LANES), lambda b, h, i, j: (b, i, h)),
        out_shape=jax.ShapeDtypeStruct((bsz, nq * tq, 512), BF16),
        scratch_shapes=[pltpu.VMEM((2, tq, 1), F32), pltpu.VMEM((2, tq, 1), F32),
                        pltpu.VMEM((2, tq, LANES), F32)],
        compiler_params=_cparams(("parallel", "parallel", "parallel", "arbitrary")),
        name="diff_attn",
    )(lam_row, a, a, a, norm_w.reshape(1, LANES))


def _na_kernel(q_ref, k_ref, v_ref, bias_ref, o_ref, *, s_lat, lc):
    i = pl.program_id(2)
    tq = q_ref.shape[0]
    nk = NA_KROWS * GRID_W
    kstart = jnp.clip(i * tq - (NA_KROWS - NA_QROWS) // 2 * GRID_W, 0, s_lat - nk)
    kstart = pl.multiple_of(kstart, TOK_TILE)
    q = q_ref[...]
    k_loc = k_ref[pl.ds(kstart, nk), :]
    v_loc = v_ref[pl.ds(kstart, nk), :]
    k_ctx = k_ref[s_lat:s_lat + lc, :]
    v_ctx = v_ref[s_lat:s_lat + lc, :]
    outs = []
    for j in range(2):
        sl = slice(j * NA_D, (j + 1) * NA_D)
        qh = q[:, sl]
        s_loc = _dot_nt(qh, k_loc[:, sl]) + bias_ref[j]
        s_ctx = _dot_nt(qh, k_ctx[:, sl])
        m = jnp.maximum(jnp.max(s_loc, -1, keepdims=True), jnp.max(s_ctx, -1, keepdims=True))
        p_loc = jnp.exp(s_loc - m)
        p_ctx = jnp.exp(s_ctx - m)
        l = jnp.sum(p_loc, -1, keepdims=True) + jnp.sum(p_ctx, -1, keepdims=True)
        o = _dot(p_loc.astype(BF16), v_loc[:, sl]) + _dot(p_ctx.astype(BF16), v_ctx[:, sl])
        outs.append(o / l)
    o_ref[...] = jnp.concatenate(outs, axis=1).astype(BF16)


def _na_attn(a, bias, layer, s_lat, lc):
    bsz, lt, _ = a.shape
    tq = NA_QROWS * GRID_W
    nq = s_lat // tq

    def variant(i):
        return jnp.where(i == 0, 0, jnp.where(i == nq - 1, 2, 1))

    return pl.pallas_call(
        functools.partial(_na_kernel, s_lat=s_lat, lc=lc),
        grid=(bsz, NA_H // 2, nq),
        in_specs=[pl.BlockSpec((None, tq, LANES), lambda b, h, i: (b, i, 12 + h)),
                  pl.BlockSpec((None, lt, LANES), lambda b, h, i: (b, 0, 16 + h)),
                  pl.BlockSpec((None, lt, LANES), lambda b, h, i: (b, 0, 20 + h)),
                  pl.BlockSpec((None, None, 2, tq, NA_KROWS * GRID_W),
                               lambda b, h, i: (layer, variant(i), h, 0, 0))],
        out_specs=pl.BlockSpec((None, tq, LANES), lambda b, h, i: (b, i, h)),
        out_shape=jax.ShapeDtypeStruct((bsz, s_lat, 512), BF16),
        compiler_params=_cparams(("parallel", "parallel", "arbitrary")),
        name="na_attn",
    )(a, a, a, bias)


def _na_ctx_kernel(q_ref, k_ref, v_ref, o_ref):
    q = q_ref[...]
    k = k_ref[...]
    v = v_ref[...]
    outs = []
    for j in range(2):
        sl = slice(j * NA_D, (j + 1) * NA_D)
        s = _dot_nt(q[:, sl], k[:, sl])
        p = jnp.exp(s - jnp.max(s, -1, keepdims=True))
        outs.append(_dot(p.astype(BF16), v[:, sl]) / jnp.sum(p, -1, keepdims=True))
    o_ref[...] = jnp.concatenate(outs, axis=1).astype(BF16)


def _na_ctx_attn(a, s_lat, lc):
    bsz = a.shape[0]
    rb = s_lat // lc

    def blk(c0):
        return pl.BlockSpec((None, lc, LANES), lambda b, h: (b, rb, c0 + h))

    return pl.pallas_call(
        _na_ctx_kernel,
        grid=(bsz, NA_H // 2),
        in_specs=[blk(12), blk(16), blk(20)],
        out_specs=pl.BlockSpec((None, lc, LANES), lambda b, h: (b, 0, h)),
        out_shape=jax.ShapeDtypeStruct((bsz, lc, 512), BF16),
        compiler_params=_cparams(("parallel", "parallel")),
        name="na_ctx_attn",
    )(a, a, a)


def _na_bias_tables(rpb, rows):
    qr = np.arange(NA_QROWS)[:, None]
    kr = np.arange(NA_KROWS)[None, :]
    qc = np.arange(GRID_W)[:, None]
    kc = np.arange(GRID_W)[None, :]
    w0 = np.clip(qc - WIN_W // 2, 0, GRID_W - WIN_W)
    okc = (kc >= w0) & (kc < w0 + WIN_W)
    dc = np.clip(kc - qc + WIN_W - 1, 0, 2 * WIN_W - 2)
    half = (NA_KROWS - NA_QROWS) // 2
    n_dr, n_dc = 2 * WIN_H - 1, 2 * WIN_W - 1
    sel_c = (dc[None] == np.arange(n_dc)[:, None, None]) & okc[None]
    sel_r = np.zeros((3, NA_QROWS, NA_KROWS, n_dr), bool)
    for vi, (r_start, delta) in enumerate(((0, 0), (NA_QROWS, -half), (rows - NA_QROWS, -2 * half))):
        r = r_start + qr
        kabs = r_start + delta + kr
        r0 = np.clip(r - WIN_H // 2, 0, rows - WIN_H)
        okr = (kabs >= r0) & (kabs < r0 + WIN_H)
        dr = np.clip(kabs - r + WIN_H - 1, 0, n_dr - 1)
        sel_r[vi] = (dr[..., None] == np.arange(n_dr)) & okr[..., None]
    toep = jnp.einsum('lhaj,jqk->lhaqk', rpb, sel_c.astype(np.float32), precision=lax.Precision.HIGHEST)
    tab = jnp.einsum('vrsa,lhaqk->lvhrqsk', sel_r.astype(np.float32), toep, precision=lax.Precision.HIGHEST)
    ok = sel_r.any(-1)[None, :, None, :, None, :, None] & okc[None, None, None, None, :, None, :]
    tab = jnp.where(ok, tab, NEG_INF)
    return tab.reshape(rpb.shape[0], 3, NA_H, NA_QROWS * GRID_W, NA_KROWS * GRID_W).astype(F32)


def _merge_kernel(of_ref, ob_ref, z_ref, ybl_ref, ybc_ref, ycl_ref, ycc_ref, ga_ref, gb_ref, gc_ref, x_ref,
                  gnw_ref, wb_ref, wo_ref, g1_ref, lg_ref, lb_ref, sh_ref, sc_ref, wr_ref, br_ref,
                  xo_ref, h_ref, lg_out_ref, *, nl):
    is_ctx = pl.program_id(1) >= nl
    yb = jnp.where(is_ctx, ybc_ref[...], ybl_ref[...])
    yc = jnp.where(is_ctx, ycc_ref[...], ycl_ref[...])
    o = of_ref[...] + ob_ref[...]
    z = z_ref[...].astype(F32)
    gnw = gnw_ref[...]
    parts = []
    for h in range(GDN_H):
        sl = slice(h * LANES, (h + 1) * LANES)
        oh = o[:, sl]
        zh = z[:, sl]
        parts.append(oh * lax.rsqrt(jnp.mean(oh * oh, -1, keepdims=True) + RMS_EPS) * gnw * (zh * _sigmoid(zh)))
    ya = jnp.concatenate(parts, axis=1).astype(BF16)
    m = (_sigmoid(ga_ref[...].astype(F32)) * _dot(ya, wb_ref[0])
         + _sigmoid(gb_ref[...].astype(F32)) * _dot(yb, wb_ref[1])
         + _sigmoid(gc_ref[...].astype(F32)) * _dot(yc, wb_ref[2]))
    mx = _dot(m.astype(BF16), wo_ref[...])
    x = _ln(DN_ALPHA * x_ref[...] + g1_ref[...] * mx) * lg_ref[...] + lb_ref[...]
    xo_ref[...] = x
    hh = _ln(x) * (1.0 + sc_ref[...]) + sh_ref[...]
    h_ref[...] = hh.astype(BF16)
    lg_out_ref[...] = _dot_3pass(hh, wr_ref[...]) + br_ref[...]


def _merge(o_f, o_b, p, yb, yb_ctx, yc, yc_ctx, x, gnw, wb, wo, modsel, nl, ln_g, ln_b, wr, br):
    bsz, lt, d = x.shape

    def tok(width, cb):
        return pl.BlockSpec((None, TOK_TILE, width), lambda b, i: (b, i, cb))

    def const(shape):
        return pl.BlockSpec(shape, lambda b, i: (0,) * len(shape))

    lat = pl.BlockSpec((None, TOK_TILE, 512), lambda b, i: (b, jnp.minimum(i, nl - 1), 0))
    cxt = pl.BlockSpec((None, TOK_TILE, 512), lambda b, i: (b, 0, 0))
    return pl.pallas_call(
        functools.partial(_merge_kernel, nl=nl),
        grid=(bsz, lt // TOK_TILE),
        in_specs=[tok(512, 0), tok(512, 0), tok(512, C_GZ // 512), lat, cxt, lat, cxt,
                  tok(d, C_GATE // d), tok(d, C_GATE // d + 1), tok(d, C_GATE // d + 2), tok(d, 0),
                  const((1, LANES)), const((3, 512, d)), const((d, d)),
                  _mod_spec(d, nl, 2), const((1, d)), const((1, d)),
                  _mod_spec(d, nl, 3), _mod_spec(d, nl, 4),
                  const((d, LANES)), const((1, LANES))],
        out_specs=[tok(d, 0), tok(d, 0), tok(LANES, 0)],
        out_shape=[jax.ShapeDtypeStruct((bsz, lt, d), F32),
                   jax.ShapeDtypeStruct((bsz, lt, d), BF16),
                   jax.ShapeDtypeStruct((bsz, lt, LANES), F32)],
        compiler_params=_cparams(("parallel", "parallel")),
        name="merge",
    )(o_f, o_b, p, yb, yb_ctx, yc, yc_ctx, p, p, p, x, gnw.reshape(1, LANES), wb, wo,
      modsel, ln_g.reshape(1, d), ln_b.reshape(1, d), modsel, modsel, wr, br)


def _ffn_kernel(be_ref, nu_ref, x_ref, w1_ref, b1_ref, w2_ref, b2_ref, o_ref, w1c, w2c):
    i = pl.program_id(0)

    @pl.when(i < nu_ref[0])
    def _():
        changed = jnp.logical_or(i == 0, be_ref[i] != be_ref[jnp.maximum(i - 1, 0)])

        @pl.when(changed)
        def _():
            w1c[...] = w1_ref[...].astype(BF16)
            w2c[...] = w2_ref[...].astype(BF16)

        x = x_ref[...].astype(BF16)
        y = None
        for c0 in range(0, D_EXPERT, FFN_COL_CHUNK):
            c1 = c0 + FFN_COL_CHUNK
            hg = _dot(x, w1c[:, c0:c1]) + b1_ref[:, c0:c1]
            hl = _dot(x, w1c[:, D_EXPERT + c0:D_EXPERT + c1]) + b1_ref[:, D_EXPERT + c0:D_EXPERT + c1]
            x_glu = jnp.minimum(hg, SWIGLU_LIMIT)
            x_lin = jnp.clip(hl, -SWIGLU_LIMIT, SWIGLU_LIMIT)
            act = x_glu * _sigmoid(SWIGLU_ALPHA * x_glu) * (x_lin + 1.0)
            part = _dot(act.astype(BF16), w2c[c0:c1, :])
            y = part if y is None else y + part
        o_ref[...] = y + b2_ref[...]

    @pl.when(i >= nu_ref[0])
    def _():
        o_ref[...] = jnp.zeros_like(o_ref)


def _ffn(blk_e, n_used, xs, w1, b1, w2, b2, layer):
    n_rows, d = xs.shape
    n_blk = n_rows // MOE_ROWS
    depth, n_exp, _, de2 = w1.shape
    gs = pltpu.PrefetchScalarGridSpec(
        num_scalar_prefetch=2,
        grid=(n_blk,),
        in_specs=[pl.BlockSpec((MOE_ROWS, d), lambda i, be, nu: (i, 0)),
                  pl.BlockSpec((None, None, d, de2), lambda i, be, nu: (layer, be[i], 0, 0)),
                  pl.BlockSpec((None, None, 1, de2), lambda i, be, nu: (layer, be[i], 0, 0)),
                  pl.BlockSpec((None, None, de2 // 2, d), lambda i, be, nu: (layer, be[i], 0, 0)),
                  pl.BlockSpec((None, None, 1, d), lambda i, be, nu: (layer, be[i], 0, 0))],
        out_specs=pl.BlockSpec((MOE_ROWS, d), lambda i, be, nu: (i, 0)),
        scratch_shapes=[pltpu.VMEM((d, de2), BF16), pltpu.VMEM((de2 // 2, d), BF16)],
    )
    return pl.pallas_call(
        _ffn_kernel,
        grid_spec=gs,
        out_shape=jax.ShapeDtypeStruct((n_rows, d), F32),
        compiler_params=_cparams(("arbitrary",)),
        name="ffn",
    )(blk_e, n_used, xs, w1, b1.reshape(depth, n_exp, 1, de2), w2, b2.reshape(depth, n_exp, 1, d))


def _route_kernel(lg_ref, rc_ref, rt_ref, n8_ref):
    l = lg_ref[...]
    tt = l.shape[0]
    lane = lax.broadcasted_iota(jnp.int32, l.shape, 1)
    picks, vals = [], []
    for _ in range(TOP_K):
        m = jnp.max(l, -1, keepdims=True)
        idx = jnp.min(jnp.where(l == m, lane, LANES), -1, keepdims=True)
        oh = lane == idx
        picks.append(oh)
        vals.append(m)
        l = jnp.where(oh, -jnp.inf, l)
    sel = sum(jnp.where(oh, 1.0, 0.0) for oh in picks)
    cnt = jnp.sum(sel, 0, keepdims=True)
    n8 = jnp.floor((cnt + 7.0) * 0.125) * 8.0
    r = lax.broadcasted_iota(jnp.int32, (LANES, LANES), 0)
    c = lax.broadcasted_iota(jnp.int32, (LANES, LANES), 1)
    off8 = _dot_f32(jnp.broadcast_to(n8, (8, LANES)), jnp.where(r < c, 1.0, 0.0))[0:1]
    tr = lax.broadcasted_iota(jnp.int32, (tt, tt), 0)
    tc = lax.broadcasted_iota(jnp.int32, (tt, tt), 1)
    rank = _dot(jnp.where(tc < tr, 1.0, 0.0).astype(BF16), sel.astype(BF16))
    slot = off8 + rank
    es = [jnp.exp(v - vals[0]) for v in vals]
    den = sum(es)
    rc = jnp.zeros(l.shape, F32)
    for kk in range(TOP_K):
        loc = jnp.sum(jnp.where(picks[kk], slot, 0.0), -1, keepdims=True)
        rc = jnp.where(lane == kk, loc, rc)
        rc = jnp.where(lane == TOP_K + kk, es[kk] / den, rc)
    rc_ref[...] = rc
    rt_ref[...] = rc.T[0:8, :]
    n8_ref[...] = jnp.broadcast_to(n8, (8, LANES))


def _route(logits):
    t = logits.shape[0]
    nt = t // MOE_TILE
    return pl.pallas_call(
        _route_kernel,
        grid=(nt,),
        in_specs=[pl.BlockSpec((MOE_TILE, LANES), lambda i: (i, 0))],
        out_specs=[pl.BlockSpec((MOE_TILE, LANES), lambda i: (i, 0)),
                   pl.BlockSpec((None, 8, MOE_TILE), lambda i: (i, 0, 0)),
                   pl.BlockSpec((None, 8, LANES), lambda i: (i, 0, 0))],
        out_shape=[jax.ShapeDtypeStruct((t, LANES), F32),
                   jax.ShapeDtypeStruct((nt, 8, MOE_TILE), F32),
                   jax.ShapeDtypeStruct((nt, 8, LANES), F32)],
        compiler_params=_cparams(("parallel",)),
        name="moe_route",
    )(logits)


def _run_copies(n, pieces, make_copy, wait):
    done = jnp.int32(0)
    for size in pieces:
        take = (n & size) != 0

        @pl.when(take)
        def _():
            cp = make_copy(done, size)
            if wait:
                cp.wait()
            else:
                cp.start()

        done = done + jnp.where(take, size, 0)


def _tile_runs(gs_ref, n8_ref, tile, make_copy, wait):
    def body(e, local):
        n = n8_ref[tile * N_EXPERTS + e]
        g = gs_ref[tile * N_EXPERTS + e]
        _run_copies(n, MOE_RUN_PIECES,
                    lambda done, size: make_copy(pl.multiple_of(local + done, 8),
                                                 pl.multiple_of(g + done, 8), size), wait)
        return local + n
    lax.fori_loop(0, N_EXPERTS, body, jnp.int32(0))


def _dispatch_kernel(gs_ref, n8_ref, go_ref, gn_ref, h_ref, rt_ref, xs_ref, gbuf, zbuf, sems):
    i = pl.program_id(0)
    slot = i % 2
    sem = sems.at[0]

    @pl.when(i == 0)
    def _():
        zbuf[...] = jnp.zeros_like(zbuf)
        zrows = zbuf.shape[0]
        for wait in (False, True):
            def body(e, carry):
                _run_copies(gn_ref[e], MOE_GAP_PIECES,
                            lambda done, size: pltpu.make_async_copy(
                                zbuf.at[pl.ds(0, size), :],
                                xs_ref.at[pl.ds(pl.multiple_of(go_ref[e] + done, 8), size), :], sem), wait)
                return carry
            lax.fori_loop(0, N_EXPERTS, body, 0)

            def tail(j, carry):
                cp = pltpu.make_async_copy(
                    zbuf, xs_ref.at[pl.ds(pl.multiple_of(go_ref[N_EXPERTS] + j * zrows, 8), zrows), :], sem)
                if wait:
                    cp.wait()
                else:
                    cp.start()
                return carry
            lax.fori_loop(0, gn_ref[N_EXPERTS], tail, 0)

    loc = rt_ref[0:TOP_K, :].astype(jnp.int32)
    row = lax.broadcasted_iota(jnp.int32, (gbuf.shape[1], loc.shape[1]), 0)
    hit = row == loc[0:1, :]
    for kk in range(1, TOP_K):
        hit = jnp.logical_or(hit, row == loc[kk:kk + 1, :])
    gbuf[slot] = _dot(jnp.where(hit, 1.0, 0.0).astype(BF16), h_ref[...])

    def runs(tile, sl, wait):
        _tile_runs(gs_ref, n8_ref, tile, lambda local, g, size: pltpu.make_async_copy(
            gbuf.at[sl, pl.ds(local, size), :], xs_ref.at[pl.ds(g, size), :], sems.at[sl]), wait)

    runs(i, slot, False)

    @pl.when(i > 0)
    def _():
        runs(i - 1, 1 - slot, True)

    @pl.when(i == pl.num_programs(0) - 1)
    def _():
        runs(i, slot, True)


def _dispatch(gstart, n8, gap_off, gap_n, h2, rt, n_rows):
    t, d = h2.shape
    nt = t // MOE_TILE
    gs = pltpu.PrefetchScalarGridSpec(
        num_scalar_prefetch=4,
        grid=(nt,),
        in_specs=[pl.BlockSpec((MOE_TILE, d), lambda i, *_: (i, 0)),
                  pl.BlockSpec((None, 8, MOE_TILE), lambda i, *_: (i, 0, 0))],
        out_specs=pl.BlockSpec(memory_space=pl.ANY),
        scratch_shapes=[pltpu.VMEM((2, MOE_GBUF_ROWS, d), F32), pltpu.VMEM((MOE_ROWS // 2, d), F32),
                        pltpu.SemaphoreType.DMA((2,))],
    )
    return pl.pallas_call(
        _dispatch_kernel,
        grid_spec=gs,
        out_shape=jax.ShapeDtypeStruct((n_rows, d), F32),
        compiler_params=_cparams(("arbitrary",)),
        name="moe_dispatch",
    )(gstart, n8, gap_off, gap_n, h2, rt)


def _combine_kernel(gs_ref, n8_ref, ys_ref, rc_ref, f_ref, ybuf, sems):
    i = pl.program_id(0)
    slot = i % 2

    def runs(tile, sl, wait):
        _tile_runs(gs_ref, n8_ref, tile, lambda local, g, size: pltpu.make_async_copy(
            ys_ref.at[pl.ds(g, size), :], ybuf.at[sl, pl.ds(local, size), :], sems.at[sl]), wait)

    @pl.when(i == 0)
    def _():
        ybuf[...] = jnp.zeros_like(ybuf)
        runs(i, slot, False)

    runs(i, slot, True)

    @pl.when(i + 1 < pl.num_programs(0))
    def _():
        runs(i + 1, 1 - slot, False)

    rc = rc_ref[...]
    col = lax.broadcasted_iota(jnp.int32, (rc.shape[0], ybuf.shape[1]), 1)
    wgt = jnp.zeros(col.shape, F32)
    for kk in range(TOP_K):
        wgt = jnp.where(col == rc[:, kk:kk + 1].astype(jnp.int32), rc[:, TOP_K + kk:TOP_K + kk + 1], wgt)
    y = ybuf[slot]
    w_hi = wgt.astype(BF16)
    w_lo = (wgt - w_hi.astype(F32)).astype(BF16)
    y_hi = y.astype(BF16)
    y_lo = (y - y_hi.astype(F32)).astype(BF16)
    f_ref[...] = _dot(w_hi, y_hi) + _dot(w_hi, y_lo) + _dot(w_lo, y_hi)


def _combine(gstart, n8, ys, rc):
    t = rc.shape[0]
    d = ys.shape[1]
    nt = t // MOE_TILE
    gs = pltpu.PrefetchScalarGridSpec(
        num_scalar_prefetch=2,
        grid=(nt,),
        in_specs=[pl.BlockSpec(memory_space=pl.ANY),
                  pl.BlockSpec((MOE_TILE, LANES), lambda i, *_: (i, 0))],
        out_specs=pl.BlockSpec((MOE_TILE, d), lambda i, *_: (i, 0)),
        scratch_shapes=[pltpu.VMEM((2, MOE_GBUF_ROWS, d), F32), pltpu.SemaphoreType.DMA((2,))],
    )
    return pl.pallas_call(
        _combine_kernel,
        grid_spec=gs,
        out_shape=jax.ShapeDtypeStruct((t, d), F32),
        compiler_params=_cparams(("arbitrary",)),
        name="moe_combine",
    )(gstart, n8, ys, rc)


def _moe(h2, logits, w1, b1, w2, b2, layer):
    t, d = h2.shape
    nt = t // MOE_TILE
    rc, rt, n8f = _route(logits)
    n8 = n8f[:, 0, :N_EXPERTS].astype(jnp.int32)
    e_rows = jnp.sum(n8, 0)
    e_pad = (e_rows + MOE_ROWS - 1) // MOE_ROWS * MOE_ROWS
    e_end = jnp.cumsum(e_pad)
    e_start = e_end - e_pad
    gstart = (e_start[None, :] + jnp.cumsum(n8, 0) - n8).reshape(-1)
    n_rows = -(-(t * TOP_K + nt * N_EXPERTS * 7 + N_EXPERTS * (MOE_ROWS - 1)) // MOE_ROWS) * MOE_ROWS
    n_used = (e_end[-1:] // MOE_ROWS).astype(jnp.int32)
    blk = jnp.minimum(jnp.arange(n_rows // MOE_ROWS, dtype=jnp.int32), n_used - 1) * MOE_ROWS
    blk_e = jnp.minimum(jnp.sum(blk[:, None] >= e_end[None, :], axis=1), N_EXPERTS - 1).astype(jnp.int32)
    n8_flat = n8.reshape(-1)
    gap_off = jnp.concatenate([e_start + e_rows, e_end[-1:]]).astype(jnp.int32)
    gap_n = jnp.concatenate([e_pad - e_rows, (n_rows - e_end[-1:]) // (MOE_ROWS // 2)]).astype(jnp.int32)
    xs = _dispatch(gstart.astype(jnp.int32), n8_flat, gap_off, gap_n, h2, rt, n_rows)
    ys = _ffn(blk_e, n_used, xs, w1, b1, w2, b2, layer)
    return _combine(gstart, n8_flat, ys, rc)


def _rope_tables(s_lat, lc):
    t = jnp.arange(s_lat)
    row = (t // GRID_W).astype(F32)
    col = (t % GRID_W).astype(F32)
    n_freq = DIFF_D // 4
    inv = ROPE_BASE ** (-jnp.arange(n_freq, dtype=F32) / n_freq)
    ang = jnp.concatenate([row[:, None] * inv, col[:, None] * inv], -1)
    ang = jnp.concatenate([ang, ang, ang, ang], -1)
    sign = jnp.where((jnp.arange(LANES) % DIFF_D) < DIFF_D // 2, -1.0, 1.0)
    cos = jnp.concatenate([jnp.cos(ang), jnp.ones((lc, LANES), F32)], 0)
    sin = jnp.concatenate([jnp.sin(ang) * sign, jnp.zeros((lc, LANES), F32)], 0)
    return cos.astype(F32), sin.astype(F32)


def kernel(x, c, ctx, c_ctx, w_ada, b_ada, w_in, conv_w, gdn_a_log, gdn_dt_bias, gdn_norm_w, diff_lambda, diff_norm_w, na_rpb, w_branch, w_out, ln_g, ln_b, w_router, b_router, w_exp1, b_exp1, w_exp2, b_exp2):
    bsz, s_lat, d = x.shape
    lc = ctx.shape[1]
    depth = w_ada.shape[0]
    assert lc == TOK_TILE and s_lat % (NA_QROWS * GRID_W) == 0 and d == 1024
    lt = s_lat + lc
    nl = s_lat // TOK_TILE
    rows = s_lat // GRID_W

    xs = jnp.concatenate([x, ctx], axis=1)
    cmat = jnp.zeros((8, d), F32).at[:bsz].set(c).at[bsz].set(c_ctx)
    mod = _ada(cmat, w_ada, b_ada)
    cos, sin = _rope_tables(s_lat, lc)

    w_main = jnp.concatenate([w_in[:, :, :2048], w_in[:, :, 2064:]], axis=2).astype(BF16)
    w_ab = jnp.pad(w_in[:, :, 2048:2064], ((0, 0), (0, 0), (0, LANES - 16))).astype(BF16)
    wb16 = w_branch.astype(BF16)
    wo16 = w_out.astype(BF16)
    wr_pad = jnp.pad(w_router, ((0, 0), (0, 0), (0, LANES - N_EXPERTS)))
    br_pad = jnp.pad(b_router, ((0, 0), (0, LANES - N_EXPERTS)), constant_values=NEG_INF).reshape(depth, 1, LANES)
    lane_pad = LANES - 4 * GDN_H
    na_bias = _na_bias_tables(na_rpb, rows)

    modsels = [jnp.stack([mod[l, :bsz], jnp.broadcast_to(mod[l, bsz], (bsz, 6 * d))], axis=1)[:, :, None, :]
               for l in range(depth)]
    (h1,) = _norm(xs, nl, shift=(modsels[0], 0))
    for l in range(depth):
        lam_init = 0.8 - 0.6 * math.exp(-0.3 * l)
        modsel = modsels[l]
        tm = 512 if (bsz * lt) % 512 == 0 else TOK_TILE
        h1f = h1.reshape(bsz * lt, d)
        p = _matmul(h1f, w_main, l, tm, C_AB // 4, BF16).reshape(bsz, lt, C_AB)
        pab = _matmul(h1f, w_ab, l, tm, LANES, F32).reshape(bsz, lt, LANES)

        def head_row(v2):
            z4 = jnp.zeros((GDN_H,), F32)
            return jnp.pad(jnp.concatenate([v2[0], z4, v2[1], z4]), (0, lane_pad)).reshape(1, LANES)
        qkv_n, gcs, gct = _gdn_prep(p, pab, conv_w[l], head_row(gdn_a_log[l]), head_row(gdn_dt_bias[l]), nl)
        o_f, o_b = _gdn_scan(*_gdn_intra(qkv_n, gcs, gct), s_lat // GDN_CHUNK)

        a = _attn_prep(p, cos, sin)
        lv = diff_lambda[l].astype(F32)
        lam = jnp.exp(jnp.sum(lv[0] * lv[1])) - jnp.exp(jnp.sum(lv[2] * lv[3])) + lam_init
        lam_row = jnp.full((1, LANES), lam, F32)
        tq = min(512, s_lat)
        tkv = lt // 3 if (lt // 3) % TOK_TILE == 0 else TOK_TILE
        yb = _diff_attn(a, lam_row, diff_norm_w[l], lam_init, tq=tq, tkv=tkv, q0=0, nq=s_lat // tq,
                        k0=0, nkv=lt // tkv)
        yb_ctx = _diff_attn(a, lam_row, diff_norm_w[l], lam_init, tq=lc, tkv=lc, q0=s_lat // lc, nq=1,
                            k0=s_lat // lc, nkv=1)
        yc = _na_attn(a, na_bias, l, s_lat, lc)
        yc_ctx = _na_ctx_attn(a, s_lat, lc)

        xs, h2, logits = _merge(o_f, o_b, p, yb, yb_ctx, yc, yc_ctx, xs, gdn_norm_w[l], wb16[l], wo16[l],
                                modsel, nl, ln_g[l, 0], ln_b[l, 0], wr_pad[l], br_pad[l])
        f = _moe(h2.reshape(bsz * lt, d), logits.reshape(bsz * lt, LANES),
                 w_exp1, b_exp1, w_exp2, b_exp2, l).reshape(bsz, lt, d)
        if l + 1 < depth:
            xs, h1 = _norm(xs, nl, y=f, gate=(modsel, 5), ln_g=ln_g[l, 1], ln_b=ln_b[l, 1],
                           shift=(modsels[l + 1], 0))
        else:
            (xs,) = _norm(xs, nl, y=f, gate=(modsel, 5), ln_g=ln_g[l, 1], ln_b=ln_b[l, 1])
    return xs[:, :s_lat]
```

```python
import functools
import math

import numpy as np
import jax
import jax.numpy as jnp
from jax import lax
from jax.experimental import pallas as pl
from jax.experimental.pallas import tpu as pltpu

F32 = jnp.float32
BF16 = jnp.bfloat16

GRID_W = 64
GDN_H = 4
GDN_DK = 128
GDN_CHUNK = 64
DIFF_H = 4
DIFF_D = 64
ROPE_BASE = 10000.0
NA_H = 8
NA_D = 64
WIN_H = 8
WIN_W = 16
N_EXPERTS = 32
TOP_K = 4
D_EXPERT = 1024
SWIGLU_LIMIT = 7.0
SWIGLU_ALPHA = 1.702
DN_ALPHA = 8.0 ** 0.25
LN_EPS = 1e-5
RMS_EPS = 1e-6
NEG_INF = -1e30
LOG2E = math.log2(math.e)

LANES = 128
TOK_TILE = 256
NA_QROWS = 8
NA_KROWS = 16
MOE_ROWS = 512
INV_BASE_LOG2 = 3
GDN_INTRA_TILE = 128
GDN_SCAN_CHUNKS = 2
FFN_COL_CHUNK = 1024
DIFF_COL_TILE = 512
MOE_TILE = 256
MOE_GBUF_ROWS = MOE_TILE * TOP_K + N_EXPERTS * 8
MOE_RUN_PIECES = tuple(MOE_TILE >> s for s in range(int(math.log2(MOE_TILE)) - 2))
MOE_GAP_PIECES = tuple((MOE_ROWS // 2) >> s for s in range(int(math.log2(MOE_ROWS)) - 3))
VMEM_LIMIT = 56 * 1024 * 1024

C_GQ, C_GK, C_GV, C_GZ = 0, 512, 1024, 1536
C_DQ, C_DK, C_DV = 2048, 2560, 3072
C_NQ, C_NK, C_NV = 3584, 4096, 4608
C_GATE = 5120
C_AB = 8192


def _cparams(sem):
    return pltpu.CompilerParams(dimension_semantics=sem, vmem_limit_bytes=VMEM_LIMIT)


def _ln(x):
    mu = jnp.mean(x, -1, keepdims=True)
    xc = x - mu
    var = jnp.mean(xc * xc, -1, keepdims=True)
    return xc * lax.rsqrt(var + LN_EPS)


def _sigmoid(x):
    return 1.0 / (1.0 + jnp.exp(-x))


def _dot(a, b):
    return jnp.dot(a, b, preferred_element_type=F32)


def _dot_nt(a, b):
    return lax.dot_general(a, b, (((1,), (1,)), ((), ())), preferred_element_type=F32)


def _dot_tn(a, b):
    return lax.dot_general(a, b, (((0,), (0,)), ((), ())), preferred_element_type=F32)


def _mm(a, b):
    return jnp.dot(a.astype(BF16), b.astype(BF16), preferred_element_type=F32)


def _dot_3pass(a, b):
    a_hi = a.astype(BF16)
    b_hi = b.astype(BF16)
    a_lo = (a - a_hi.astype(F32)).astype(BF16)
    b_lo = (b - b_hi.astype(F32)).astype(BF16)
    return _dot(a_hi, b_hi) + _dot(a_hi, b_lo) + _dot(a_lo, b_hi)


def _dot_f32(a, b):
    return jnp.dot(a, b, preferred_element_type=F32, precision=lax.Precision.HIGHEST)


def _ada_kernel(c_ref, w_ref, b_ref, o_ref):
    c = c_ref[...]
    s = (c * _sigmoid(c)).astype(BF16)
    o_ref[0] = _dot(s, w_ref[0].astype(BF16)) + b_ref[0]


def _ada(cmat, w_ada, b_ada):
    depth, d, n = w_ada.shape
    tn = n // 4
    return pl.pallas_call(
        _ada_kernel,
        grid=(depth, n // tn),
        in_specs=[pl.BlockSpec((8, d), lambda l, j: (0, 0)),
                  pl.BlockSpec((1, d, tn), lambda l, j: (l, 0, j)),
                  pl.BlockSpec((1, 1, tn), lambda l, j: (l, 0, j))],
        out_specs=pl.BlockSpec((1, 8, tn), lambda l, j: (l, 0, j)),
        out_shape=jax.ShapeDtypeStruct((depth, 8, n), F32),
        compiler_params=_cparams(("parallel", "parallel")),
        name="ada",
    )(cmat, w_ada, b_ada.reshape(depth, 1, n))


def _modulate_kernel(x_ref, sh_ref, sc_ref, h_ref):
    h_ref[...] = (_ln(x_ref[...]) * (1.0 + sc_ref[...]) + sh_ref[...]).astype(BF16)


def _mod_spec(d, nl, k):
    return pl.BlockSpec((None, None, 1, d), lambda b, i: (b, i // nl, 0, k))


def _modulate(x, nl, modsel, shift_k):
    bsz, lt, d = x.shape
    tok = pl.BlockSpec((None, TOK_TILE, d), lambda b, i: (b, i, 0))
    return pl.pallas_call(
        _modulate_kernel,
        grid=(bsz, lt // TOK_TILE),
        in_specs=[tok, _mod_spec(d, nl, shift_k), _mod_spec(d, nl, shift_k + 1)],
        out_specs=tok,
        out_shape=jax.ShapeDtypeStruct((bsz, lt, d), BF16),
        compiler_params=_cparams(("parallel", "parallel")),
        name="modulate",
    )(x, modsel, modsel)


def _mm_kernel(a_ref, w_ref, o_ref):
    o_ref[...] = _dot(a_ref[...], w_ref[...]).astype(o_ref.dtype)


def _matmul(a, w, layer, tm, tn, out_dtype):
    m, k = a.shape
    n = w.shape[2]
    return pl.pallas_call(
        _mm_kernel,
        grid=(n // tn, m // tm),
        in_specs=[pl.BlockSpec((tm, k), lambda j, i: (i, 0)),
                  pl.BlockSpec((None, k, tn), lambda j, i: (layer, 0, j))],
        out_specs=pl.BlockSpec((tm, tn), lambda j, i: (i, j)),
        out_shape=jax.ShapeDtypeStruct((m, n), out_dtype),
        compiler_params=_cparams(("parallel", "parallel")),
        name="matmul",
    )(a, w)


def _gdn_prep_kernel(x_ref, hp_ref, hn_ref, ab_ref, cw_ref, alog_ref, dtb_ref,
                     qkv_ref, gc_ref, gct_ref, *, nl):
    i = pl.program_id(1)
    x = x_ref[...].astype(F32)
    t = x.shape[0]
    prev_ok = jnp.logical_and(i != 0, i != nl)
    next_ok = jnp.logical_and(i != nl - 1, i != nl)
    halo = hp_ref.shape[0]
    prow = jnp.where(prev_ok, hp_ref[halo - 1:halo, :].astype(F32), 0.0)
    nrow = jnp.where(next_ok, hn_ref[0:1, :].astype(F32), 0.0)
    rid = lax.broadcasted_iota(jnp.int32, x.shape, 0)
    xp = jnp.where(rid == 0, prow, pltpu.roll(x, 1, 0))
    xn = jnp.where(rid == t - 1, nrow, pltpu.roll(x, t - 1, 0))
    w = cw_ref[...]
    y = xp * w[0:1] + x * w[1:2] + xn * w[2:3]
    y = y * _sigmoid(y)
    for g in range(12):
        blk = y[:, g * LANES:(g + 1) * LANES]
        if g < 8:
            blk = blk * lax.rsqrt(jnp.sum(blk * blk, -1, keepdims=True) + RMS_EPS)
        if g < 4:
            blk = blk * (GDN_DK ** -0.5)
        qkv_ref[:, g * LANES:(g + 1) * LANES] = blk

    ab = ab_ref[...]
    lane = lax.broadcasted_iota(jnp.int32, ab.shape, 1)
    xs = ab + dtb_ref[...]
    softplus = jnp.maximum(xs, 0.0) + jnp.log(1.0 + jnp.exp(-jnp.abs(xs)))
    g = -jnp.exp(alog_ref[...]) * softplus
    beta = _sigmoid(ab)
    is_a = (lane % 8) < 4
    gb = jnp.where(lane < 16, jnp.where(is_a, g, beta), 0.0)
    r = lax.broadcasted_iota(jnp.int32, (t, t), 0)
    c = lax.broadcasted_iota(jnp.int32, (t, t), 1)
    same = (r // GDN_CHUNK) == (c // GDN_CHUNK)
    tri_l = jnp.where(jnp.logical_and(same, c <= r), 1.0, 0.0)
    tri_u = jnp.where(jnp.logical_and(same, c >= r), 1.0, 0.0)
    cf = _dot_f32(tri_l, gb)
    cb = _dot_f32(tri_u, gb)
    gc = jnp.where(lane < 4, cf, jnp.where(jnp.logical_and(lane >= 8, lane < 12), cb, gb))
    gc_ref[...] = gc
    gct = gc.T
    for ch in range(t // GDN_CHUNK):
        gct_ref[ch] = gct[0:16, ch * GDN_CHUNK:(ch + 1) * GDN_CHUNK]


def _gdn_prep(p, pab, conv_w, alog_row, dtb_row, nl):
    bsz, lt, _ = p.shape
    nt = lt // TOK_TILE
    w3 = 3 * 512
    halo = 16
    rb = TOK_TILE // halo
    return pl.pallas_call(
        functools.partial(_gdn_prep_kernel, nl=nl),
        grid=(bsz, nt),
        in_specs=[pl.BlockSpec((None, TOK_TILE, w3), lambda b, i: (b, i, 0)),
                  pl.BlockSpec((None, halo, w3), lambda b, i: (b, jnp.maximum(i * rb - 1, 0), 0)),
                  pl.BlockSpec((None, halo, w3), lambda b, i: (b, jnp.minimum(i * rb + rb, lt // halo - 1), 0)),
                  pl.BlockSpec((None, TOK_TILE, LANES), lambda b, i: (b, i, 0)),
                  pl.BlockSpec((3, w3), lambda b, i: (0, 0)),
                  pl.BlockSpec((1, LANES), lambda b, i: (0, 0)),
                  pl.BlockSpec((1, LANES), lambda b, i: (0, 0))],
        out_specs=[pl.BlockSpec((None, TOK_TILE, w3), lambda b, i: (b, i, 0)),
                   pl.BlockSpec((None, TOK_TILE, LANES), lambda b, i: (b, i, 0)),
                   pl.BlockSpec((None, TOK_TILE // GDN_CHUNK, 16, GDN_CHUNK), lambda b, i: (b, i, 0, 0))],
        out_shape=[jax.ShapeDtypeStruct((bsz, lt, w3), F32),
                   jax.ShapeDtypeStruct((bsz, lt, LANES), F32),
                   jax.ShapeDtypeStruct((bsz, lt // GDN_CHUNK, 16, GDN_CHUNK), F32)],
        compiler_params=_cparams(("parallel", "parallel")),
        name="gdn_prep",
    )(p, p, p, pab, conv_w, alog_row, dtb_row)


def _gdn_intra_kernel(qkv_ref, gc_ref, gct_ref, u_ref, w_ref, qg_ref, kt_ref, qk_ref, eg_ref):
    cs = GDN_CHUNK
    nch = qkv_ref.shape[0] // cs
    chains = [(d, c, h) for d in range(2) for c in range(nch) for h in range(GDN_H)]
    ri = lax.broadcasted_iota(jnp.int32, (cs, cs), 0)
    ci = lax.broadcasted_iota(jnp.int32, (cs, cs), 1)
    eye = ri == ci
    base = (ri >> INV_BASE_LOG2) == (ci >> INV_BASE_LOG2)
    incl = (ri >= ci, ri <= ci)
    strict = (ri > ci, ri < ci)
    qkv = qkv_ref[...]
    gc = gc_ref[...]

    def part(col0, c, h):
        return qkv[c * cs:(c + 1) * cs, col0 + h * LANES:col0 + (h + 1) * LANES]

    ch_keys = [(c, h) for c in range(nch) for h in range(GDN_H)]
    q = {key: part(C_GQ, *key) for key in ch_keys}
    k = {key: part(C_GK, *key) for key in ch_keys}
    v = {key: part(C_GV, *key) for key in ch_keys}
    q16 = {key: q[key].astype(BF16) for key in ch_keys}
    k16 = {key: k[key].astype(BF16) for key in ch_keys}
    gcol = [gc[c * cs:(c + 1) * cs, 8 * d + h:8 * d + h + 1] for d, c, h in chains]
    bcol = [gc[c * cs:(c + 1) * cs, 8 * d + 4 + h:8 * d + 5 + h] for d, c, h in chains]
    grow = [gct_ref[c, 8 * d + h:8 * d + h + 1, :] for d, c, h in chains]
    glast = [g[cs - 1:cs, :] if d == 0 else g[0:1, :] for g, (d, c, h) in zip(gcol, chains)]
    decay = [jnp.where(incl[d], jnp.exp(jnp.where(incl[d], gcol[n] - grow[n], 0.0)), 0.0)
             for n, (d, c, h) in enumerate(chains)]
    kb = [k[(c, h)] * bcol[n] for n, (d, c, h) in enumerate(chains)]
    amat = [jnp.where(strict[d], _dot_nt(kb[n].astype(BF16), k16[(c, h)]) * decay[n], 0.0)
            for n, (d, c, h) in enumerate(chains)]
    qk = [_dot_nt(q16[(c, h)], k16[(c, h)]) * decay[n] for n, (d, c, h) in enumerate(chains)]
    eg = [jnp.exp(g) for g in gcol]
    rhs = [jnp.concatenate([v[(c, h)] * bcol[n], kb[n] * eg[n]], axis=1) for n, (d, c, h) in enumerate(chains)]
    rmat = [jnp.where(base, -a, 0.0) for a in amat]
    mmat = rmat
    for _ in range(INV_BASE_LOG2 - 1):
        mmat = [_mm(m, m) for m in mmat]
        prod = [_mm(r, m) for r, m in zip(rmat, mmat)]
        rmat = [r + m + p for r, m, p in zip(rmat, mmat, prod)]
    tmat = [jnp.where(eye, 1.0, r) for r in rmat]
    for lb in range(INV_BASE_LOG2, int(math.log2(cs))):
        same_pair = (ri >> (lb + 1)) == (ci >> (lb + 1))
        off = (jnp.logical_and((ri >> lb) == (ci >> lb) + 1, same_pair),
               jnp.logical_and((ci >> lb) == (ri >> lb) + 1, same_pair))
        inner = [_mm(jnp.where(off[d], amat[n], 0.0), tmat[n]) for n, (d, c, h) in enumerate(chains)]
        outer = [_mm(t, x) for t, x in zip(tmat, inner)]
        tmat = [t - x for t, x in zip(tmat, outer)]
    corr = [_mm(jnp.where(eye, 0.0, t), r) for t, r in zip(tmat, rhs)]
    for n, (d, c, h) in enumerate(chains):
        rows = slice(c * cs, (c + 1) * cs)
        cols = slice(h * LANES, (h + 1) * LANES)
        sol = rhs[n] + corr[n]
        u_ref[d, rows, cols] = sol[:, :LANES]
        w_ref[d, rows, cols] = sol[:, LANES:].astype(BF16)
        qg_ref[d, rows, cols] = (q[(c, h)] * eg[n]).astype(BF16)
        kt_ref[d, rows, cols] = (k[(c, h)] * jnp.exp(glast[n] - gcol[n])).astype(BF16)
        qk_ref[d, c, h] = qk[n].astype(BF16)
        eg_ref[d, c, h:h + 1, :] = jnp.broadcast_to(jnp.exp(glast[n]), (1, LANES))


def _gdn_intra(qkv, gc, gct):
    bsz, lt, w3 = qkv.shape
    cs = GDN_CHUNK
    nc = lt // cs
    nch = GDN_INTRA_TILE // cs

    def tok(dt):
        return (pl.BlockSpec((None, 2, GDN_INTRA_TILE, 512), lambda b, i: (b, 0, i, 0)),
                jax.ShapeDtypeStruct((bsz, 2, lt, 512), dt))

    outs = [tok(F32), tok(BF16), tok(BF16), tok(BF16),
            (pl.BlockSpec((None, 2, nch, GDN_H, cs, cs), lambda b, i: (b, 0, i, 0, 0, 0)),
             jax.ShapeDtypeStruct((bsz, 2, nc, GDN_H, cs, cs), BF16)),
            (pl.BlockSpec((None, 2, nch, GDN_H, LANES), lambda b, i: (b, 0, i, 0, 0)),
             jax.ShapeDtypeStruct((bsz, 2, nc, GDN_H, LANES), F32))]
    return pl.pallas_call(
        _gdn_intra_kernel,
        grid=(bsz, lt // GDN_INTRA_TILE),
        in_specs=[pl.BlockSpec((None, GDN_INTRA_TILE, w3), lambda b, i: (b, i, 0)),
                  pl.BlockSpec((None, GDN_INTRA_TILE, LANES), lambda b, i: (b, i, 0)),
                  pl.BlockSpec((None, nch, 16, cs), lambda b, i: (b, i, 0, 0))],
        out_specs=[o[0] for o in outs],
        out_shape=[o[1] for o in outs],
        compiler_params=_cparams(("parallel", "parallel")),
        name="gdn_intra",
    )(qkv, gc, gct)


def _gdn_scan_kernel(*refs):
    s_ref = refs[-1]
    cs = GDN_CHUNK

    @pl.when(pl.program_id(1) == 0)
    def _():
        s_ref[...] = jnp.zeros_like(s_ref)

    chains = [(d, h) for d in range(2) for h in range(GDN_H)]
    src = (refs[0:6], refs[6:12])
    outs = refs[12:14]
    per_step = outs[0].shape[0] // cs

    def cols(h):
        return slice(h * LANES, (h + 1) * LANES)

    s = [s_ref[d * GDN_H + h] for d, h in chains]
    for sub in range(per_step):
        cidx = (sub, per_step - 1 - sub)
        rows = [slice(c * cs, (c + 1) * cs) for c in cidx]
        wq = [_dot(jnp.concatenate([src[d][1][rows[d], cols(h)], src[d][2][rows[d], cols(h)]], axis=0),
                   s[n].astype(BF16)) for n, (d, h) in enumerate(chains)]
        v16 = [(src[d][0][rows[d], cols(h)] - wq[n][:cs]).astype(BF16) for n, (d, h) in enumerate(chains)]
        o = [wq[n][cs:] + _dot(src[d][4][cidx[d], h], v16[n]) for n, (d, h) in enumerate(chains)]
        s = [s[n] * src[d][5][cidx[d], h:h + 1, :] + _dot_tn(src[d][3][rows[d], cols(h)], v16[n])
             for n, (d, h) in enumerate(chains)]
        for n, (d, h) in enumerate(chains):
            outs[d][rows[d], cols(h)] = o[n]
    for n, (d, h) in enumerate(chains):
        s_ref[d * GDN_H + h] = s[n]


def _gdn_scan(u, w, qg, kt, qk, eg, n_lat_chunks):
    bsz, _, lt, _ = u.shape
    per = GDN_SCAN_CHUNKS
    cs = GDN_CHUNK * per
    nc = lt // cs
    n_lat = n_lat_chunks // per
    ncc = nc - n_lat

    def cf(i):
        return jnp.where(i < ncc, n_lat + i, i - ncc)

    def cb(i):
        return nc - 1 - i

    def specs(d, c):
        tok = pl.BlockSpec((None, None, cs, 512), lambda b, i: (b, d, c(i), 0))
        return [tok, tok, tok, tok,
                pl.BlockSpec((None, None, per, GDN_H, GDN_CHUNK, GDN_CHUNK), lambda b, i: (b, d, c(i), 0, 0, 0)),
                pl.BlockSpec((None, None, per, GDN_H, LANES), lambda b, i: (b, d, c(i), 0, 0))]

    def out(c):
        return pl.BlockSpec((None, cs, 512), lambda b, i: (b, c(i), 0))

    args = (u, w, qg, kt, qk, eg)
    return pl.pallas_call(
        _gdn_scan_kernel,
        grid=(bsz, nc),
        in_specs=specs(0, cf) + specs(1, cb),
        out_specs=[out(cf), out(cb)],
        out_shape=[jax.ShapeDtypeStruct((bsz, lt, 512), F32)] * 2,
        scratch_shapes=[pltpu.VMEM((2 * GDN_H, GDN_DK, LANES), F32)],
        compiler_params=_cparams(("parallel", "arbitrary")),
        name="gdn_scan",
    )(*args, *args)


def _attn_prep_kernel(dq_ref, dk_ref, dv_ref, nq_ref, nk_ref, nv_ref, cos_ref, sin_ref, o_ref):
    cos = cos_ref[...]
    sin = sin_ref[...]
    lane = lax.broadcasted_iota(jnp.int32, cos.shape, 1)
    first = (lane % DIFF_D) < DIFF_D // 2

    def rope(x):
        rot = jnp.where(first, pltpu.roll(x, LANES - DIFF_D // 2, 1), pltpu.roll(x, DIFF_D // 2, 1))
        return x * cos + rot * sin

    for h in range(DIFF_H):
        sl = slice(h * LANES, (h + 1) * LANES)
        o_ref[:, h * LANES:(h + 1) * LANES] = (rope(dq_ref[:, sl].astype(F32))
                                               * (DIFF_D ** -0.5 * LOG2E)).astype(BF16)
        o_ref[:, 512 + h * LANES:512 + (h + 1) * LANES] = rope(dk_ref[:, sl].astype(F32)).astype(BF16)
    o_ref[:, 1024:1536] = dv_ref[...].astype(BF16)
    o_ref[:, 1536:2048] = (nq_ref[...].astype(F32) * (NA_D ** -0.5)).astype(BF16)
    o_ref[:, 2048:2560] = nk_ref[...].astype(BF16)
    o_ref[:, 2560:3072] = nv_ref[...].astype(BF16)


def _attn_prep(p, cos, sin):
    bsz, lt, _ = p.shape

    def col(cb):
        return pl.BlockSpec((None, TOK_TILE, 512), lambda b, i: (b, i, cb))

    tab = pl.BlockSpec((TOK_TILE, LANES), lambda b, i: (i, 0))
    return pl.pallas_call(
        _attn_prep_kernel,
        grid=(bsz, lt // TOK_TILE),
        in_specs=[col(C_DQ // 512), col(C_DK // 512), col(C_DV // 512),
                  col(C_NQ // 512), col(C_NK // 512), col(C_NV // 512), tab, tab],
        out_specs=pl.BlockSpec((None, TOK_TILE, 3072), lambda b, i: (b, i, 0)),
        out_shape=jax.ShapeDtypeStruct((bsz, lt, 3072), BF16),
        compiler_params=_cparams(("parallel", "parallel")),
        name="attn_prep",
    )(p, p, p, p, p, p, cos, sin)


def _diff_kernel(lam_ref, q_ref, k_ref, v_ref, nw_ref, o_ref, m_sc, l_sc, acc_sc, *, nkv, lam_init):
    kv = pl.program_id(3)

    @pl.when(kv == 0)
    def _():
        m_sc[...] = jnp.full_like(m_sc, NEG_INF)
        l_sc[...] = jnp.zeros_like(l_sc)
        acc_sc[...] = jnp.zeros_like(acc_sc)

    q = q_ref[...]
    k = k_ref[...]
    v = v_ref[...]
    maps = range(2)
    tkv = k.shape[0]
    pieces = [(c0, min(c0 + DIFF_COL_TILE, tkv)) for c0 in range(0, tkv, DIFF_COL_TILE)]

    def fold(x, op):
        acc = x[:, 0:LANES]
        for c0 in range(LANES, x.shape[1], LANES):
            acc = op(acc, x[:, c0:c0 + LANES])
        return acc

    s = [[_dot_nt(q[:, m * DIFF_D:(m + 1) * DIFF_D], k[c0:c1, m * DIFF_D:(m + 1) * DIFF_D])
          for c0, c1 in pieces] for m in maps]
    m_new, alpha = [], []
    for m in maps:
        mx = None
        for sp in s[m]:
            part = fold(sp, jnp.maximum)
            mx = part if mx is None else jnp.maximum(mx, part)
        m_prev = m_sc[m]
        m_new.append(jnp.maximum(m_prev, jnp.max(mx, -1, keepdims=True)))
        alpha.append(jnp.exp2(m_prev - m_new[m]))
    for m in maps:
        sm, pv = None, None
        for (c0, c1), sp in zip(pieces, s[m]):
            p = jnp.exp2(sp - m_new[m])
            psum = fold(p, jnp.add)
            sm = psum if sm is None else sm + psum
            part = _dot(p.astype(BF16), v[c0:c1, :])
            pv = part if pv is None else pv + part
        l_sc[m] = alpha[m] * l_sc[m] + jnp.sum(sm, -1, keepdims=True)
        acc_sc[m] = alpha[m] * acc_sc[m] + pv
        m_sc[m] = m_new[m]

    @pl.when(kv == nkv - 1)
    def _():
        lam = lam_ref[0:1, 0:1]
        o = acc_sc[0] / l_sc[0] - lam * (acc_sc[1] / l_sc[1])
        y = o * lax.rsqrt(jnp.mean(o * o, -1, keepdims=True) + RMS_EPS) * nw_ref[...] * (1.0 - lam_init)
        o_ref[...] = y.astype(BF16)


def _diff_attn(a, lam_row, norm_w, lam_init, *, tq, tkv, q0, nq, k0, nkv):
    bsz = a.shape[0]
    return pl.pallas_call(
        functools.partial(_diff_kernel, nkv=nkv, lam_init=lam_init),
        grid=(bsz, DIFF_H, nq, nkv),
        in_specs=[pl.BlockSpec((1, LANES), lambda b, h, i, j: (0, 0)),
                  pl.BlockSpec((None, tq, LANES), lambda b, h, i, j: (b, q0 + i, h)),
                  pl.BlockSpec((None, tkv, LANES), lambda b, h, i, j: (b, k0 + j, 4 + h)),
                  pl.BlockSpec((None, tkv, LANES), lambda b, h, i, j: (b, k0 + j, 8 + h)),
                  pl.BlockSpec((1, LANES), lambda b, h, i, j: (0, 0))],
        out_specs=pl.BlockSpec((None, tq, LANES), lambda b, h, i, j: (b, i, h)),
        out_shape=jax.ShapeDtypeStruct((bsz, nq * tq, 512), BF16),
        scratch_shapes=[pltpu.VMEM((2, tq, 1), F32), pltpu.VMEM((2, tq, 1), F32),
                        pltpu.VMEM((2, tq, LANES), F32)],
        compiler_params=_cparams(("parallel", "parallel", "parallel", "arbitrary")),
        name="diff_attn",
    )(lam_row, a, a, a, norm_w.reshape(1, LANES))


def _na_kernel(q_ref, k_ref, v_ref, bias_ref, o_ref, *, s_lat, lc):
    i = pl.program_id(2)
    tq = q_ref.shape[0]
    nk = NA_KROWS * GRID_W
    kstart = jnp.clip(i * tq - (NA_KROWS - NA_QROWS) // 2 * GRID_W, 0, s_lat - nk)
    kstart = pl.multiple_of(kstart, TOK_TILE)
    q = q_ref[...]
    k_loc = k_ref[pl.ds(kstart, nk), :]
    v_loc = v_ref[pl.ds(kstart, nk), :]
    k_ctx = k_ref[s_lat:s_lat + lc, :]
    v_ctx = v_ref[s_lat:s_lat + lc, :]
    outs = []
    for j in range(2):
        sl = slice(j * NA_D, (j + 1) * NA_D)
        qh = q[:, sl]
        s_loc = _dot_nt(qh, k_loc[:, sl]) + bias_ref[j]
        s_ctx = _dot_nt(qh, k_ctx[:, sl])
        m = jnp.maximum(jnp.max(s_loc, -1, keepdims=True), jnp.max(s_ctx, -1, keepdims=True))
        p_loc = jnp.exp(s_loc - m)
        p_ctx = jnp.exp(s_ctx - m)
        l = jnp.sum(p_loc, -1, keepdims=True) + jnp.sum(p_ctx, -1, keepdims=True)
        o = _dot(p_loc.astype(BF16), v_loc[:, sl]) + _dot(p_ctx.astype(BF16), v_ctx[:, sl])
        outs.append(o / l)
    o_ref[...] = jnp.concatenate(outs, axis=1).astype(BF16)


def _na_attn(a, bias, layer, s_lat, lc):
    bsz, lt, _ = a.shape
    tq = NA_QROWS * GRID_W
    nq = s_lat // tq

    def variant(i):
        return jnp.where(i == 0, 0, jnp.where(i == nq - 1, 2, 1))

    return pl.pallas_call(
        functools.partial(_na_kernel, s_lat=s_lat, lc=lc),
        grid=(bsz, NA_H // 2, nq),
        in_specs=[pl.BlockSpec((None, tq, LANES), lambda b, h, i: (b, i, 12 + h)),
                  pl.BlockSpec((None, lt, LANES), lambda b, h, i: (b, 0, 16 + h)),
                  pl.BlockSpec((None, lt, LANES), lambda b, h, i: (b, 0, 20 + h)),
                  pl.BlockSpec((None, None, 2, tq, NA_KROWS * GRID_W),
                               lambda b, h, i: (layer, variant(i), h, 0, 0))],
        out_specs=pl.BlockSpec((None, tq, LANES), lambda b, h, i: (b, i, h)),
        out_shape=jax.ShapeDtypeStruct((bsz, s_lat, 512), BF16),
        compiler_params=_cparams(("parallel", "parallel", "arbitrary")),
        name="na_attn",
    )(a, a, a, bias)


def _na_ctx_kernel(q_ref, k_ref, v_ref, o_ref):
    q = q_ref[...]
    k = k_ref[...]
    v = v_ref[...]
    outs = []
    for j in range(2):
        sl = slice(j * NA_D, (j + 1) * NA_D)
        s = _dot_nt(q[:, sl], k[:, sl])
        p = jnp.exp(s - jnp.max(s, -1, keepdims=True))
        outs.append(_dot(p.astype(BF16), v[:, sl]) / jnp.sum(p, -1, keepdims=True))
    o_ref[...] = jnp.concatenate(outs, axis=1).astype(BF16)


def _na_ctx_attn(a, s_lat, lc):
    bsz = a.shape[0]
    rb = s_lat // lc

    def blk(c0):
        return pl.BlockSpec((None, lc, LANES), lambda b, h: (b, rb, c0 + h))

    return pl.pallas_call(
        _na_ctx_kernel,
        grid=(bsz, NA_H // 2),
        in_specs=[blk(12), blk(16), blk(20)],
        out_specs=pl.BlockSpec((None, lc, LANES), lambda b, h: (b, 0, h)),
        out_shape=jax.ShapeDtypeStruct((bsz, lc, 512), BF16),
        compiler_params=_cparams(("parallel", "parallel")),
        name="na_ctx_attn",
    )(a, a, a)


def _na_bias_tables(rpb, rows):
    qr = np.arange(NA_QROWS)[:, None]
    kr = np.arange(NA_KROWS)[None, :]
    qc = np.arange(GRID_W)[:, None]
    kc = np.arange(GRID_W)[None, :]
    w0 = np.clip(qc - WIN_W // 2, 0, GRID_W - WIN_W)
    okc = (kc >= w0) & (kc < w0 + WIN_W)
    dc = np.clip(kc - qc + WIN_W - 1, 0, 2 * WIN_W - 2)
    half = (NA_KROWS - NA_QROWS) // 2
    n_dr, n_dc = 2 * WIN_H - 1, 2 * WIN_W - 1
    sel_c = (dc[None] == np.arange(n_dc)[:, None, None]) & okc[None]
    sel_p = np.zeros((3, NA_QROWS, NA_KROWS // 2, n_dr + 1), bool)
    ok = np.zeros((3, NA_QROWS, GRID_W, NA_KROWS, GRID_W), bool)
    for vi, (r_start, delta) in enumerate(((0, 0), (NA_QROWS, -half), (rows - NA_QROWS, -2 * half))):
        r = r_start + qr
        kabs = r_start + delta + kr
        r0 = np.clip(r - WIN_H // 2, 0, rows - WIN_H)
        okr = (kabs >= r0) & (kabs < r0 + WIN_H)
        ok[vi] = okr[:, None, :, None] & okc[None, :, None, :]
        dr_first = (kabs - r + WIN_H - 1)[:, 0::2]
        sel_p[vi] = (dr_first[..., None] + 1) == np.arange(n_dr + 1)
    toep = jnp.einsum('lhaj,jqk->lhaqk', rpb, sel_c.astype(np.float32), precision=lax.Precision.HIGHEST)
    toep = jnp.pad(toep, ((0, 0), (0, 0), (1, 1), (0, 0), (0, 0)))
    toep_pair = jnp.concatenate([toep[:, :, :-1], toep[:, :, 1:]], axis=-1)
    tab = jnp.einsum('vrpa,lhaqw->lvhrqpw', sel_p.astype(np.float32), toep_pair, precision=lax.Precision.HIGHEST)
    tab = tab.reshape(rpb.shape[0], 3, NA_H, NA_QROWS * GRID_W, NA_KROWS * GRID_W)
    ok = ok.reshape(1, 3, 1, NA_QROWS * GRID_W, NA_KROWS * GRID_W)
    return jnp.where(ok, tab, NEG_INF).astype(F32)


def _merge_kernel(of_ref, ob_ref, z_ref, ybl_ref, ybc_ref, ycl_ref, ycc_ref, ga_ref, gb_ref, gc_ref, x_ref,
                  gnw_ref, wb_ref, wo_ref, g1_ref, lg_ref, lb_ref, sh_ref, sc_ref, wr_ref, br_ref,
                  xo_ref, h_ref, lg_out_ref, *, nl):
    is_ctx = pl.program_id(1) >= nl
    yb = jnp.where(is_ctx, ybc_ref[...], ybl_ref[...])
    yc = jnp.where(is_ctx, ycc_ref[...], ycl_ref[...])
    o = of_ref[...] + ob_ref[...]
    z = z_ref[...].astype(F32)
    gnw = gnw_ref[...]
    parts = []
    for h in range(GDN_H):
        sl = slice(h * LANES, (h + 1) * LANES)
        oh = o[:, sl]
        zh = z[:, sl]
        parts.append(oh * lax.rsqrt(jnp.mean(oh * oh, -1, keepdims=True) + RMS_EPS) * gnw * (zh * _sigmoid(zh)))
    ya = jnp.concatenate(parts, axis=1).astype(BF16)
    m = (_sigmoid(ga_ref[...].astype(F32)) * _dot(ya, wb_ref[0])
         + _sigmoid(gb_ref[...].astype(F32)) * _dot(yb, wb_ref[1])
         + _sigmoid(gc_ref[...].astype(F32)) * _dot(yc, wb_ref[2]))
    mx = _dot(m.astype(BF16), wo_ref[...])
    x = _ln(DN_ALPHA * x_ref[...] + g1_ref[...] * mx) * lg_ref[...] + lb_ref[...]
    xo_ref[...] = x
    hh = _ln(x) * (1.0 + sc_ref[...]) + sh_ref[...]
    h_ref[...] = hh.astype(BF16)
    lg_out_ref[...] = _dot_3pass(hh, wr_ref[...]) + br_ref[...]


def _merge(o_f, o_b, p, yb, yb_ctx, yc, yc_ctx, x, gnw, wb, wo, modsel, nl, ln_g, ln_b, wr, br):
    bsz, lt, d = x.shape

    def tok(width, cb):
        return pl.BlockSpec((None, TOK_TILE, width), lambda b, i: (b, i, cb))

    def const(shape):
        return pl.BlockSpec(shape, lambda b, i: (0,) * len(shape))

    lat = pl.BlockSpec((None, TOK_TILE, 512), lambda b, i: (b, jnp.minimum(i, nl - 1), 0))
    cxt = pl.BlockSpec((None, TOK_TILE, 512), lambda b, i: (b, 0, 0))
    return pl.pallas_call(
        functools.partial(_merge_kernel, nl=nl),
        grid=(bsz, lt // TOK_TILE),
        in_specs=[tok(512, 0), tok(512, 0), tok(512, C_GZ // 512), lat, cxt, lat, cxt,
                  tok(d, C_GATE // d), tok(d, C_GATE // d + 1), tok(d, C_GATE // d + 2), tok(d, 0),
                  const((1, LANES)), const((3, 512, d)), const((d, d)),
                  _mod_spec(d, nl, 2), const((1, d)), const((1, d)),
                  _mod_spec(d, nl, 3), _mod_spec(d, nl, 4),
                  const((d, LANES)), const((1, LANES))],
        out_specs=[tok(d, 0), tok(d, 0), tok(LANES, 0)],
        out_shape=[jax.ShapeDtypeStruct((bsz, lt, d), F32),
                   jax.ShapeDtypeStruct((bsz, lt, d), BF16),
                   jax.ShapeDtypeStruct((bsz, lt, LANES), F32)],
        compiler_params=_cparams(("parallel", "parallel")),
        name="merge",
    )(o_f, o_b, p, yb, yb_ctx, yc, yc_ctx, p, p, p, x, gnw.reshape(1, LANES), wb, wo,
      modsel, ln_g.reshape(1, d), ln_b.reshape(1, d), modsel, modsel, wr, br)


def _ffn_kernel(be_ref, nu_ref, x_ref, w1_ref, b1_ref, w2_ref, b2_ref, o_ref, w1c, w2c):
    i = pl.program_id(0)

    @pl.when(i < nu_ref[0])
    def _():
        changed = jnp.logical_or(i == 0, be_ref[i] != be_ref[jnp.maximum(i - 1, 0)])

        @pl.when(changed)
        def _():
            w1c[...] = w1_ref[...].astype(BF16)
            w2c[...] = w2_ref[...].astype(BF16)

        x = x_ref[...].astype(BF16)
        y = None
        for c0 in range(0, D_EXPERT, FFN_COL_CHUNK):
            c1 = c0 + FFN_COL_CHUNK
            hg = _dot(x, w1c[:, c0:c1]) + b1_ref[:, c0:c1]
            hl = _dot(x, w1c[:, D_EXPERT + c0:D_EXPERT + c1]) + b1_ref[:, D_EXPERT + c0:D_EXPERT + c1]
            x_glu = jnp.minimum(hg, SWIGLU_LIMIT)
            x_lin = jnp.clip(hl, -SWIGLU_LIMIT, SWIGLU_LIMIT)
            act = x_glu * _sigmoid(SWIGLU_ALPHA * x_glu) * (x_lin + 1.0)
            part = _dot(act.astype(BF16), w2c[c0:c1, :])
            y = part if y is None else y + part
        o_ref[...] = y + b2_ref[...]

    @pl.when(i >= nu_ref[0])
    def _():
        o_ref[...] = jnp.zeros_like(o_ref)


def _ffn(blk_e, n_used, xs, w1, b1, w2, b2, layer):
    n_rows, d = xs.shape
    n_blk = n_rows // MOE_ROWS
    depth, n_exp, _, de2 = w1.shape
    gs = pltpu.PrefetchScalarGridSpec(
        num_scalar_prefetch=2,
        grid=(n_blk,),
        in_specs=[pl.BlockSpec((MOE_ROWS, d), lambda i, be, nu: (i, 0)),
                  pl.BlockSpec((None, None, d, de2), lambda i, be, nu: (layer, be[i], 0, 0)),
                  pl.BlockSpec((None, None, 1, de2), lambda i, be, nu: (layer, be[i], 0, 0)),
                  pl.BlockSpec((None, None, de2 // 2, d), lambda i, be, nu: (layer, be[i], 0, 0)),
                  pl.BlockSpec((None, None, 1, d), lambda i, be, nu: (layer, be[i], 0, 0))],
        out_specs=pl.BlockSpec((MOE_ROWS, d), lambda i, be, nu: (i, 0)),
        scratch_shapes=[pltpu.VMEM((d, de2), BF16), pltpu.VMEM((de2 // 2, d), BF16)],
    )
    return pl.pallas_call(
        _ffn_kernel,
        grid_spec=gs,
        out_shape=jax.ShapeDtypeStruct((n_rows, d), F32),
        compiler_params=_cparams(("arbitrary",)),
        name="ffn",
    )(blk_e, n_used, xs, w1, b1.reshape(depth, n_exp, 1, de2), w2, b2.reshape(depth, n_exp, 1, d))


def _route_kernel(lg_ref, rc_ref, rt_ref, n8_ref):
    l = lg_ref[...]
    tt = l.shape[0]
    lane = lax.broadcasted_iota(jnp.int32, l.shape, 1)
    picks, vals = [], []
    for _ in range(TOP_K):
        m = jnp.max(l, -1, keepdims=True)
        idx = jnp.min(jnp.where(l == m, lane, LANES), -1, keepdims=True)
        oh = lane == idx
        picks.append(oh)
        vals.append(m)
        l = jnp.where(oh, -jnp.inf, l)
    sel = sum(jnp.where(oh, 1.0, 0.0) for oh in picks)
    cnt = jnp.sum(sel, 0, keepdims=True)
    n8 = jnp.floor((cnt + 7.0) * 0.125) * 8.0
    r = lax.broadcasted_iota(jnp.int32, (LANES, LANES), 0)
    c = lax.broadcasted_iota(jnp.int32, (LANES, LANES), 1)
    off8 = _dot_f32(jnp.broadcast_to(n8, (8, LANES)), jnp.where(r < c, 1.0, 0.0))[0:1]
    tr = lax.broadcasted_iota(jnp.int32, (tt, tt), 0)
    tc = lax.broadcasted_iota(jnp.int32, (tt, tt), 1)
    rank = _dot(jnp.where(tc < tr, 1.0, 0.0).astype(BF16), sel.astype(BF16))
    slot = off8 + rank
    es = [jnp.exp(v - vals[0]) for v in vals]
    den = sum(es)
    rc = jnp.zeros(l.shape, F32)
    for kk in range(TOP_K):
        loc = jnp.sum(jnp.where(picks[kk], slot, 0.0), -1, keepdims=True)
        rc = jnp.where(lane == kk, loc, rc)
        rc = jnp.where(lane == TOP_K + kk, es[kk] / den, rc)
    rc_ref[...] = rc
    rt_ref[...] = rc.T[0:8, :]
    n8_ref[...] = jnp.broadcast_to(n8, (8, LANES))


def _route(logits):
    t = logits.shape[0]
    nt = t // MOE_TILE
    return pl.pallas_call(
        _route_kernel,
        grid=(nt,),
        in_specs=[pl.BlockSpec((MOE_TILE, LANES), lambda i: (i, 0))],
        out_specs=[pl.BlockSpec((MOE_TILE, LANES), lambda i: (i, 0)),
                   pl.BlockSpec((None, 8, MOE_TILE), lambda i: (i, 0, 0)),
                   pl.BlockSpec((None, 8, LANES), lambda i: (i, 0, 0))],
        out_shape=[jax.ShapeDtypeStruct((t, LANES), F32),
                   jax.ShapeDtypeStruct((nt, 8, MOE_TILE), F32),
                   jax.ShapeDtypeStruct((nt, 8, LANES), F32)],
        compiler_params=_cparams(("parallel",)),
        name="moe_route",
    )(logits)


def _run_copies(n, pieces, make_copy, wait):
    done = jnp.int32(0)
    for size in pieces:
        take = (n & size) != 0

        @pl.when(take)
        def _():
            cp = make_copy(done, size)
            if wait:
                cp.wait()
            else:
                cp.start()

        done = done + jnp.where(take, size, 0)


def _tile_runs(gs_ref, n8_ref, tile, make_copy, wait):
    def body(e, local):
        n = n8_ref[tile * N_EXPERTS + e]
        g = gs_ref[tile * N_EXPERTS + e]
        _run_copies(n, MOE_RUN_PIECES,
                    lambda done, size: make_copy(pl.multiple_of(local + done, 8),
                                                 pl.multiple_of(g + done, 8), size), wait)
        return local + n
    lax.fori_loop(0, N_EXPERTS, body, jnp.int32(0))


def _dispatch_kernel(gs_ref, n8_ref, go_ref, gn_ref, h_ref, rt_ref, xs_ref, gbuf, zbuf, sems):
    i = pl.program_id(0)
    slot = i % 2
    sem = sems.at[0]

    @pl.when(i == 0)
    def _():
        zbuf[...] = jnp.zeros_like(zbuf)
        zrows = zbuf.shape[0]
        for wait in (False, True):
            def body(e, carry):
                _run_copies(gn_ref[e], MOE_GAP_PIECES,
                            lambda done, size: pltpu.make_async_copy(
                                zbuf.at[pl.ds(0, size), :],
                                xs_ref.at[pl.ds(pl.multiple_of(go_ref[e] + done, 8), size), :], sem), wait)
                return carry
            lax.fori_loop(0, N_EXPERTS, body, 0)

            def tail(j, carry):
                cp = pltpu.make_async_copy(
                    zbuf, xs_ref.at[pl.ds(pl.multiple_of(go_ref[N_EXPERTS] + j * zrows, 8), zrows), :], sem)
                if wait:
                    cp.wait()
                else:
                    cp.start()
                return carry
            lax.fori_loop(0, gn_ref[N_EXPERTS], tail, 0)

    loc = rt_ref[0:TOP_K, :].astype(jnp.int32)
    row = lax.broadcasted_iota(jnp.int32, (gbuf.shape[1], loc.shape[1]), 0)
    hit = row == loc[0:1, :]
    for kk in range(1, TOP_K):
        hit = jnp.logical_or(hit, row == loc[kk:kk + 1, :])
    gbuf[slot] = _dot(jnp.where(hit, 1.0, 0.0).astype(BF16), h_ref[...])

    def runs(tile, sl, wait):
        _tile_runs(gs_ref, n8_ref, tile, lambda local, g, size: pltpu.make_async_copy(
            gbuf.at[sl, pl.ds(local, size), :], xs_ref.at[pl.ds(g, size), :], sems.at[sl]), wait)

    runs(i, slot, False)

    @pl.when(i > 0)
    def _():
        runs(i - 1, 1 - slot, True)

    @pl.when(i == pl.num_programs(0) - 1)
    def _():
        runs(i, slot, True)


def _dispatch(gstart, n8, gap_off, gap_n, h2, rt, n_rows):
    t, d = h2.shape
    nt = t // MOE_TILE
    gs = pltpu.PrefetchScalarGridSpec(
        num_scalar_prefetch=4,
        grid=(nt,),
        in_specs=[pl.BlockSpec((MOE_TILE, d), lambda i, *_: (i, 0)),
                  pl.BlockSpec((None, 8, MOE_TILE), lambda i, *_: (i, 0, 0))],
        out_specs=pl.BlockSpec(memory_space=pl.ANY),
        scratch_shapes=[pltpu.VMEM((2, MOE_GBUF_ROWS, d), F32), pltpu.VMEM((MOE_ROWS // 2, d), F32),
                        pltpu.SemaphoreType.DMA((2,))],
    )
    return pl.pallas_call(
        _dispatch_kernel,
        grid_spec=gs,
        out_shape=jax.ShapeDtypeStruct((n_rows, d), F32),
        compiler_params=_cparams(("arbitrary",)),
        name="moe_dispatch",
    )(gstart, n8, gap_off, gap_n, h2, rt)


def _combine_kernel(gs_ref, n8_ref, ys_ref, rc_ref, x_ref, g_ref, lg_ref, lb_ref, *refs, has_h):
    if has_h:
        sh_ref, sc_ref, xo_ref, h_ref, ybuf, sems = refs
    else:
        xo_ref, ybuf, sems = refs
    i = pl.program_id(0)
    slot = i % 2

    def runs(tile, sl, wait):
        _tile_runs(gs_ref, n8_ref, tile, lambda local, g, size: pltpu.make_async_copy(
            ys_ref.at[pl.ds(g, size), :], ybuf.at[sl, pl.ds(local, size), :], sems.at[sl]), wait)

    @pl.when(i == 0)
    def _():
        ybuf[...] = jnp.zeros_like(ybuf)
        runs(i, slot, False)

    runs(i, slot, True)

    @pl.when(i + 1 < pl.num_programs(0))
    def _():
        runs(i + 1, 1 - slot, False)

    rc = rc_ref[...]
    col = lax.broadcasted_iota(jnp.int32, (rc.shape[0], ybuf.shape[1]), 1)
    wgt = jnp.zeros(col.shape, F32)
    for kk in range(TOP_K):
        wgt = jnp.where(col == rc[:, kk:kk + 1].astype(jnp.int32), rc[:, TOP_K + kk:TOP_K + kk + 1], wgt)
    y = ybuf[slot]
    w_hi = wgt.astype(BF16)
    w_lo = (wgt - w_hi.astype(F32)).astype(BF16)
    y_hi = y.astype(BF16)
    y_lo = (y - y_hi.astype(F32)).astype(BF16)
    f = _dot(w_hi, y_hi) + _dot(w_hi, y_lo) + _dot(w_lo, y_hi)
    x = _ln(DN_ALPHA * x_ref[...] + g_ref[...] * f) * lg_ref[...] + lb_ref[...]
    xo_ref[...] = x
    if has_h:
        h_ref[...] = (_ln(x) * (1.0 + sc_ref[...]) + sh_ref[...]).astype(BF16)


def _combine(gstart, n8, ys, rc, x, nl, tiles_per_batch, gate, ln_g, ln_b, shift=None):
    t, d = x.shape
    nt = t // MOE_TILE
    has_h = shift is not None

    def mod(k):
        return pl.BlockSpec((None, None, 1, d),
                            lambda i, *_: (i // tiles_per_batch, (i % tiles_per_batch) // nl, 0, k))

    tok = pl.BlockSpec((MOE_TILE, d), lambda i, *_: (i, 0))
    vec = pl.BlockSpec((1, d), lambda i, *_: (0, 0))
    in_specs = [pl.BlockSpec(memory_space=pl.ANY), pl.BlockSpec((MOE_TILE, LANES), lambda i, *_: (i, 0)),
                tok, mod(gate[1]), vec, vec]
    args = [ys, rc, x, gate[0], ln_g.reshape(1, d), ln_b.reshape(1, d)]
    out_specs, out_shape = [tok], [jax.ShapeDtypeStruct((t, d), F32)]
    if has_h:
        in_specs += [mod(shift[1]), mod(shift[1] + 1)]
        args += [shift[0], shift[0]]
        out_specs.append(tok)
        out_shape.append(jax.ShapeDtypeStruct((t, d), BF16))
    gs = pltpu.PrefetchScalarGridSpec(
        num_scalar_prefetch=2,
        grid=(nt,),
        in_specs=in_specs,
        out_specs=out_specs,
        scratch_shapes=[pltpu.VMEM((2, MOE_GBUF_ROWS, d), F32), pltpu.SemaphoreType.DMA((2,))],
    )
    return pl.pallas_call(
        functools.partial(_combine_kernel, has_h=has_h),
        grid_spec=gs,
        out_shape=out_shape,
        compiler_params=_cparams(("arbitrary",)),
        name="moe_combine",
    )(gstart, n8, *args)


def _moe(h2, logits, w1, b1, w2, b2, layer, **post):
    t, d = h2.shape
    nt = t // MOE_TILE
    rc, rt, n8f = _route(logits)
    n8 = n8f[:, 0, :N_EXPERTS].astype(jnp.int32)
    e_rows = jnp.sum(n8, 0)
    e_pad = (e_rows + MOE_ROWS - 1) // MOE_ROWS * MOE_ROWS
    e_end = jnp.cumsum(e_pad)
    e_start = e_end - e_pad
    gstart = (e_start[None, :] + jnp.cumsum(n8, 0) - n8).reshape(-1)
    n_rows = -(-(t * TOP_K + nt * N_EXPERTS * 7 + N_EXPERTS * (MOE_ROWS - 1)) // MOE_ROWS) * MOE_ROWS
    n_used = (e_end[-1:] // MOE_ROWS).astype(jnp.int32)
    blk = jnp.minimum(jnp.arange(n_rows // MOE_ROWS, dtype=jnp.int32), n_used - 1) * MOE_ROWS
    blk_e = jnp.minimum(jnp.sum(blk[:, None] >= e_end[None, :], axis=1), N_EXPERTS - 1).astype(jnp.int32)
    n8_flat = n8.reshape(-1)
    gap_off = jnp.concatenate([e_start + e_rows, e_end[-1:]]).astype(jnp.int32)
    gap_n = jnp.concatenate([e_pad - e_rows, (n_rows - e_end[-1:]) // (MOE_ROWS // 2)]).astype(jnp.int32)
    xs = _dispatch(gstart.astype(jnp.int32), n8_flat, gap_off, gap_n, h2, rt, n_rows)
    ys = _ffn(blk_e, n_used, xs, w1, b1, w2, b2, layer)
    return _combine(gstart, n8_flat, ys, rc, **post)


def _rope_tables(s_lat, lc):
    t = jnp.arange(s_lat)
    row = (t // GRID_W).astype(F32)
    col = (t % GRID_W).astype(F32)
    n_freq = DIFF_D // 4
    inv = ROPE_BASE ** (-jnp.arange(n_freq, dtype=F32) / n_freq)
    ang = jnp.concatenate([row[:, None] * inv, col[:, None] * inv], -1)
    ang = jnp.concatenate([ang, ang, ang, ang], -1)
    sign = jnp.where((jnp.arange(LANES) % DIFF_D) < DIFF_D // 2, -1.0, 1.0)
    cos = jnp.concatenate([jnp.cos(ang), jnp.ones((lc, LANES), F32)], 0)
    sin = jnp.concatenate([jnp.sin(ang) * sign, jnp.zeros((lc, LANES), F32)], 0)
    return cos.astype(F32), sin.astype(F32)


def kernel(x, c, ctx, c_ctx, w_ada, b_ada, w_in, conv_w, gdn_a_log, gdn_dt_bias, gdn_norm_w, diff_lambda, diff_norm_w, na_rpb, w_branch, w_out, ln_g, ln_b, w_router, b_router, w_exp1, b_exp1, w_exp2, b_exp2):
    bsz, s_lat, d = x.shape
    lc = ctx.shape[1]
    depth = w_ada.shape[0]
    assert lc == TOK_TILE == MOE_TILE and s_lat % (NA_QROWS * GRID_W) == 0 and d == 1024 and depth == 4
    lt = s_lat + lc
    nl = s_lat // TOK_TILE
    rows = s_lat // GRID_W
    assert rows >= 3 * NA_QROWS

    xs = jnp.concatenate([x, ctx], axis=1)
    cmat = jnp.zeros((8, d), F32).at[:bsz].set(c).at[bsz].set(c_ctx)
    mod = _ada(cmat, w_ada, b_ada)
    cos, sin = _rope_tables(s_lat, lc)

    w_main = jnp.concatenate([w_in[:, :, :2048], w_in[:, :, 2064:]], axis=2).astype(BF16)
    w_ab = jnp.pad(w_in[:, :, 2048:2064], ((0, 0), (0, 0), (0, LANES - 16))).astype(BF16)
    wb16 = w_branch.astype(BF16)
    wo16 = w_out.astype(BF16)
    wr_pad = jnp.pad(w_router, ((0, 0), (0, 0), (0, LANES - N_EXPERTS)))
    br_pad = jnp.pad(b_router, ((0, 0), (0, LANES - N_EXPERTS)), constant_values=NEG_INF).reshape(depth, 1, LANES)
    lane_pad = LANES - 4 * GDN_H
    na_bias = _na_bias_tables(na_rpb, rows)

    modsels = [jnp.stack([mod[l, :bsz], jnp.broadcast_to(mod[l, bsz], (bsz, 6 * d))], axis=1)[:, :, None, :]
               for l in range(depth)]
    h1 = _modulate(xs, nl, modsels[0], 0)
    for l in range(depth):
        lam_init = 0.8 - 0.6 * math.exp(-0.3 * l)
        modsel = modsels[l]
        tm = 512 if (bsz * lt) % 512 == 0 else TOK_TILE
        h1f = h1.reshape(bsz * lt, d)
        p = _matmul(h1f, w_main, l, tm, C_AB // 4, BF16).reshape(bsz, lt, C_AB)
        pab = _matmul(h1f, w_ab, l, tm, LANES, F32).reshape(bsz, lt, LANES)

        def head_row(v2):
            z4 = jnp.zeros((GDN_H,), F32)
            return jnp.pad(jnp.concatenate([v2[0], z4, v2[1], z4]), (0, lane_pad)).reshape(1, LANES)
        qkv_n, gcs, gct = _gdn_prep(p, pab, conv_w[l], head_row(gdn_a_log[l]), head_row(gdn_dt_bias[l]), nl)
        o_f, o_b = _gdn_scan(*_gdn_intra(qkv_n, gcs, gct), s_lat // GDN_CHUNK)

        a = _attn_prep(p, cos, sin)
        lv = diff_lambda[l].astype(F32)
        lam = jnp.exp(jnp.sum(lv[0] * lv[1])) - jnp.exp(jnp.sum(lv[2] * lv[3])) + lam_init
        lam_row = jnp.full((1, LANES), lam, F32)
        tq = min(512, s_lat)
        tkv = lt // 3 if (lt // 3) % TOK_TILE == 0 else TOK_TILE
        yb = _diff_attn(a, lam_row, diff_norm_w[l], lam_init, tq=tq, tkv=tkv, q0=0, nq=s_lat // tq,
                        k0=0, nkv=lt // tkv)
        yb_ctx = _diff_attn(a, lam_row, diff_norm_w[l], lam_init, tq=lc, tkv=lc, q0=s_lat // lc, nq=1,
                            k0=s_lat // lc, nkv=1)
        yc = _na_attn(a, na_bias, l, s_lat, lc)
        yc_ctx = _na_ctx_attn(a, s_lat, lc)

        xs, h2, logits = _merge(o_f, o_b, p, yb, yb_ctx, yc, yc_ctx, xs, gdn_norm_w[l], wb16[l], wo16[l],
                                modsel, nl, ln_g[l, 0], ln_b[l, 0], wr_pad[l], br_pad[l])
        res = _moe(h2.reshape(bsz * lt, d), logits.reshape(bsz * lt, LANES), w_exp1, b_exp1, w_exp2, b_exp2, l,
                   x=xs.reshape(bsz * lt, d), nl=nl, tiles_per_batch=lt // MOE_TILE, gate=(modsel, 5),
                   ln_g=ln_g[l, 1], ln_b=ln_b[l, 1], shift=(modsels[l + 1], 0) if l + 1 < depth else None)
        xs = res[0].reshape(bsz, lt, d)
        if l + 1 < depth:
            h1 = res[1].reshape(bsz, lt, d)
    return xs[:, :s_lat]
```

```python
import functools
import math

import numpy as np
import jax
import jax.numpy as jnp
from jax import lax
from jax.experimental import pallas as pl
from jax.experimental.pallas import tpu as pltpu

F32 = jnp.float32
BF16 = jnp.bfloat16

GRID_W = 64
GDN_H = 4
GDN_DK = 128
GDN_CHUNK = 64
DIFF_H = 4
DIFF_D = 64
ROPE_BASE = 10000.0
NA_H = 8
NA_D = 64
WIN_H = 8
WIN_W = 16
N_EXPERTS = 32
TOP_K = 4
D_EXPERT = 1024
SWIGLU_LIMIT = 7.0
SWIGLU_ALPHA = 1.702
DN_ALPHA = 8.0 ** 0.25
LN_EPS = 1e-5
RMS_EPS = 1e-6
NEG_INF = -1e30
LOG2E = math.log2(math.e)

LANES = 128
TOK_TILE = 256
NA_QROWS = 8
NA_KROWS = 16
MOE_ROWS = 512
INV_BASE_LOG2 = 3
GDN_INTRA_TILE = 256
GDN_SCAN_CHUNKS = 4
FFN_COL_CHUNK = 1024
DIFF_COL_TILE = 512
MOE_TILE = 256
MOE_GBUF_ROWS = MOE_TILE * TOP_K + N_EXPERTS * 8
MOE_RUN_PIECES = tuple(MOE_TILE >> s for s in range(int(math.log2(MOE_TILE)) - 2))
MOE_GAP_PIECES = tuple((MOE_ROWS // 2) >> s for s in range(int(math.log2(MOE_ROWS)) - 3))
VMEM_LIMIT = 56 * 1024 * 1024

C_GQ, C_GK, C_GV, C_GZ = 0, 512, 1024, 1536
C_DQ, C_DK, C_DV = 2048, 2560, 3072
C_NQ, C_NK, C_NV = 3584, 4096, 4608
C_GATE = 5120
C_AB = 8192


def _cparams(sem):
    return pltpu.CompilerParams(dimension_semantics=sem, vmem_limit_bytes=VMEM_LIMIT)


def _ln(x):
    mu = jnp.mean(x, -1, keepdims=True)
    xc = x - mu
    var = jnp.mean(xc * xc, -1, keepdims=True)
    return xc * lax.rsqrt(var + LN_EPS)


def _sigmoid(x):
    return 1.0 / (1.0 + jnp.exp(-x))


def _dot(a, b):
    return jnp.dot(a, b, preferred_element_type=F32)


def _dot_nt(a, b):
    return lax.dot_general(a, b, (((1,), (1,)), ((), ())), preferred_element_type=F32)


def _dot_tn(a, b):
    return lax.dot_general(a, b, (((0,), (0,)), ((), ())), preferred_element_type=F32)


def _mm(a, b):
    return jnp.dot(a.astype(BF16), b.astype(BF16), preferred_element_type=F32)


def _dot_3pass(a, b):
    a_hi = a.astype(BF16)
    b_hi = b.astype(BF16)
    a_lo = (a - a_hi.astype(F32)).astype(BF16)
    b_lo = (b - b_hi.astype(F32)).astype(BF16)
    return _dot(a_hi, b_hi) + _dot(a_hi, b_lo) + _dot(a_lo, b_hi)


def _dot_f32(a, b):
    return jnp.dot(a, b, preferred_element_type=F32, precision=lax.Precision.HIGHEST)


def _ada_kernel(c_ref, w_ref, b_ref, o_ref):
    c = c_ref[...]
    s = (c * _sigmoid(c)).astype(BF16)
    o_ref[0] = _dot(s, w_ref[0].astype(BF16)) + b_ref[0]


def _ada(cmat, w_ada, b_ada):
    depth, d, n = w_ada.shape
    tn = n // 4
    return pl.pallas_call(
        _ada_kernel,
        grid=(depth, n // tn),
        in_specs=[pl.BlockSpec((8, d), lambda l, j: (0, 0)),
                  pl.BlockSpec((1, d, tn), lambda l, j: (l, 0, j)),
                  pl.BlockSpec((1, 1, tn), lambda l, j: (l, 0, j))],
        out_specs=pl.BlockSpec((1, 8, tn), lambda l, j: (l, 0, j)),
        out_shape=jax.ShapeDtypeStruct((depth, 8, n), F32),
        compiler_params=_cparams(("parallel", "parallel")),
        name="ada",
    )(cmat, w_ada, b_ada.reshape(depth, 1, n))


def _modulate_kernel(x_ref, sh_ref, sc_ref, h_ref):
    h_ref[...] = (_ln(x_ref[...]) * (1.0 + sc_ref[...]) + sh_ref[...]).astype(BF16)


def _mod_spec(d, nl, k):
    return pl.BlockSpec((None, None, 1, d), lambda b, i: (b, i // nl, 0, k))


def _modulate(x, nl, modsel, shift_k):
    bsz, lt, d = x.shape
    tok = pl.BlockSpec((None, TOK_TILE, d), lambda b, i: (b, i, 0))
    return pl.pallas_call(
        _modulate_kernel,
        grid=(bsz, lt // TOK_TILE),
        in_specs=[tok, _mod_spec(d, nl, shift_k), _mod_spec(d, nl, shift_k + 1)],
        out_specs=tok,
        out_shape=jax.ShapeDtypeStruct((bsz, lt, d), BF16),
        compiler_params=_cparams(("parallel", "parallel")),
        name="modulate",
    )(x, modsel, modsel)


def _mm_kernel(a_ref, w_ref, o_ref):
    o_ref[...] = _dot(a_ref[...], w_ref[...]).astype(o_ref.dtype)


def _matmul(a, w, layer, tm, tn, out_dtype):
    m, k = a.shape
    n = w.shape[2]
    return pl.pallas_call(
        _mm_kernel,
        grid=(n // tn, m // tm),
        in_specs=[pl.BlockSpec((tm, k), lambda j, i: (i, 0)),
                  pl.BlockSpec((None, k, tn), lambda j, i: (layer, 0, j))],
        out_specs=pl.BlockSpec((tm, tn), lambda j, i: (i, j)),
        out_shape=jax.ShapeDtypeStruct((m, n), out_dtype),
        compiler_params=_cparams(("parallel", "parallel")),
        name="matmul",
    )(a, w)


def _gdn_prep_kernel(x_ref, hp_ref, hn_ref, ab_ref, cw_ref, alog_ref, dtb_ref,
                     qkv_ref, gc_ref, gct_ref, *, nl):
    i = pl.program_id(1)
    x = x_ref[...].astype(F32)
    t = x.shape[0]
    prev_ok = jnp.logical_and(i != 0, i != nl)
    next_ok = jnp.logical_and(i != nl - 1, i != nl)
    halo = hp_ref.shape[0]
    prow = jnp.where(prev_ok, hp_ref[halo - 1:halo, :].astype(F32), 0.0)
    nrow = jnp.where(next_ok, hn_ref[0:1, :].astype(F32), 0.0)
    rid = lax.broadcasted_iota(jnp.int32, x.shape, 0)
    xp = jnp.where(rid == 0, prow, pltpu.roll(x, 1, 0))
    xn = jnp.where(rid == t - 1, nrow, pltpu.roll(x, t - 1, 0))
    w = cw_ref[...]
    y = xp * w[0:1] + x * w[1:2] + xn * w[2:3]
    y = y * _sigmoid(y)
    for g in range(12):
        blk = y[:, g * LANES:(g + 1) * LANES]
        if g < 8:
            blk = blk * lax.rsqrt(jnp.sum(blk * blk, -1, keepdims=True) + RMS_EPS)
        if g < 4:
            blk = blk * (GDN_DK ** -0.5)
        qkv_ref[:, g * LANES:(g + 1) * LANES] = blk

    ab = ab_ref[...]
    lane = lax.broadcasted_iota(jnp.int32, ab.shape, 1)
    xs = ab + dtb_ref[...]
    softplus = jnp.maximum(xs, 0.0) + jnp.log(1.0 + jnp.exp(-jnp.abs(xs)))
    g = -jnp.exp(alog_ref[...]) * softplus
    beta = _sigmoid(ab)
    is_a = (lane % 8) < 4
    gb = jnp.where(lane < 16, jnp.where(is_a, g, beta), 0.0)
    r = lax.broadcasted_iota(jnp.int32, (t, t), 0)
    c = lax.broadcasted_iota(jnp.int32, (t, t), 1)
    same = (r // GDN_CHUNK) == (c // GDN_CHUNK)
    tri_l = jnp.where(jnp.logical_and(same, c <= r), 1.0, 0.0)
    tri_u = jnp.where(jnp.logical_and(same, c >= r), 1.0, 0.0)
    cf = _dot_f32(tri_l, gb)
    cb = _dot_f32(tri_u, gb)
    gc = jnp.where(lane < 4, cf, jnp.where(jnp.logical_and(lane >= 8, lane < 12), cb, gb))
    gc_ref[...] = gc
    gct = gc.T
    for ch in range(t // GDN_CHUNK):
        gct_ref[ch] = gct[0:16, ch * GDN_CHUNK:(ch + 1) * GDN_CHUNK]


def _gdn_prep(p, pab, conv_w, alog_row, dtb_row, nl):
    bsz, lt, _ = p.shape
    nt = lt // TOK_TILE
    w3 = 3 * 512
    halo = 16
    rb = TOK_TILE // halo
    return pl.pallas_call(
        functools.partial(_gdn_prep_kernel, nl=nl),
        grid=(bsz, nt),
        in_specs=[pl.BlockSpec((None, TOK_TILE, w3), lambda b, i: (b, i, 0)),
                  pl.BlockSpec((None, halo, w3), lambda b, i: (b, jnp.maximum(i * rb - 1, 0), 0)),
                  pl.BlockSpec((None, halo, w3), lambda b, i: (b, jnp.minimum(i * rb + rb, lt // halo - 1), 0)),
                  pl.BlockSpec((None, TOK_TILE, LANES), lambda b, i: (b, i, 0)),
                  pl.BlockSpec((3, w3), lambda b, i: (0, 0)),
                  pl.BlockSpec((1, LANES), lambda b, i: (0, 0)),
                  pl.BlockSpec((1, LANES), lambda b, i: (0, 0))],
        out_specs=[pl.BlockSpec((None, TOK_TILE, w3), lambda b, i: (b, i, 0)),
                   pl.BlockSpec((None, TOK_TILE, LANES), lambda b, i: (b, i, 0)),
                   pl.BlockSpec((None, TOK_TILE // GDN_CHUNK, 16, GDN_CHUNK), lambda b, i: (b, i, 0, 0))],
        out_shape=[jax.ShapeDtypeStruct((bsz, lt, w3), F32),
                   jax.ShapeDtypeStruct((bsz, lt, LANES), F32),
                   jax.ShapeDtypeStruct((bsz, lt // GDN_CHUNK, 16, GDN_CHUNK), F32)],
        compiler_params=_cparams(("parallel", "parallel")),
        name="gdn_prep",
    )(p, p, p, pab, conv_w, alog_row, dtb_row)


def _gdn_intra_kernel(qkv_ref, gc_ref, gct_ref, u_ref, w_ref, qg_ref, kt_ref, qk_ref, eg_ref):
    cs = GDN_CHUNK
    nch = qkv_ref.shape[0] // cs
    chains = [(d, c, h) for d in range(2) for c in range(nch) for h in range(GDN_H)]
    ri = lax.broadcasted_iota(jnp.int32, (cs, cs), 0)
    ci = lax.broadcasted_iota(jnp.int32, (cs, cs), 1)
    eye = ri == ci
    base = (ri >> INV_BASE_LOG2) == (ci >> INV_BASE_LOG2)
    incl = (ri >= ci, ri <= ci)
    strict = (ri > ci, ri < ci)
    qkv = qkv_ref[...]
    gc = gc_ref[...]

    def part(col0, c, h):
        return qkv[c * cs:(c + 1) * cs, col0 + h * LANES:col0 + (h + 1) * LANES]

    ch_keys = [(c, h) for c in range(nch) for h in range(GDN_H)]
    q = {key: part(C_GQ, *key) for key in ch_keys}
    k = {key: part(C_GK, *key) for key in ch_keys}
    v = {key: part(C_GV, *key) for key in ch_keys}
    q16 = {key: q[key].astype(BF16) for key in ch_keys}
    k16 = {key: k[key].astype(BF16) for key in ch_keys}
    gcol = [gc[c * cs:(c + 1) * cs, 8 * d + h:8 * d + h + 1] for d, c, h in chains]
    bcol = [gc[c * cs:(c + 1) * cs, 8 * d + 4 + h:8 * d + 5 + h] for d, c, h in chains]
    grow = [gct_ref[c, 8 * d + h:8 * d + h + 1, :] for d, c, h in chains]
    glast = [g[cs - 1:cs, :] if d == 0 else g[0:1, :] for g, (d, c, h) in zip(gcol, chains)]
    decay = [jnp.where(incl[d], jnp.exp(jnp.where(incl[d], gcol[n] - grow[n], 0.0)), 0.0)
             for n, (d, c, h) in enumerate(chains)]
    kb = [k[(c, h)] * bcol[n] for n, (d, c, h) in enumerate(chains)]
    amat = [jnp.where(strict[d], _dot_nt(kb[n].astype(BF16), k16[(c, h)]) * decay[n], 0.0)
            for n, (d, c, h) in enumerate(chains)]
    qk = [_dot_nt(q16[(c, h)], k16[(c, h)]) * decay[n] for n, (d, c, h) in enumerate(chains)]
    eg = [jnp.exp(g) for g in gcol]
    rhs = [jnp.concatenate([v[(c, h)] * bcol[n], kb[n] * eg[n]], axis=1) for n, (d, c, h) in enumerate(chains)]
    rmat = [jnp.where(base, -a, 0.0) for a in amat]
    mmat = rmat
    for _ in range(INV_BASE_LOG2 - 1):
        mmat = [_mm(m, m) for m in mmat]
        prod = [_mm(r, m) for r, m in zip(rmat, mmat)]
        rmat = [r + m + p for r, m, p in zip(rmat, mmat, prod)]
    tmat = [jnp.where(eye, 1.0, r) for r in rmat]
    for lb in range(INV_BASE_LOG2, int(math.log2(cs))):
        same_pair = (ri >> (lb + 1)) == (ci >> (lb + 1))
        off = (jnp.logical_and((ri >> lb) == (ci >> lb) + 1, same_pair),
               jnp.logical_and((ci >> lb) == (ri >> lb) + 1, same_pair))
        inner = [_mm(jnp.where(off[d], amat[n], 0.0), tmat[n]) for n, (d, c, h) in enumerate(chains)]
        outer = [_mm(t, x) for t, x in zip(tmat, inner)]
        tmat = [t - x for t, x in zip(tmat, outer)]
    corr = [_mm(jnp.where(eye, 0.0, t), r) for t, r in zip(tmat, rhs)]
    for n, (d, c, h) in enumerate(chains):
        rows = slice(c * cs, (c + 1) * cs)
        cols = slice(h * LANES, (h + 1) * LANES)
        sol = rhs[n] + corr[n]
        u_ref[d, rows, cols] = sol[:, :LANES]
        w_ref[d, rows, cols] = sol[:, LANES:].astype(BF16)
        qg_ref[d, rows, cols] = (q[(c, h)] * eg[n]).astype(BF16)
        kt_ref[d, rows, cols] = (k[(c, h)] * jnp.exp(glast[n] - gcol[n])).astype(BF16)
        qk_ref[d, c, h] = qk[n].astype(BF16)
        eg_ref[d, c, h:h + 1, :] = jnp.broadcast_to(jnp.exp(glast[n]), (1, LANES))


def _gdn_intra(qkv, gc, gct):
    bsz, lt, w3 = qkv.shape
    cs = GDN_CHUNK
    nc = lt // cs
    nch = GDN_INTRA_TILE // cs

    def tok(dt):
        return (pl.BlockSpec((None, 2, GDN_INTRA_TILE, 512), lambda b, i: (b, 0, i, 0)),
                jax.ShapeDtypeStruct((bsz, 2, lt, 512), dt))

    outs = [tok(F32), tok(BF16), tok(BF16), tok(BF16),
            (pl.BlockSpec((None, 2, nch, GDN_H, cs, cs), lambda b, i: (b, 0, i, 0, 0, 0)),
             jax.ShapeDtypeStruct((bsz, 2, nc, GDN_H, cs, cs), BF16)),
            (pl.BlockSpec((None, 2, nch, GDN_H, LANES), lambda b, i: (b, 0, i, 0, 0)),
             jax.ShapeDtypeStruct((bsz, 2, nc, GDN_H, LANES), F32))]
    return pl.pallas_call(
        _gdn_intra_kernel,
        grid=(bsz, lt // GDN_INTRA_TILE),
        in_specs=[pl.BlockSpec((None, GDN_INTRA_TILE, w3), lambda b, i: (b, i, 0)),
                  pl.BlockSpec((None, GDN_INTRA_TILE, LANES), lambda b, i: (b, i, 0)),
                  pl.BlockSpec((None, nch, 16, cs), lambda b, i: (b, i, 0, 0))],
        out_specs=[o[0] for o in outs],
        out_shape=[o[1] for o in outs],
        compiler_params=_cparams(("parallel", "parallel")),
        name="gdn_intra",
    )(qkv, gc, gct)


def _gdn_scan_kernel(*refs):
    s_ref = refs[-1]
    cs = GDN_CHUNK

    @pl.when(pl.program_id(1) == 0)
    def _():
        s_ref[...] = jnp.zeros_like(s_ref)

    chains = [(d, h) for d in range(2) for h in range(GDN_H)]
    src = (refs[0:6], refs[6:12])
    outs = refs[12:14]
    per_step = outs[0].shape[0] // cs

    def cols(h):
        return slice(h * LANES, (h + 1) * LANES)

    s = [s_ref[d * GDN_H + h] for d, h in chains]
    for sub in range(per_step):
        cidx = (sub, per_step - 1 - sub)
        rows = [slice(c * cs, (c + 1) * cs) for c in cidx]
        wq = [_dot(jnp.concatenate([src[d][1][rows[d], cols(h)], src[d][2][rows[d], cols(h)]], axis=0),
                   s[n].astype(BF16)) for n, (d, h) in enumerate(chains)]
        v16 = [(src[d][0][rows[d], cols(h)] - wq[n][:cs]).astype(BF16) for n, (d, h) in enumerate(chains)]
        o = [wq[n][cs:] + _dot(src[d][4][cidx[d], h], v16[n]) for n, (d, h) in enumerate(chains)]
        s = [s[n] * src[d][5][cidx[d], h:h + 1, :] + _dot_tn(src[d][3][rows[d], cols(h)], v16[n])
             for n, (d, h) in enumerate(chains)]
        for n, (d, h) in enumerate(chains):
            outs[d][rows[d], cols(h)] = o[n]
    for n, (d, h) in enumerate(chains):
        s_ref[d * GDN_H + h] = s[n]


def _gdn_scan(u, w, qg, kt, qk, eg, n_lat_chunks):
    bsz, _, lt, _ = u.shape
    per = GDN_SCAN_CHUNKS
    cs = GDN_CHUNK * per
    nc = lt // cs
    n_lat = n_lat_chunks // per
    ncc = nc - n_lat

    def cf(i):
        return jnp.where(i < ncc, n_lat + i, i - ncc)

    def cb(i):
        return nc - 1 - i

    def specs(d, c):
        tok = pl.BlockSpec((None, None, cs, 512), lambda b, i: (b, d, c(i), 0))
        return [tok, tok, tok, tok,
                pl.BlockSpec((None, None, per, GDN_H, GDN_CHUNK, GDN_CHUNK), lambda b, i: (b, d, c(i), 0, 0, 0)),
                pl.BlockSpec((None, None, per, GDN_H, LANES), lambda b, i: (b, d, c(i), 0, 0))]

    def out(c):
        return pl.BlockSpec((None, cs, 512), lambda b, i: (b, c(i), 0))

    args = (u, w, qg, kt, qk, eg)
    return pl.pallas_call(
        _gdn_scan_kernel,
        grid=(bsz, nc),
        in_specs=specs(0, cf) + specs(1, cb),
        out_specs=[out(cf), out(cb)],
        out_shape=[jax.ShapeDtypeStruct((bsz, lt, 512), F32)] * 2,
        scratch_shapes=[pltpu.VMEM((2 * GDN_H, GDN_DK, LANES), F32)],
        compiler_params=_cparams(("parallel", "arbitrary")),
        name="gdn_scan",
    )(*args, *args)


def _attn_prep_kernel(dq_ref, dk_ref, dv_ref, nq_ref, nk_ref, nv_ref, cos_ref, sin_ref, o_ref):
    cos = cos_ref[...]
    sin = sin_ref[...]
    lane = lax.broadcasted_iota(jnp.int32, cos.shape, 1)
    first = (lane % DIFF_D) < DIFF_D // 2

    def rope(x):
        rot = jnp.where(first, pltpu.roll(x, LANES - DIFF_D // 2, 1), pltpu.roll(x, DIFF_D // 2, 1))
        return x * cos + rot * sin

    for h in range(DIFF_H):
        sl = slice(h * LANES, (h + 1) * LANES)
        o_ref[:, h * LANES:(h + 1) * LANES] = (rope(dq_ref[:, sl].astype(F32))
                                               * (DIFF_D ** -0.5 * LOG2E)).astype(BF16)
        o_ref[:, 512 + h * LANES:512 + (h + 1) * LANES] = rope(dk_ref[:, sl].astype(F32)).astype(BF16)
    o_ref[:, 1024:1536] = dv_ref[...].astype(BF16)
    o_ref[:, 1536:2048] = (nq_ref[...].astype(F32) * (NA_D ** -0.5)).astype(BF16)
    o_ref[:, 2048:2560] = nk_ref[...].astype(BF16)
    o_ref[:, 2560:3072] = nv_ref[...].astype(BF16)


def _attn_prep(p, cos, sin):
    bsz, lt, _ = p.shape

    def col(cb):
        return pl.BlockSpec((None, TOK_TILE, 512), lambda b, i: (b, i, cb))

    tab = pl.BlockSpec((TOK_TILE, LANES), lambda b, i: (i, 0))
    return pl.pallas_call(
        _attn_prep_kernel,
        grid=(bsz, lt // TOK_TILE),
        in_specs=[col(C_DQ // 512), col(C_DK // 512), col(C_DV // 512),
                  col(C_NQ // 512), col(C_NK // 512), col(C_NV // 512), tab, tab],
        out_specs=pl.BlockSpec((None, TOK_TILE, 3072), lambda b, i: (b, i, 0)),
        out_shape=jax.ShapeDtypeStruct((bsz, lt, 3072), BF16),
        compiler_params=_cparams(("parallel", "parallel")),
        name="attn_prep",
    )(p, p, p, p, p, p, cos, sin)


def _diff_kernel(lam_ref, q_ref, k_ref, v_ref, nw_ref, o_ref, m_sc, l_sc, acc_sc, *, nkv, lam_init):
    kv = pl.program_id(3)

    @pl.when(kv == 0)
    def _():
        m_sc[...] = jnp.full_like(m_sc, NEG_INF)
        l_sc[...] = jnp.zeros_like(l_sc)
        acc_sc[...] = jnp.zeros_like(acc_sc)

    q = q_ref[...]
    k = k_ref[...]
    v = v_ref[...]
    maps = range(2)
    tkv = k.shape[0]
    pieces = [(c0, min(c0 + DIFF_COL_TILE, tkv)) for c0 in range(0, tkv, DIFF_COL_TILE)]

    def fold(x, op):
        acc = x[:, 0:LANES]
        for c0 in range(LANES, x.shape[1], LANES):
            acc = op(acc, x[:, c0:c0 + LANES])
        return acc

    s = [[_dot_nt(q[:, m * DIFF_D:(m + 1) * DIFF_D], k[c0:c1, m * DIFF_D:(m + 1) * DIFF_D])
          for c0, c1 in pieces] for m in maps]
    m_new, alpha = [], []
    for m in maps:
        mx = None
        for sp in s[m]:
            part = fold(sp, jnp.maximum)
            mx = part if mx is None else jnp.maximum(mx, part)
        m_prev = m_sc[m]
        m_new.append(jnp.maximum(m_prev, jnp.max(mx, -1, keepdims=True)))
        alpha.append(jnp.exp2(m_prev - m_new[m]))
    for m in maps:
        sm, pv = None, None
        for (c0, c1), sp in zip(pieces, s[m]):
            p = jnp.exp2(sp - m_new[m])
            psum = fold(p, jnp.add)
            sm = psum if sm is None else sm + psum
            part = _dot(p.astype(BF16), v[c0:c1, :])
            pv = part if pv is None else pv + part
        l_sc[m] = alpha[m] * l_sc[m] + jnp.sum(sm, -1, keepdims=True)
        acc_sc[m] = alpha[m] * acc_sc[m] + pv
        m_sc[m] = m_new[m]

    @pl.when(kv == nkv - 1)
    def _():
        lam = lam_ref[0:1, 0:1]
        o = acc_sc[0] / l_sc[0] - lam * (acc_sc[1] / l_sc[1])
        y = o * lax.rsqrt(jnp.mean(o * o, -1, keepdims=True) + RMS_EPS) * nw_ref[...] * (1.0 - lam_init)
        o_ref[...] = y.astype(BF16)


def _diff_attn(a, lam_row, norm_w, lam_init, *, tq, tkv, q0, nq, k0, nkv):
    bsz = a.shape[0]
    return pl.pallas_call(
        functools.partial(_diff_kernel, nkv=nkv, lam_init=lam_init),
        grid=(bsz, DIFF_H, nq, nkv),
        in_specs=[pl.BlockSpec((1, LANES), lambda b, h, i, j: (0, 0)),
                  pl.BlockSpec((None, tq, LANES), lambda b, h, i, j: (b, q0 + i, h)),
                  pl.BlockSpec((None, tkv, LANES), lambda b, h, i, j: (b, k0 + j, 4 + h)),
                  pl.BlockSpec((None, tkv, LANES), lambda b, h, i, j: (b, k0 + j, 8 + h)),
                  pl.BlockSpec((1, LANES), lambda b, h, i, j: (0, 0))],
        out_specs=pl.BlockSpec((None, tq, LANES), lambda b, h, i, j: (b, i, h)),
        out_shape=jax.ShapeDtypeStruct((bsz, nq * tq, 512), BF16),
        scratch_shapes=[pltpu.VMEM((2, tq, 1), F32), pltpu.VMEM((2, tq, 1), F32),
                        pltpu.VMEM((2, tq, LANES), F32)],
        compiler_params=_cparams(("parallel", "parallel", "parallel", "arbitrary")),
        name="diff_attn",
    )(lam_row, a, a, a, norm_w.reshape(1, LANES))


def _na_kernel(q_ref, k_ref, v_ref, bias_ref, o_ref, *, s_lat, lc):
    i = pl.program_id(2)
    tq = q_ref.shape[0]
    nk = NA_KROWS * GRID_W
    kstart = jnp.clip(i * tq - (NA_KROWS - NA_QROWS) // 2 * GRID_W, 0, s_lat - nk)
    kstart = pl.multiple_of(kstart, TOK_TILE)
    q = q_ref[...]
    k_loc = k_ref[pl.ds(kstart, nk), :]
    v_loc = v_ref[pl.ds(kstart, nk), :]
    k_ctx = k_ref[s_lat:s_lat + lc, :]
    v_ctx = v_ref[s_lat:s_lat + lc, :]
    outs = []
    for j in range(2):
        sl = slice(j * NA_D, (j + 1) * NA_D)
        qh = q[:, sl]
        s_loc = _dot_nt(qh, k_loc[:, sl]) + bias_ref[j]
        s_ctx = _dot_nt(qh, k_ctx[:, sl])
        m = jnp.maximum(jnp.max(s_loc, -1, keepdims=True), jnp.max(s_ctx, -1, keepdims=True))
        p_loc = jnp.exp(s_loc - m)
        p_ctx = jnp.exp(s_ctx - m)
        l = jnp.sum(p_loc, -1, keepdims=True) + jnp.sum(p_ctx, -1, keepdims=True)
        o = _dot(p_loc.astype(BF16), v_loc[:, sl]) + _dot(p_ctx.astype(BF16), v_ctx[:, sl])
        outs.append(o / l)
    o_ref[...] = jnp.concatenate(outs, axis=1).astype(BF16)


def _na_attn(a, bias, layer, s_lat, lc):
    bsz, lt, _ = a.shape
    tq = NA_QROWS * GRID_W
    nq = s_lat // tq

    def variant(i):
        return jnp.where(i == 0, 0, jnp.where(i == nq - 1, 2, 1))

    return pl.pallas_call(
        functools.partial(_na_kernel, s_lat=s_lat, lc=lc),
        grid=(bsz, NA_H // 2, nq),
        in_specs=[pl.BlockSpec((None, tq, LANES), lambda b, h, i: (b, i, 12 + h)),
                  pl.BlockSpec((None, lt, LANES), lambda b, h, i: (b, 0, 16 + h)),
                  pl.BlockSpec((None, lt, LANES), lambda b, h, i: (b, 0, 20 + h)),
                  pl.BlockSpec((None, None, 2, tq, NA_KROWS * GRID_W),
                               lambda b, h, i: (layer, variant(i), h, 0, 0))],
        out_specs=pl.BlockSpec((None, tq, LANES), lambda b, h, i: (b, i, h)),
        out_shape=jax.ShapeDtypeStruct((bsz, s_lat, 512), BF16),
        compiler_params=_cparams(("parallel", "parallel", "arbitrary")),
        name="na_attn",
    )(a, a, a, bias)


def _na_ctx_kernel(q_ref, k_ref, v_ref, o_ref):
    q = q_ref[...]
    k = k_ref[...]
    v = v_ref[...]
    outs = []
    for j in range(2):
        sl = slice(j * NA_D, (j + 1) * NA_D)
        s = _dot_nt(q[:, sl], k[:, sl])
        p = jnp.exp(s - jnp.max(s, -1, keepdims=True))
        outs.append(_dot(p.astype(BF16), v[:, sl]) / jnp.sum(p, -1, keepdims=True))
    o_ref[...] = jnp.concatenate(outs, axis=1).astype(BF16)


def _na_ctx_attn(a, s_lat, lc):
    bsz = a.shape[0]
    rb = s_lat // lc

    def blk(c0):
        return pl.BlockSpec((None, lc, LANES), lambda b, h: (b, rb, c0 + h))

    return pl.pallas_call(
        _na_ctx_kernel,
        grid=(bsz, NA_H // 2),
        in_specs=[blk(12), blk(16), blk(20)],
        out_specs=pl.BlockSpec((None, lc, LANES), lambda b, h: (b, 0, h)),
        out_shape=jax.ShapeDtypeStruct((bsz, lc, 512), BF16),
        compiler_params=_cparams(("parallel", "parallel")),
        name="na_ctx_attn",
    )(a, a, a)


def _na_bias_tables(rpb, rows):
    qr = np.arange(NA_QROWS)[:, None]
    kr = np.arange(NA_KROWS)[None, :]
    qc = np.arange(GRID_W)[:, None]
    kc = np.arange(GRID_W)[None, :]
    w0 = np.clip(qc - WIN_W // 2, 0, GRID_W - WIN_W)
    okc = (kc >= w0) & (kc < w0 + WIN_W)
    dc = np.clip(kc - qc + WIN_W - 1, 0, 2 * WIN_W - 2)
    half = (NA_KROWS - NA_QROWS) // 2
    n_dr, n_dc = 2 * WIN_H - 1, 2 * WIN_W - 1
    sel_c = (dc[None] == np.arange(n_dc)[:, None, None]) & okc[None]
    sel_p = np.zeros((3, NA_QROWS, NA_KROWS // 2, n_dr + 1), bool)
    ok = np.zeros((3, NA_QROWS, GRID_W, NA_KROWS, GRID_W), bool)
    for vi, (r_start, delta) in enumerate(((0, 0), (NA_QROWS, -half), (rows - NA_QROWS, -2 * half))):
        r = r_start + qr
        kabs = r_start + delta + kr
        r0 = np.clip(r - WIN_H // 2, 0, rows - WIN_H)
        okr = (kabs >= r0) & (kabs < r0 + WIN_H)
        ok[vi] = okr[:, None, :, None] & okc[None, :, None, :]
        dr_first = (kabs - r + WIN_H - 1)[:, 0::2]
        sel_p[vi] = (dr_first[..., None] + 1) == np.arange(n_dr + 1)
    toep = jnp.einsum('lhaj,jqk->lhaqk', rpb, sel_c.astype(np.float32), precision=lax.Precision.HIGHEST)
    toep = jnp.pad(toep, ((0, 0), (0, 0), (1, 1), (0, 0), (0, 0)))
    toep_pair = jnp.concatenate([toep[:, :, :-1], toep[:, :, 1:]], axis=-1)
    tab = jnp.einsum('vrpa,lhaqw->lvhrqpw', sel_p.astype(np.float32), toep_pair, precision=lax.Precision.HIGHEST)
    tab = tab.reshape(rpb.shape[0], 3, NA_H, NA_QROWS * GRID_W, NA_KROWS * GRID_W)
    ok = ok.reshape(1, 3, 1, NA_QROWS * GRID_W, NA_KROWS * GRID_W)
    return jnp.where(ok, tab, NEG_INF).astype(F32)


def _merge_kernel(of_ref, ob_ref, z_ref, ybl_ref, ybc_ref, ycl_ref, ycc_ref, ga_ref, gb_ref, gc_ref, x_ref,
                  gnw_ref, wb_ref, wo_ref, g1_ref, lg_ref, lb_ref, sh_ref, sc_ref, wr_ref, br_ref,
                  xo_ref, h_ref, lg_out_ref, *, nl):
    is_ctx = pl.program_id(1) >= nl
    yb = jnp.where(is_ctx, ybc_ref[...], ybl_ref[...])
    yc = jnp.where(is_ctx, ycc_ref[...], ycl_ref[...])
    o = of_ref[...] + ob_ref[...]
    z = z_ref[...].astype(F32)
    gnw = gnw_ref[...]
    parts = []
    for h in range(GDN_H):
        sl = slice(h * LANES, (h + 1) * LANES)
        oh = o[:, sl]
        zh = z[:, sl]
        parts.append(oh * lax.rsqrt(jnp.mean(oh * oh, -1, keepdims=True) + RMS_EPS) * gnw * (zh * _sigmoid(zh)))
    ya = jnp.concatenate(parts, axis=1).astype(BF16)
    m = (_sigmoid(ga_ref[...].astype(F32)) * _dot(ya, wb_ref[0])
         + _sigmoid(gb_ref[...].astype(F32)) * _dot(yb, wb_ref[1])
         + _sigmoid(gc_ref[...].astype(F32)) * _dot(yc, wb_ref[2]))
    mx = _dot(m.astype(BF16), wo_ref[...])
    x = _ln(DN_ALPHA * x_ref[...] + g1_ref[...] * mx) * lg_ref[...] + lb_ref[...]
    xo_ref[...] = x
    hh = _ln(x) * (1.0 + sc_ref[...]) + sh_ref[...]
    h_ref[...] = hh.astype(BF16)
    lg_out_ref[...] = _dot_3pass(hh, wr_ref[...]) + br_ref[...]


def _merge(o_f, o_b, p, yb, yb_ctx, yc, yc_ctx, x, gnw, wb, wo, modsel, nl, ln_g, ln_b, wr, br):
    bsz, lt, d = x.shape

    def tok(width, cb):
        return pl.BlockSpec((None, TOK_TILE, width), lambda b, i: (b, i, cb))

    def const(shape):
        return pl.BlockSpec(shape, lambda b, i: (0,) * len(shape))

    lat = pl.BlockSpec((None, TOK_TILE, 512), lambda b, i: (b, jnp.minimum(i, nl - 1), 0))
    cxt = pl.BlockSpec((None, TOK_TILE, 512), lambda b, i: (b, 0, 0))
    return pl.pallas_call(
        functools.partial(_merge_kernel, nl=nl),
        grid=(bsz, lt // TOK_TILE),
        in_specs=[tok(512, 0), tok(512, 0), tok(512, C_GZ // 512), lat, cxt, lat, cxt,
                  tok(d, C_GATE // d), tok(d, C_GATE // d + 1), tok(d, C_GATE // d + 2), tok(d, 0),
                  const((1, LANES)), const((3, 512, d)), const((d, d)),
                  _mod_spec(d, nl, 2), const((1, d)), const((1, d)),
                  _mod_spec(d, nl, 3), _mod_spec(d, nl, 4),
                  const((d, LANES)), const((1, LANES))],
        out_specs=[tok(d, 0), tok(d, 0), tok(LANES, 0)],
        out_shape=[jax.ShapeDtypeStruct((bsz, lt, d), F32),
                   jax.ShapeDtypeStruct((bsz, lt, d), BF16),
                   jax.ShapeDtypeStruct((bsz, lt, LANES), F32)],
        compiler_params=_cparams(("parallel", "parallel")),
        name="merge",
    )(o_f, o_b, p, yb, yb_ctx, yc, yc_ctx, p, p, p, x, gnw.reshape(1, LANES), wb, wo,
      modsel, ln_g.reshape(1, d), ln_b.reshape(1, d), modsel, modsel, wr, br)


def _ffn_kernel(be_ref, nu_ref, x_ref, w1_ref, b1_ref, w2_ref, b2_ref, o_ref, w1c, w2c):
    i = pl.program_id(0)

    @pl.when(i < nu_ref[0])
    def _():
        changed = jnp.logical_or(i == 0, be_ref[i] != be_ref[jnp.maximum(i - 1, 0)])

        @pl.when(changed)
        def _():
            w1c[...] = w1_ref[...].astype(BF16)
            w2c[...] = w2_ref[...].astype(BF16)

        x = x_ref[...].astype(BF16)
        y = None
        for c0 in range(0, D_EXPERT, FFN_COL_CHUNK):
            c1 = c0 + FFN_COL_CHUNK
            hg = _dot(x, w1c[:, c0:c1]) + b1_ref[:, c0:c1]
            hl = _dot(x, w1c[:, D_EXPERT + c0:D_EXPERT + c1]) + b1_ref[:, D_EXPERT + c0:D_EXPERT + c1]
            x_glu = jnp.minimum(hg, SWIGLU_LIMIT)
            x_lin = jnp.clip(hl, -SWIGLU_LIMIT, SWIGLU_LIMIT)
            act = x_glu * _sigmoid(SWIGLU_ALPHA * x_glu) * (x_lin + 1.0)
            part = _dot(act.astype(BF16), w2c[c0:c1, :])
            y = part if y is None else y + part
        o_ref[...] = y + b2_ref[...]

    @pl.when(i >= nu_ref[0])
    def _():
        o_ref[...] = jnp.zeros_like(o_ref)


def _ffn(blk_e, n_used, xs, w1, b1, w2, b2, layer):
    n_rows, d = xs.shape
    n_blk = n_rows // MOE_ROWS
    depth, n_exp, _, de2 = w1.shape
    gs = pltpu.PrefetchScalarGridSpec(
        num_scalar_prefetch=2,
        grid=(n_blk,),
        in_specs=[pl.BlockSpec((MOE_ROWS, d), lambda i, be, nu: (i, 0)),
                  pl.BlockSpec((None, None, d, de2), lambda i, be, nu: (layer, be[i], 0, 0)),
                  pl.BlockSpec((None, None, 1, de2), lambda i, be, nu: (layer, be[i], 0, 0)),
                  pl.BlockSpec((None, None, de2 // 2, d), lambda i, be, nu: (layer, be[i], 0, 0)),
                  pl.BlockSpec((None, None, 1, d), lambda i, be, nu: (layer, be[i], 0, 0))],
        out_specs=pl.BlockSpec((MOE_ROWS, d), lambda i, be, nu: (i, 0)),
        scratch_shapes=[pltpu.VMEM((d, de2), BF16), pltpu.VMEM((de2 // 2, d), BF16)],
    )
    return pl.pallas_call(
        _ffn_kernel,
        grid_spec=gs,
        out_shape=jax.ShapeDtypeStruct((n_rows, d), F32),
        compiler_params=_cparams(("arbitrary",)),
        name="ffn",
    )(blk_e, n_used, xs, w1, b1.reshape(depth, n_exp, 1, de2), w2, b2.reshape(depth, n_exp, 1, d))


def _route_kernel(lg_ref, rc_ref, rt_ref, n8_ref):
    l = lg_ref[...]
    tt = l.shape[0]
    lane = lax.broadcasted_iota(jnp.int32, l.shape, 1)
    picks, vals = [], []
    for _ in range(TOP_K):
        m = jnp.max(l, -1, keepdims=True)
        idx = jnp.min(jnp.where(l == m, lane, LANES), -1, keepdims=True)
        oh = lane == idx
        picks.append(oh)
        vals.append(m)
        l = jnp.where(oh, -jnp.inf, l)
    sel = sum(jnp.where(oh, 1.0, 0.0) for oh in picks)
    cnt = jnp.sum(sel, 0, keepdims=True)
    n8 = jnp.floor((cnt + 7.0) * 0.125) * 8.0
    r = lax.broadcasted_iota(jnp.int32, (LANES, LANES), 0)
    c = lax.broadcasted_iota(jnp.int32, (LANES, LANES), 1)
    off8 = _dot_f32(jnp.broadcast_to(n8, (8, LANES)), jnp.where(r < c, 1.0, 0.0))[0:1]
    tr = lax.broadcasted_iota(jnp.int32, (tt, tt), 0)
    tc = lax.broadcasted_iota(jnp.int32, (tt, tt), 1)
    rank = _dot(jnp.where(tc < tr, 1.0, 0.0).astype(BF16), sel.astype(BF16))
    slot = off8 + rank
    es = [jnp.exp(v - vals[0]) for v in vals]
    den = sum(es)
    rc = jnp.zeros(l.shape, F32)
    for kk in range(TOP_K):
        loc = jnp.sum(jnp.where(picks[kk], slot, 0.0), -1, keepdims=True)
        rc = jnp.where(lane == kk, loc, rc)
        rc = jnp.where(lane == TOP_K + kk, es[kk] / den, rc)
    rc_ref[...] = rc
    rt_ref[...] = rc.T[0:8, :]
    n8_ref[...] = jnp.broadcast_to(n8, (8, LANES))


def _route(logits):
    t = logits.shape[0]
    nt = t // MOE_TILE
    return pl.pallas_call(
        _route_kernel,
        grid=(nt,),
        in_specs=[pl.BlockSpec((MOE_TILE, LANES), lambda i: (i, 0))],
        out_specs=[pl.BlockSpec((MOE_TILE, LANES), lambda i: (i, 0)),
                   pl.BlockSpec((None, 8, MOE_TILE), lambda i: (i, 0, 0)),
                   pl.BlockSpec((None, 8, LANES), lambda i: (i, 0, 0))],
        out_shape=[jax.ShapeDtypeStruct((t, LANES), F32),
                   jax.ShapeDtypeStruct((nt, 8, MOE_TILE), F32),
                   jax.ShapeDtypeStruct((nt, 8, LANES), F32)],
        compiler_params=_cparams(("parallel",)),
        name="moe_route",
    )(logits)


def _run_copies(n, pieces, make_copy, wait):
    done = jnp.int32(0)
    for size in pieces:
        take = (n & size) != 0

        @pl.when(take)
        def _():
            cp = make_copy(done, size)
            if wait:
                cp.wait()
            else:
                cp.start()

        done = done + jnp.where(take, size, 0)


def _tile_runs(gs_ref, n8_ref, tile, make_copy, wait):
    def body(e, local):
        n = n8_ref[tile * N_EXPERTS + e]
        g = gs_ref[tile * N_EXPERTS + e]
        _run_copies(n, MOE_RUN_PIECES,
                    lambda done, size: make_copy(pl.multiple_of(local + done, 8),
                                                 pl.multiple_of(g + done, 8), size), wait)
        return local + n
    lax.fori_loop(0, N_EXPERTS, body, jnp.int32(0))


def _dispatch_kernel(gs_ref, n8_ref, go_ref, gn_ref, h_ref, rt_ref, xs_ref, gbuf, zbuf, sems):
    i = pl.program_id(0)
    slot = i % 2
    sem = sems.at[0]

    @pl.when(i == 0)
    def _():
        zbuf[...] = jnp.zeros_like(zbuf)
        zrows = zbuf.shape[0]
        for wait in (False, True):
            def body(e, carry):
                _run_copies(gn_ref[e], MOE_GAP_PIECES,
                            lambda done, size: pltpu.make_async_copy(
                                zbuf.at[pl.ds(0, size), :],
                                xs_ref.at[pl.ds(pl.multiple_of(go_ref[e] + done, 8), size), :], sem), wait)
                return carry
            lax.fori_loop(0, N_EXPERTS, body, 0)

            def tail(j, carry):
                cp = pltpu.make_async_copy(
                    zbuf, xs_ref.at[pl.ds(pl.multiple_of(go_ref[N_EXPERTS] + j * zrows, 8), zrows), :], sem)
                if wait:
                    cp.wait()
                else:
                    cp.start()
                return carry
            lax.fori_loop(0, gn_ref[N_EXPERTS], tail, 0)

    loc = rt_ref[0:TOP_K, :].astype(jnp.int32)
    row = lax.broadcasted_iota(jnp.int32, (gbuf.shape[1], loc.shape[1]), 0)
    hit = row == loc[0:1, :]
    for kk in range(1, TOP_K):
        hit = jnp.logical_or(hit, row == loc[kk:kk + 1, :])
    gbuf[slot] = _dot(jnp.where(hit, 1.0, 0.0).astype(BF16), h_ref[...])

    def runs(tile, sl, wait):
        _tile_runs(gs_ref, n8_ref, tile, lambda local, g, size: pltpu.make_async_copy(
            gbuf.at[sl, pl.ds(local, size), :], xs_ref.at[pl.ds(g, size), :], sems.at[sl]), wait)

    runs(i, slot, False)

    @pl.when(i > 0)
    def _():
        runs(i - 1, 1 - slot, True)

    @pl.when(i == pl.num_programs(0) - 1)
    def _():
        runs(i, slot, True)


def _dispatch(gstart, n8, gap_off, gap_n, h2, rt, n_rows):
    t, d = h2.shape
    nt = t // MOE_TILE
    gs = pltpu.PrefetchScalarGridSpec(
        num_scalar_prefetch=4,
        grid=(nt,),
        in_specs=[pl.BlockSpec((MOE_TILE, d), lambda i, *_: (i, 0)),
                  pl.BlockSpec((None, 8, MOE_TILE), lambda i, *_: (i, 0, 0))],
        out_specs=pl.BlockSpec(memory_space=pl.ANY),
        scratch_shapes=[pltpu.VMEM((2, MOE_GBUF_ROWS, d), F32), pltpu.VMEM((MOE_ROWS // 2, d), F32),
                        pltpu.SemaphoreType.DMA((2,))],
    )
    return pl.pallas_call(
        _dispatch_kernel,
        grid_spec=gs,
        out_shape=jax.ShapeDtypeStruct((n_rows, d), F32),
        compiler_params=_cparams(("arbitrary",)),
        name="moe_dispatch",
    )(gstart, n8, gap_off, gap_n, h2, rt)


def _combine_kernel(gs_ref, n8_ref, ys_ref, rc_ref, x_ref, g_ref, lg_ref, lb_ref, *refs, has_h):
    if has_h:
        sh_ref, sc_ref, xo_ref, h_ref, ybuf, sems = refs
    else:
        xo_ref, ybuf, sems = refs
    i = pl.program_id(0)
    slot = i % 2

    def runs(tile, sl, wait):
        _tile_runs(gs_ref, n8_ref, tile, lambda local, g, size: pltpu.make_async_copy(
            ys_ref.at[pl.ds(g, size), :], ybuf.at[sl, pl.ds(local, size), :], sems.at[sl]), wait)

    @pl.when(i == 0)
    def _():
        ybuf[...] = jnp.zeros_like(ybuf)
        runs(i, slot, False)

    runs(i, slot, True)

    @pl.when(i + 1 < pl.num_programs(0))
    def _():
        runs(i + 1, 1 - slot, False)

    rc = rc_ref[...]
    col = lax.broadcasted_iota(jnp.int32, (rc.shape[0], ybuf.shape[1]), 1)
    wgt = jnp.zeros(col.shape, F32)
    for kk in range(TOP_K):
        wgt = jnp.where(col == rc[:, kk:kk + 1].astype(jnp.int32), rc[:, TOP_K + kk:TOP_K + kk + 1], wgt)
    y = ybuf[slot]
    w_hi = wgt.astype(BF16)
    w_lo = (wgt - w_hi.astype(F32)).astype(BF16)
    y_hi = y.astype(BF16)
    y_lo = (y - y_hi.astype(F32)).astype(BF16)
    f = _dot(w_hi, y_hi) + _dot(w_hi, y_lo) + _dot(w_lo, y_hi)
    x = _ln(DN_ALPHA * x_ref[...] + g_ref[...] * f) * lg_ref[...] + lb_ref[...]
    xo_ref[...] = x
    if has_h:
        h_ref[...] = (_ln(x) * (1.0 + sc_ref[...]) + sh_ref[...]).astype(BF16)


def _combine(gstart, n8, ys, rc, x, nl, tiles_per_batch, gate, ln_g, ln_b, shift=None):
    t, d = x.shape
    nt = t // MOE_TILE
    has_h = shift is not None

    def mod(k):
        return pl.BlockSpec((None, None, 1, d),
                            lambda i, *_: (i // tiles_per_batch, (i % tiles_per_batch) // nl, 0, k))

    tok = pl.BlockSpec((MOE_TILE, d), lambda i, *_: (i, 0))
    vec = pl.BlockSpec((1, d), lambda i, *_: (0, 0))
    in_specs = [pl.BlockSpec(memory_space=pl.ANY), pl.BlockSpec((MOE_TILE, LANES), lambda i, *_: (i, 0)),
                tok, mod(gate[1]), vec, vec]
    args = [ys, rc, x, gate[0], ln_g.reshape(1, d), ln_b.reshape(1, d)]
    out_specs, out_shape = [tok], [jax.ShapeDtypeStruct((t, d), F32)]
    if has_h:
        in_specs += [mod(shift[1]), mod(shift[1] + 1)]
        args += [shift[0], shift[0]]
        out_specs.append(tok)
        out_shape.append(jax.ShapeDtypeStruct((t, d), BF16))
    gs = pltpu.PrefetchScalarGridSpec(
        num_scalar_prefetch=2,
        grid=(nt,),
        in_specs=in_specs,
        out_specs=out_specs,
        scratch_shapes=[pltpu.VMEM((2, MOE_GBUF_ROWS, d), F32), pltpu.SemaphoreType.DMA((2,))],
    )
    return pl.pallas_call(
        functools.partial(_combine_kernel, has_h=has_h),
        grid_spec=gs,
        out_shape=out_shape,
        compiler_params=_cparams(("arbitrary",)),
        name="moe_combine",
    )(gstart, n8, *args)


def _moe(h2, logits, w1, b1, w2, b2, layer, **post):
    t, d = h2.shape
    nt = t // MOE_TILE
    rc, rt, n8f = _route(logits)
    n8 = n8f[:, 0, :N_EXPERTS].astype(jnp.int32)
    e_rows = jnp.sum(n8, 0)
    e_pad = (e_rows + MOE_ROWS - 1) // MOE_ROWS * MOE_ROWS
    e_end = jnp.cumsum(e_pad)
    e_start = e_end - e_pad
    gstart = (e_start[None, :] + jnp.cumsum(n8, 0) - n8).reshape(-1)
    n_rows = -(-(t * TOP_K + nt * N_EXPERTS * 7 + N_EXPERTS * (MOE_ROWS - 1)) // MOE_ROWS) * MOE_ROWS
    n_used = (e_end[-1:] // MOE_ROWS).astype(jnp.int32)
    blk = jnp.minimum(jnp.arange(n_rows // MOE_ROWS, dtype=jnp.int32), n_used - 1) * MOE_ROWS
    blk_e = jnp.minimum(jnp.sum(blk[:, None] >= e_end[None, :], axis=1), N_EXPERTS - 1).astype(jnp.int32)
    n8_flat = n8.reshape(-1)
    gap_off = jnp.concatenate([e_start + e_rows, e_end[-1:]]).astype(jnp.int32)
    gap_n = jnp.concatenate([e_pad - e_rows, (n_rows - e_end[-1:]) // (MOE_ROWS // 2)]).astype(jnp.int32)
    xs = _dispatch(gstart.astype(jnp.int32), n8_flat, gap_off, gap_n, h2, rt, n_rows)
    ys = _ffn(blk_e, n_used, xs, w1, b1, w2, b2, layer)
    return _combine(gstart, n8_flat, ys, rc, **post)


def _rope_tables(s_lat, lc):
    t = jnp.arange(s_lat)
    row = (t // GRID_W).astype(F32)
    col = (t % GRID_W).astype(F32)
    n_freq = DIFF_D // 4
    inv = ROPE_BASE ** (-jnp.arange(n_freq, dtype=F32) / n_freq)
    ang = jnp.concatenate([row[:, None] * inv, col[:, None] * inv], -1)
    ang = jnp.concatenate([ang, ang, ang, ang], -1)
    sign = jnp.where((jnp.arange(LANES) % DIFF_D) < DIFF_D // 2, -1.0, 1.0)
    cos = jnp.concatenate([jnp.cos(ang), jnp.ones((lc, LANES), F32)], 0)
    sin = jnp.concatenate([jnp.sin(ang) * sign, jnp.zeros((lc, LANES), F32)], 0)
    return cos.astype(F32), sin.astype(F32)


def kernel(x, c, ctx, c_ctx, w_ada, b_ada, w_in, conv_w, gdn_a_log, gdn_dt_bias, gdn_norm_w, diff_lambda, diff_norm_w, na_rpb, w_branch, w_out, ln_g, ln_b, w_router, b_router, w_exp1, b_exp1, w_exp2, b_exp2):
    bsz, s_lat, d = x.shape
    lc = ctx.shape[1]
    depth = w_ada.shape[0]
    assert lc == TOK_TILE == MOE_TILE and s_lat % (NA_QROWS * GRID_W) == 0 and d == 1024 and depth == 4
    lt = s_lat + lc
    nl = s_lat // TOK_TILE
    rows = s_lat // GRID_W
    assert rows >= 3 * NA_QROWS

    xs = jnp.concatenate([x, ctx], axis=1)
    cmat = jnp.zeros((8, d), F32).at[:bsz].set(c).at[bsz].set(c_ctx)
    mod = _ada(cmat, w_ada, b_ada)
    cos, sin = _rope_tables(s_lat, lc)

    w_main = jnp.concatenate([w_in[:, :, :2048], w_in[:, :, 2064:]], axis=2).astype(BF16)
    w_ab = jnp.pad(w_in[:, :, 2048:2064], ((0, 0), (0, 0), (0, LANES - 16))).astype(BF16)
    wb16 = w_branch.astype(BF16)
    wo16 = w_out.astype(BF16)
    wr_pad = jnp.pad(w_router, ((0, 0), (0, 0), (0, LANES - N_EXPERTS)))
    br_pad = jnp.pad(b_router, ((0, 0), (0, LANES - N_EXPERTS)), constant_values=NEG_INF).reshape(depth, 1, LANES)
    lane_pad = LANES - 4 * GDN_H
    na_bias = _na_bias_tables(na_rpb, rows)

    modsels = [jnp.stack([mod[l, :bsz], jnp.broadcast_to(mod[l, bsz], (bsz, 6 * d))], axis=1)[:, :, None, :]
               for l in range(depth)]
    h1 = _modulate(xs, nl, modsels[0], 0)
    for l in range(depth):
        lam_init = 0.8 - 0.6 * math.exp(-0.3 * l)
        modsel = modsels[l]
        tm = 512 if (bsz * lt) % 512 == 0 else TOK_TILE
        h1f = h1.reshape(bsz * lt, d)
        p = _matmul(h1f, w_main, l, tm, C_AB // 4, BF16).reshape(bsz, lt, C_AB)
        pab = _matmul(h1f, w_ab, l, tm, LANES, F32).reshape(bsz, lt, LANES)

        def head_row(v2):
            z4 = jnp.zeros((GDN_H,), F32)
            return jnp.pad(jnp.concatenate([v2[0], z4, v2[1], z4]), (0, lane_pad)).reshape(1, LANES)
        qkv_n, gcs, gct = _gdn_prep(p, pab, conv_w[l], head_row(gdn_a_log[l]), head_row(gdn_dt_bias[l]), nl)
        o_f, o_b = _gdn_scan(*_gdn_intra(qkv_n, gcs, gct), s_lat // GDN_CHUNK)

        a = _attn_prep(p, cos, sin)
        lv = diff_lambda[l].astype(F32)
        lam = jnp.exp(jnp.sum(lv[0] * lv[1])) - jnp.exp(jnp.sum(lv[2] * lv[3])) + lam_init
        lam_row = jnp.full((1, LANES), lam, F32)
        tq = min(1024, s_lat)
        tkv = lt // 3 if (lt // 3) % TOK_TILE == 0 else TOK_TILE
        yb = _diff_attn(a, lam_row, diff_norm_w[l], lam_init, tq=tq, tkv=tkv, q0=0, nq=s_lat // tq,
                        k0=0, nkv=lt // tkv)
        yb_ctx = _diff_attn(a, lam_row, diff_norm_w[l], lam_init, tq=lc, tkv=lc, q0=s_lat // lc, nq=1,
                            k0=s_lat // lc, nkv=1)
        yc = _na_attn(a, na_bias, l, s_lat, lc)
        yc_ctx = _na_ctx_attn(a, s_lat, lc)

        xs, h2, logits = _merge(o_f, o_b, p, yb, yb_ctx, yc, yc_ctx, xs, gdn_norm_w[l], wb16[l], wo16[l],
                                modsel, nl, ln_g[l, 0], ln_b[l, 0], wr_pad[l], br_pad[l])
        res = _moe(h2.reshape(bsz * lt, d), logits.reshape(bsz * lt, LANES), w_exp1, b_exp1, w_exp2, b_exp2, l,
                   x=xs.reshape(bsz * lt, d), nl=nl, tiles_per_batch=lt // MOE_TILE, gate=(modsel, 5),
                   ln_g=ln_g[l, 1], ln_b=ln_b[l, 1], shift=(modsels[l + 1], 0) if l + 1 < depth else None)
        xs = res[0].reshape(bsz, lt, d)
        if l + 1 < depth:
            h1 = res[1].reshape(bsz, lt, d)
    return xs[:, :s_lat]
```

```python
import functools
import math

import numpy as np
import jax
import jax.numpy as jnp
from jax import lax
from jax.experimental import pallas as pl
from jax.experimental.pallas import tpu as pltpu

F32 = jnp.float32
BF16 = jnp.bfloat16

GRID_W = 64
GDN_H = 4
GDN_DK = 128
GDN_CHUNK = 64
DIFF_H = 4
DIFF_D = 64
ROPE_BASE = 10000.0
NA_H = 8
NA_D = 64
WIN_H = 8
WIN_W = 16
N_EXPERTS = 32
TOP_K = 4
D_EXPERT = 1024
SWIGLU_LIMIT = 7.0
SWIGLU_ALPHA = 1.702
DN_ALPHA = 8.0 ** 0.25
LN_EPS = 1e-5
RMS_EPS = 1e-6
NEG_INF = -1e30
LOG2E = math.log2(math.e)

LANES = 128
TOK_TILE = 256
NA_QROWS = 8
NA_KROWS = 16
MOE_ROWS = 512
INV_BASE_LOG2 = 3
GDN_INTRA_TILE = 256
GDN_SCAN_CHUNKS = 4
FFN_COL_CHUNK = 1024
DIFF_COL_TILE = 512
DIFF_ROW_BLOCK = 512
MOE_TILE = 256
MOE_GBUF_ROWS = MOE_TILE * TOP_K + N_EXPERTS * 8
MOE_RUN_PIECES = tuple(MOE_TILE >> s for s in range(int(math.log2(MOE_TILE)) - 2))
MOE_GAP_PIECES = tuple((MOE_ROWS // 2) >> s for s in range(int(math.log2(MOE_ROWS)) - 3))
VMEM_LIMIT = 56 * 1024 * 1024

C_GQ, C_GK, C_GV, C_GZ = 0, 512, 1024, 1536
C_DQ, C_DK, C_DV = 2048, 2560, 3072
C_NQ, C_NK, C_NV = 3584, 4096, 4608
C_GATE = 5120
C_AB = 8192


def _cparams(sem):
    return pltpu.CompilerParams(dimension_semantics=sem, vmem_limit_bytes=VMEM_LIMIT)


def _ln(x):
    mu = jnp.mean(x, -1, keepdims=True)
    xc = x - mu
    var = jnp.mean(xc * xc, -1, keepdims=True)
    return xc * lax.rsqrt(var + LN_EPS)


def _sigmoid(x):
    return 1.0 / (1.0 + jnp.exp(-x))


def _dot(a, b):
    return jnp.dot(a, b, preferred_element_type=F32)


def _dot_nt(a, b):
    return lax.dot_general(a, b, (((1,), (1,)), ((), ())), preferred_element_type=F32)


def _dot_tn(a, b):
    return lax.dot_general(a, b, (((0,), (0,)), ((), ())), preferred_element_type=F32)


def _mm(a, b):
    return jnp.dot(a.astype(BF16), b.astype(BF16), preferred_element_type=F32)


def _dot_3pass(a, b):
    a_hi = a.astype(BF16)
    b_hi = b.astype(BF16)
    a_lo = (a - a_hi.astype(F32)).astype(BF16)
    b_lo = (b - b_hi.astype(F32)).astype(BF16)
    return _dot(a_hi, b_hi) + _dot(a_hi, b_lo) + _dot(a_lo, b_hi)


def _dot_f32(a, b):
    return jnp.dot(a, b, preferred_element_type=F32, precision=lax.Precision.HIGHEST)


def _ada_kernel(c_ref, w_ref, b_ref, o_ref):
    c = c_ref[...]
    s = (c * _sigmoid(c)).astype(BF16)
    o_ref[0] = _dot(s, w_ref[0].astype(BF16)) + b_ref[0]


def _ada(cmat, w_ada, b_ada):
    depth, d, n = w_ada.shape
    tn = n // 4
    return pl.pallas_call(
        _ada_kernel,
        grid=(depth, n // tn),
        in_specs=[pl.BlockSpec((8, d), lambda l, j: (0, 0)),
                  pl.BlockSpec((1, d, tn), lambda l, j: (l, 0, j)),
                  pl.BlockSpec((1, 1, tn), lambda l, j: (l, 0, j))],
        out_specs=pl.BlockSpec((1, 8, tn), lambda l, j: (l, 0, j)),
        out_shape=jax.ShapeDtypeStruct((depth, 8, n), F32),
        compiler_params=_cparams(("parallel", "parallel")),
        name="ada",
    )(cmat, w_ada, b_ada.reshape(depth, 1, n))


def _modulate_kernel(x_ref, sh_ref, sc_ref, h_ref):
    h_ref[...] = (_ln(x_ref[...]) * (1.0 + sc_ref[...]) + sh_ref[...]).astype(BF16)


def _mod_spec(d, nl, k):
    return pl.BlockSpec((None, None, 1, d), lambda b, i: (b, i // nl, 0, k))


def _modulate(x, nl, modsel, shift_k):
    bsz, lt, d = x.shape
    tok = pl.BlockSpec((None, TOK_TILE, d), lambda b, i: (b, i, 0))
    return pl.pallas_call(
        _modulate_kernel,
        grid=(bsz, lt // TOK_TILE),
        in_specs=[tok, _mod_spec(d, nl, shift_k), _mod_spec(d, nl, shift_k + 1)],
        out_specs=tok,
        out_shape=jax.ShapeDtypeStruct((bsz, lt, d), BF16),
        compiler_params=_cparams(("parallel", "parallel")),
        name="modulate",
    )(x, modsel, modsel)


def _mm_kernel(a_ref, w_ref, o_ref):
    o_ref[...] = _dot(a_ref[...], w_ref[...]).astype(o_ref.dtype)


def _matmul(a, w, layer, tm, tn, out_dtype):
    m, k = a.shape
    n = w.shape[2]
    return pl.pallas_call(
        _mm_kernel,
        grid=(n // tn, m // tm),
        in_specs=[pl.BlockSpec((tm, k), lambda j, i: (i, 0)),
                  pl.BlockSpec((None, k, tn), lambda j, i: (layer, 0, j))],
        out_specs=pl.BlockSpec((tm, tn), lambda j, i: (i, j)),
        out_shape=jax.ShapeDtypeStruct((m, n), out_dtype),
        compiler_params=_cparams(("parallel", "parallel")),
        name="matmul",
    )(a, w)


def _gdn_prep_kernel(x_ref, hp_ref, hn_ref, ab_ref, cw_ref, alog_ref, dtb_ref,
                     qkv_ref, gc_ref, gct_ref, *, nl):
    i = pl.program_id(1)
    x = x_ref[...].astype(F32)
    t = x.shape[0]
    prev_ok = jnp.logical_and(i != 0, i != nl)
    next_ok = jnp.logical_and(i != nl - 1, i != nl)
    halo = hp_ref.shape[0]
    prow = jnp.where(prev_ok, hp_ref[halo - 1:halo, :].astype(F32), 0.0)
    nrow = jnp.where(next_ok, hn_ref[0:1, :].astype(F32), 0.0)
    rid = lax.broadcasted_iota(jnp.int32, x.shape, 0)
    xp = jnp.where(rid == 0, prow, pltpu.roll(x, 1, 0))
    xn = jnp.where(rid == t - 1, nrow, pltpu.roll(x, t - 1, 0))
    w = cw_ref[...]
    y = xp * w[0:1] + x * w[1:2] + xn * w[2:3]
    y = y * _sigmoid(y)
    for g in range(12):
        blk = y[:, g * LANES:(g + 1) * LANES]
        if g < 8:
            blk = blk * lax.rsqrt(jnp.sum(blk * blk, -1, keepdims=True) + RMS_EPS)
        if g < 4:
            blk = blk * (GDN_DK ** -0.5)
        qkv_ref[:, g * LANES:(g + 1) * LANES] = blk

    ab = ab_ref[...]
    lane = lax.broadcasted_iota(jnp.int32, ab.shape, 1)
    xs = ab + dtb_ref[...]
    softplus = jnp.maximum(xs, 0.0) + jnp.log(1.0 + jnp.exp(-jnp.abs(xs)))
    g = -jnp.exp(alog_ref[...]) * softplus
    beta = _sigmoid(ab)
    is_a = (lane % 8) < 4
    gb = jnp.where(lane < 16, jnp.where(is_a, g, beta), 0.0)
    r = lax.broadcasted_iota(jnp.int32, (t, t), 0)
    c = lax.broadcasted_iota(jnp.int32, (t, t), 1)
    same = (r // GDN_CHUNK) == (c // GDN_CHUNK)
    tri_l = jnp.where(jnp.logical_and(same, c <= r), 1.0, 0.0)
    tri_u = jnp.where(jnp.logical_and(same, c >= r), 1.0, 0.0)
    cf = _dot_f32(tri_l, gb)
    cb = _dot_f32(tri_u, gb)
    gc = jnp.where(lane < 4, cf, jnp.where(jnp.logical_and(lane >= 8, lane < 12), cb, gb))
    gc_ref[...] = gc
    gct = gc.T
    for ch in range(t // GDN_CHUNK):
        gct_ref[ch] = gct[0:16, ch * GDN_CHUNK:(ch + 1) * GDN_CHUNK]


def _gdn_prep(p, pab, conv_w, alog_row, dtb_row, nl):
    bsz, lt, _ = p.shape
    nt = lt // TOK_TILE
    w3 = 3 * 512
    halo = 16
    rb = TOK_TILE // halo
    return pl.pallas_call(
        functools.partial(_gdn_prep_kernel, nl=nl),
        grid=(bsz, nt),
        in_specs=[pl.BlockSpec((None, TOK_TILE, w3), lambda b, i: (b, i, 0)),
                  pl.BlockSpec((None, halo, w3), lambda b, i: (b, jnp.maximum(i * rb - 1, 0), 0)),
                  pl.BlockSpec((None, halo, w3), lambda b, i: (b, jnp.minimum(i * rb + rb, lt // halo - 1), 0)),
                  pl.BlockSpec((None, TOK_TILE, LANES), lambda b, i: (b, i, 0)),
                  pl.BlockSpec((3, w3), lambda b, i: (0, 0)),
                  pl.BlockSpec((1, LANES), lambda b, i: (0, 0)),
                  pl.BlockSpec((1, LANES), lambda b, i: (0, 0))],
        out_specs=[pl.BlockSpec((None, TOK_TILE, w3), lambda b, i: (b, i, 0)),
                   pl.BlockSpec((None, TOK_TILE, LANES), lambda b, i: (b, i, 0)),
                   pl.BlockSpec((None, TOK_TILE // GDN_CHUNK, 16, GDN_CHUNK), lambda b, i: (b, i, 0, 0))],
        out_shape=[jax.ShapeDtypeStruct((bsz, lt, w3), F32),
                   jax.ShapeDtypeStruct((bsz, lt, LANES), F32),
                   jax.ShapeDtypeStruct((bsz, lt // GDN_CHUNK, 16, GDN_CHUNK), F32)],
        compiler_params=_cparams(("parallel", "parallel")),
        name="gdn_prep",
    )(p, p, p, pab, conv_w, alog_row, dtb_row)


def _gdn_intra_kernel(qkv_ref, gc_ref, gct_ref, u_ref, w_ref, qg_ref, kt_ref, qk_ref, eg_ref):
    cs = GDN_CHUNK
    nch = qkv_ref.shape[0] // cs
    chains = [(d, c, h) for d in range(2) for c in range(nch) for h in range(GDN_H)]
    ri = lax.broadcasted_iota(jnp.int32, (cs, cs), 0)
    ci = lax.broadcasted_iota(jnp.int32, (cs, cs), 1)
    eye = ri == ci
    base = (ri >> INV_BASE_LOG2) == (ci >> INV_BASE_LOG2)
    incl = (ri >= ci, ri <= ci)
    strict = (ri > ci, ri < ci)
    qkv = qkv_ref[...]
    gc = gc_ref[...]

    def part(col0, c, h):
        return qkv[c * cs:(c + 1) * cs, col0 + h * LANES:col0 + (h + 1) * LANES]

    ch_keys = [(c, h) for c in range(nch) for h in range(GDN_H)]
    q = {key: part(C_GQ, *key) for key in ch_keys}
    k = {key: part(C_GK, *key) for key in ch_keys}
    v = {key: part(C_GV, *key) for key in ch_keys}
    q16 = {key: q[key].astype(BF16) for key in ch_keys}
    k16 = {key: k[key].astype(BF16) for key in ch_keys}
    gcol = [gc[c * cs:(c + 1) * cs, 8 * d + h:8 * d + h + 1] for d, c, h in chains]
    bcol = [gc[c * cs:(c + 1) * cs, 8 * d + 4 + h:8 * d + 5 + h] for d, c, h in chains]
    grow = [gct_ref[c, 8 * d + h:8 * d + h + 1, :] for d, c, h in chains]
    glast = [g[cs - 1:cs, :] if d == 0 else g[0:1, :] for g, (d, c, h) in zip(gcol, chains)]
    decay = [jnp.where(incl[d], jnp.exp(jnp.where(incl[d], gcol[n] - grow[n], 0.0)), 0.0)
             for n, (d, c, h) in enumerate(chains)]
    kb = [k[(c, h)] * bcol[n] for n, (d, c, h) in enumerate(chains)]
    amat = [jnp.where(strict[d], _dot_nt(kb[n].astype(BF16), k16[(c, h)]) * decay[n], 0.0)
            for n, (d, c, h) in enumerate(chains)]
    qk = [_dot_nt(q16[(c, h)], k16[(c, h)]) * decay[n] for n, (d, c, h) in enumerate(chains)]
    eg = [jnp.exp(g) for g in gcol]
    rhs = [jnp.concatenate([v[(c, h)] * bcol[n], kb[n] * eg[n]], axis=1) for n, (d, c, h) in enumerate(chains)]
    rmat = [jnp.where(base, -a, 0.0) for a in amat]
    mmat = rmat
    for _ in range(INV_BASE_LOG2 - 1):
        mmat = [_mm(m, m) for m in mmat]
        prod = [_mm(r, m) for r, m in zip(rmat, mmat)]
        rmat = [r + m + p for r, m, p in zip(rmat, mmat, prod)]
    tmat = [jnp.where(eye, 1.0, r) for r in rmat]
    for lb in range(INV_BASE_LOG2, int(math.log2(cs))):
        same_pair = (ri >> (lb + 1)) == (ci >> (lb + 1))
        off = (jnp.logical_and((ri >> lb) == (ci >> lb) + 1, same_pair),
               jnp.logical_and((ci >> lb) == (ri >> lb) + 1, same_pair))
        inner = [_mm(jnp.where(off[d], amat[n], 0.0), tmat[n]) for n, (d, c, h) in enumerate(chains)]
        outer = [_mm(t, x) for t, x in zip(tmat, inner)]
        tmat = [t - x for t, x in zip(tmat, outer)]
    corr = [_mm(jnp.where(eye, 0.0, t), r) for t, r in zip(tmat, rhs)]
    for n, (d, c, h) in enumerate(chains):
        rows = slice(c * cs, (c + 1) * cs)
        cols = slice(h * LANES, (h + 1) * LANES)
        sol = rhs[n] + corr[n]
        u_ref[d, rows, cols] = sol[:, :LANES]
        w_ref[d, rows, cols] = sol[:, LANES:].astype(BF16)
        qg_ref[d, rows, cols] = (q[(c, h)] * eg[n]).astype(BF16)
        kt_ref[d, rows, cols] = (k[(c, h)] * jnp.exp(glast[n] - gcol[n])).astype(BF16)
        qk_ref[d, c, h] = qk[n].astype(BF16)
        eg_ref[d, c, h:h + 1, :] = jnp.broadcast_to(jnp.exp(glast[n]), (1, LANES))


def _gdn_intra(qkv, gc, gct):
    bsz, lt, w3 = qkv.shape
    cs = GDN_CHUNK
    nc = lt // cs
    nch = GDN_INTRA_TILE // cs

    def tok(dt):
        return (pl.BlockSpec((None, 2, GDN_INTRA_TILE, 512), lambda b, i: (b, 0, i, 0)),
                jax.ShapeDtypeStruct((bsz, 2, lt, 512), dt))

    outs = [tok(F32), tok(BF16), tok(BF16), tok(BF16),
            (pl.BlockSpec((None, 2, nch, GDN_H, cs, cs), lambda b, i: (b, 0, i, 0, 0, 0)),
             jax.ShapeDtypeStruct((bsz, 2, nc, GDN_H, cs, cs), BF16)),
            (pl.BlockSpec((None, 2, nch, GDN_H, LANES), lambda b, i: (b, 0, i, 0, 0)),
             jax.ShapeDtypeStruct((bsz, 2, nc, GDN_H, LANES), F32))]
    return pl.pallas_call(
        _gdn_intra_kernel,
        grid=(bsz, lt // GDN_INTRA_TILE),
        in_specs=[pl.BlockSpec((None, GDN_INTRA_TILE, w3), lambda b, i: (b, i, 0)),
                  pl.BlockSpec((None, GDN_INTRA_TILE, LANES), lambda b, i: (b, i, 0)),
                  pl.BlockSpec((None, nch, 16, cs), lambda b, i: (b, i, 0, 0))],
        out_specs=[o[0] for o in outs],
        out_shape=[o[1] for o in outs],
        compiler_params=_cparams(("parallel", "parallel")),
        name="gdn_intra",
    )(qkv, gc, gct)


def _gdn_scan_kernel(*refs):
    s_ref = refs[-1]
    cs = GDN_CHUNK

    @pl.when(pl.program_id(1) == 0)
    def _():
        s_ref[...] = jnp.zeros_like(s_ref)

    chains = [(d, h) for d in range(2) for h in range(GDN_H)]
    src = (refs[0:6], refs[6:12])
    outs = refs[12:14]
    per_step = outs[0].shape[0] // cs

    def cols(h):
        return slice(h * LANES, (h + 1) * LANES)

    s = [s_ref[d * GDN_H + h] for d, h in chains]
    for sub in range(per_step):
        cidx = (sub, per_step - 1 - sub)
        rows = [slice(c * cs, (c + 1) * cs) for c in cidx]
        wq = [_dot(jnp.concatenate([src[d][1][rows[d], cols(h)], src[d][2][rows[d], cols(h)]], axis=0),
                   s[n].astype(BF16)) for n, (d, h) in enumerate(chains)]
        v16 = [(src[d][0][rows[d], cols(h)] - wq[n][:cs]).astype(BF16) for n, (d, h) in enumerate(chains)]
        o = [wq[n][cs:] + _dot(src[d][4][cidx[d], h], v16[n]) for n, (d, h) in enumerate(chains)]
        s = [s[n] * src[d][5][cidx[d], h:h + 1, :] + _dot_tn(src[d][3][rows[d], cols(h)], v16[n])
             for n, (d, h) in enumerate(chains)]
        for n, (d, h) in enumerate(chains):
            outs[d][rows[d], cols(h)] = o[n]
    for n, (d, h) in enumerate(chains):
        s_ref[d * GDN_H + h] = s[n]


def _gdn_scan(u, w, qg, kt, qk, eg, n_lat_chunks):
    bsz, _, lt, _ = u.shape
    per = GDN_SCAN_CHUNKS
    cs = GDN_CHUNK * per
    nc = lt // cs
    n_lat = n_lat_chunks // per
    ncc = nc - n_lat

    def cf(i):
        return jnp.where(i < ncc, n_lat + i, i - ncc)

    def cb(i):
        return nc - 1 - i

    def specs(d, c):
        tok = pl.BlockSpec((None, None, cs, 512), lambda b, i: (b, d, c(i), 0))
        return [tok, tok, tok, tok,
                pl.BlockSpec((None, None, per, GDN_H, GDN_CHUNK, GDN_CHUNK), lambda b, i: (b, d, c(i), 0, 0, 0)),
                pl.BlockSpec((None, None, per, GDN_H, LANES), lambda b, i: (b, d, c(i), 0, 0))]

    def out(c):
        return pl.BlockSpec((None, cs, 512), lambda b, i: (b, c(i), 0))

    args = (u, w, qg, kt, qk, eg)
    return pl.pallas_call(
        _gdn_scan_kernel,
        grid=(bsz, nc),
        in_specs=specs(0, cf) + specs(1, cb),
        out_specs=[out(cf), out(cb)],
        out_shape=[jax.ShapeDtypeStruct((bsz, lt, 512), F32)] * 2,
        scratch_shapes=[pltpu.VMEM((2 * GDN_H, GDN_DK, LANES), F32)],
        compiler_params=_cparams(("parallel", "arbitrary")),
        name="gdn_scan",
    )(*args, *args)


def _attn_prep_kernel(dq_ref, dk_ref, dv_ref, nq_ref, nk_ref, nv_ref, cos_ref, sin_ref, o_ref):
    cos = cos_ref[...]
    sin = sin_ref[...]
    lane = lax.broadcasted_iota(jnp.int32, cos.shape, 1)
    first = (lane % DIFF_D) < DIFF_D // 2

    def rope(x):
        rot = jnp.where(first, pltpu.roll(x, LANES - DIFF_D // 2, 1), pltpu.roll(x, DIFF_D // 2, 1))
        return x * cos + rot * sin

    for h in range(DIFF_H):
        sl = slice(h * LANES, (h + 1) * LANES)
        o_ref[:, h * LANES:(h + 1) * LANES] = (rope(dq_ref[:, sl].astype(F32))
                                               * (DIFF_D ** -0.5 * LOG2E)).astype(BF16)
        o_ref[:, 512 + h * LANES:512 + (h + 1) * LANES] = rope(dk_ref[:, sl].astype(F32)).astype(BF16)
    o_ref[:, 1024:1536] = dv_ref[...].astype(BF16)
    o_ref[:, 1536:2048] = (nq_ref[...].astype(F32) * (NA_D ** -0.5)).astype(BF16)
    o_ref[:, 2048:2560] = nk_ref[...].astype(BF16)
    o_ref[:, 2560:3072] = nv_ref[...].astype(BF16)


def _attn_prep(p, cos, sin):
    bsz, lt, _ = p.shape

    def col(cb):
        return pl.BlockSpec((None, TOK_TILE, 512), lambda b, i: (b, i, cb))

    tab = pl.BlockSpec((TOK_TILE, LANES), lambda b, i: (i, 0))
    return pl.pallas_call(
        _attn_prep_kernel,
        grid=(bsz, lt // TOK_TILE),
        in_specs=[col(C_DQ // 512), col(C_DK // 512), col(C_DV // 512),
                  col(C_NQ // 512), col(C_NK // 512), col(C_NV // 512), tab, tab],
        out_specs=pl.BlockSpec((None, TOK_TILE, 3072), lambda b, i: (b, i, 0)),
        out_shape=jax.ShapeDtypeStruct((bsz, lt, 3072), BF16),
        compiler_params=_cparams(("parallel", "parallel")),
        name="attn_prep",
    )(p, p, p, p, p, p, cos, sin)


def _diff_kernel(lam_ref, q_ref, k_ref, v_ref, nw_ref, o_ref, m_sc, l_sc, acc_sc, *, nkv, lam_init):
    kv = pl.program_id(3)

    @pl.when(kv == 0)
    def _():
        m_sc[...] = jnp.full_like(m_sc, NEG_INF)
        l_sc[...] = jnp.zeros_like(l_sc)
        acc_sc[...] = jnp.zeros_like(acc_sc)

    q = q_ref[...]
    k = k_ref[...]
    v = v_ref[...]
    maps = range(2)
    tkv = k.shape[0]
    pieces = [(c0, min(c0 + DIFF_COL_TILE, tkv)) for c0 in range(0, tkv, DIFF_COL_TILE)]

    def fold(x, op):
        acc = x[:, 0:LANES]
        for c0 in range(LANES, x.shape[1], LANES):
            acc = op(acc, x[:, c0:c0 + LANES])
        return acc

    tq = q.shape[0]
    rb = min(DIFF_ROW_BLOCK, tq)
    chains = [(slice(r0, r0 + rb), m) for r0 in range(0, tq, rb) for m in maps]
    s = [[_dot_nt(q[rows, m * DIFF_D:(m + 1) * DIFF_D], k[c0:c1, m * DIFF_D:(m + 1) * DIFF_D])
          for c0, c1 in pieces] for rows, m in chains]
    m_new, alpha = [], []
    for n, (rows, m) in enumerate(chains):
        mx = None
        for sp in s[n]:
            part = fold(sp, jnp.maximum)
            mx = part if mx is None else jnp.maximum(mx, part)
        m_prev = m_sc[m, rows]
        m_new.append(jnp.maximum(m_prev, jnp.max(mx, -1, keepdims=True)))
        alpha.append(jnp.exp2(m_prev - m_new[n]))
    for n, (rows, m) in enumerate(chains):
        sm, pv = None, None
        for (c0, c1), sp in zip(pieces, s[n]):
            p = jnp.exp2(sp - m_new[n])
            psum = fold(p, jnp.add)
            sm = psum if sm is None else sm + psum
            part = _dot(p.astype(BF16), v[c0:c1, :])
            pv = part if pv is None else pv + part
        l_sc[m, rows] = alpha[n] * l_sc[m, rows] + jnp.sum(sm, -1, keepdims=True)
        acc_sc[m, rows] = alpha[n] * acc_sc[m, rows] + pv
        m_sc[m, rows] = m_new[n]

    @pl.when(kv == nkv - 1)
    def _():
        lam = lam_ref[0:1, 0:1]
        o = acc_sc[0] / l_sc[0] - lam * (acc_sc[1] / l_sc[1])
        y = o * lax.rsqrt(jnp.mean(o * o, -1, keepdims=True) + RMS_EPS) * nw_ref[...] * (1.0 - lam_init)
        o_ref[...] = y.astype(BF16)


def _diff_attn(a, lam_row, norm_w, lam_init, *, tq, tkv, q0, nq, k0, nkv):
    bsz = a.shape[0]
    return pl.pallas_call(
        functools.partial(_diff_kernel, nkv=nkv, lam_init=lam_init),
        grid=(bsz, DIFF_H, nq, nkv),
        in_specs=[pl.BlockSpec((1, LANES), lambda b, h, i, j: (0, 0)),
                  pl.BlockSpec((None, tq, LANES), lambda b, h, i, j: (b, q0 + i, h)),
                  pl.BlockSpec((None, tkv, LANES), lambda b, h, i, j: (b, k0 + j, 4 + h)),
                  pl.BlockSpec((None, tkv, LANES), lambda b, h, i, j: (b, k0 + j, 8 + h)),
                  pl.BlockSpec((1, LANES), lambda b, h, i, j: (0, 0))],
        out_specs=pl.BlockSpec((None, tq, LANES), lambda b, h, i, j: (b, i, h)),
        out_shape=jax.ShapeDtypeStruct((bsz, nq * tq, 512), BF16),
        scratch_shapes=[pltpu.VMEM((2, tq, 1), F32), pltpu.VMEM((2, tq, 1), F32),
                        pltpu.VMEM((2, tq, LANES), F32)],
        compiler_params=_cparams(("parallel", "parallel", "parallel", "arbitrary")),
        name="diff_attn",
    )(lam_row, a, a, a, norm_w.reshape(1, LANES))


def _na_kernel(q_ref, k_ref, v_ref, bias_ref, o_ref, *, s_lat, lc):
    i = pl.program_id(2)
    tq = q_ref.shape[0]
    nk = NA_KROWS * GRID_W
    kstart = jnp.clip(i * tq - (NA_KROWS - NA_QROWS) // 2 * GRID_W, 0, s_lat - nk)
    kstart = pl.multiple_of(kstart, TOK_TILE)
    q = q_ref[...]
    k_loc = k_ref[pl.ds(kstart, nk), :]
    v_loc = v_ref[pl.ds(kstart, nk), :]
    k_ctx = k_ref[s_lat:s_lat + lc, :]
    v_ctx = v_ref[s_lat:s_lat + lc, :]
    heads = range(2)
    sls = [slice(j * NA_D, (j + 1) * NA_D) for j in heads]
    s_loc = [_dot_nt(q[:, sls[j]], k_loc[:, sls[j]]) + bias_ref[j] for j in heads]
    s_ctx = [_dot_nt(q[:, sls[j]], k_ctx[:, sls[j]]) for j in heads]
    m = [jnp.maximum(jnp.max(s_loc[j], -1, keepdims=True), jnp.max(s_ctx[j], -1, keepdims=True)) for j in heads]
    p_loc = [jnp.exp(s_loc[j] - m[j]) for j in heads]
    p_ctx = [jnp.exp(s_ctx[j] - m[j]) for j in heads]
    l = [jnp.sum(p_loc[j], -1, keepdims=True) + jnp.sum(p_ctx[j], -1, keepdims=True) for j in heads]
    o = [_dot(p_loc[j].astype(BF16), v_loc[:, sls[j]]) + _dot(p_ctx[j].astype(BF16), v_ctx[:, sls[j]])
         for j in heads]
    o_ref[...] = jnp.concatenate([o[j] / l[j] for j in heads], axis=1).astype(BF16)


def _na_attn(a, bias, layer, s_lat, lc):
    bsz, lt, _ = a.shape
    tq = NA_QROWS * GRID_W
    nq = s_lat // tq

    def variant(i):
        return jnp.where(i == 0, 0, jnp.where(i == nq - 1, 2, 1))

    return pl.pallas_call(
        functools.partial(_na_kernel, s_lat=s_lat, lc=lc),
        grid=(bsz, NA_H // 2, nq),
        in_specs=[pl.BlockSpec((None, tq, LANES), lambda b, h, i: (b, i, 12 + h)),
                  pl.BlockSpec((None, lt, LANES), lambda b, h, i: (b, 0, 16 + h)),
                  pl.BlockSpec((None, lt, LANES), lambda b, h, i: (b, 0, 20 + h)),
                  pl.BlockSpec((None, None, 2, tq, NA_KROWS * GRID_W),
                               lambda b, h, i: (layer, variant(i), h, 0, 0))],
        out_specs=pl.BlockSpec((None, tq, LANES), lambda b, h, i: (b, i, h)),
        out_shape=jax.ShapeDtypeStruct((bsz, s_lat, 512), BF16),
        compiler_params=_cparams(("parallel", "parallel", "arbitrary")),
        name="na_attn",
    )(a, a, a, bias)


def _na_ctx_kernel(q_ref, k_ref, v_ref, o_ref):
    q = q_ref[...]
    k = k_ref[...]
    v = v_ref[...]
    outs = []
    for j in range(2):
        sl = slice(j * NA_D, (j + 1) * NA_D)
        s = _dot_nt(q[:, sl], k[:, sl])
        p = jnp.exp(s - jnp.max(s, -1, keepdims=True))
        outs.append(_dot(p.astype(BF16), v[:, sl]) / jnp.sum(p, -1, keepdims=True))
    o_ref[...] = jnp.concatenate(outs, axis=1).astype(BF16)


def _na_ctx_attn(a, s_lat, lc):
    bsz = a.shape[0]
    rb = s_lat // lc

    def blk(c0):
        return pl.BlockSpec((None, lc, LANES), lambda b, h: (b, rb, c0 + h))

    return pl.pallas_call(
        _na_ctx_kernel,
        grid=(bsz, NA_H // 2),
        in_specs=[blk(12), blk(16), blk(20)],
        out_specs=pl.BlockSpec((None, lc, LANES), lambda b, h: (b, 0, h)),
        out_shape=jax.ShapeDtypeStruct((bsz, lc, 512), BF16),
        compiler_params=_cparams(("parallel", "parallel")),
        name="na_ctx_attn",
    )(a, a, a)


def _na_bias_tables(rpb, rows):
    qr = np.arange(NA_QROWS)[:, None]
    kr = np.arange(NA_KROWS)[None, :]
    qc = np.arange(GRID_W)[:, None]
    kc = np.arange(GRID_W)[None, :]
    w0 = np.clip(qc - WIN_W // 2, 0, GRID_W - WIN_W)
    okc = (kc >= w0) & (kc < w0 + WIN_W)
    dc = np.clip(kc - qc + WIN_W - 1, 0, 2 * WIN_W - 2)
    half = (NA_KROWS - NA_QROWS) // 2
    n_dr, n_dc = 2 * WIN_H - 1, 2 * WIN_W - 1
    sel_c = (dc[None] == np.arange(n_dc)[:, None, None]) & okc[None]
    sel_p = np.zeros((3, NA_QROWS, NA_KROWS // 2, n_dr + 1), bool)
    ok = np.zeros((3, NA_QROWS, GRID_W, NA_KROWS, GRID_W), bool)
    for vi, (r_start, delta) in enumerate(((0, 0), (NA_QROWS, -half), (rows - NA_QROWS, -2 * half))):
        r = r_start + qr
        kabs = r_start + delta + kr
        r0 = np.clip(r - WIN_H // 2, 0, rows - WIN_H)
        okr = (kabs >= r0) & (kabs < r0 + WIN_H)
        ok[vi] = okr[:, None, :, None] & okc[None, :, None, :]
        dr_first = (kabs - r + WIN_H - 1)[:, 0::2]
        sel_p[vi] = (dr_first[..., None] + 1) == np.arange(n_dr + 1)
    toep = jnp.einsum('lhaj,jqk->lhaqk', rpb, sel_c.astype(np.float32), precision=lax.Precision.HIGHEST)
    toep = jnp.pad(toep, ((0, 0), (0, 0), (1, 1), (0, 0), (0, 0)))
    toep_pair = jnp.concatenate([toep[:, :, :-1], toep[:, :, 1:]], axis=-1)
    tab = jnp.einsum('vrpa,lhaqw->lvhrqpw', sel_p.astype(np.float32), toep_pair, precision=lax.Precision.HIGHEST)
    tab = tab.reshape(rpb.shape[0], 3, NA_H, NA_QROWS * GRID_W, NA_KROWS * GRID_W)
    ok = ok.reshape(1, 3, 1, NA_QROWS * GRID_W, NA_KROWS * GRID_W)
    return jnp.where(ok, tab, NEG_INF).astype(F32)


def _merge_kernel(of_ref, ob_ref, z_ref, ybl_ref, ybc_ref, ycl_ref, ycc_ref, ga_ref, gb_ref, gc_ref, x_ref,
                  gnw_ref, wb_ref, wo_ref, g1_ref, lg_ref, lb_ref, sh_ref, sc_ref, wr_ref, br_ref,
                  xo_ref, h_ref, lg_out_ref, *, nl):
    is_ctx = pl.program_id(1) >= nl
    yb = jnp.where(is_ctx, ybc_ref[...], ybl_ref[...])
    yc = jnp.where(is_ctx, ycc_ref[...], ycl_ref[...])
    o = of_ref[...] + ob_ref[...]
    z = z_ref[...].astype(F32)
    gnw = gnw_ref[...]
    parts = []
    for h in range(GDN_H):
        sl = slice(h * LANES, (h + 1) * LANES)
        oh = o[:, sl]
        zh = z[:, sl]
        parts.append(oh * lax.rsqrt(jnp.mean(oh * oh, -1, keepdims=True) + RMS_EPS) * gnw * (zh * _sigmoid(zh)))
    ya = jnp.concatenate(parts, axis=1).astype(BF16)
    m = (_sigmoid(ga_ref[...].astype(F32)) * _dot(ya, wb_ref[0])
         + _sigmoid(gb_ref[...].astype(F32)) * _dot(yb, wb_ref[1])
         + _sigmoid(gc_ref[...].astype(F32)) * _dot(yc, wb_ref[2]))
    mx = _dot(m.astype(BF16), wo_ref[...])
    x = _ln(DN_ALPHA * x_ref[...] + g1_ref[...] * mx) * lg_ref[...] + lb_ref[...]
    xo_ref[...] = x
    hh = _ln(x) * (1.0 + sc_ref[...]) + sh_ref[...]
    h_ref[...] = hh.astype(BF16)
    lg_out_ref[...] = _dot_3pass(hh, wr_ref[...]) + br_ref[...]


def _merge(o_f, o_b, p, yb, yb_ctx, yc, yc_ctx, x, gnw, wb, wo, modsel, nl, ln_g, ln_b, wr, br):
    bsz, lt, d = x.shape

    def tok(width, cb):
        return pl.BlockSpec((None, TOK_TILE, width), lambda b, i: (b, i, cb))

    def const(shape):
        return pl.BlockSpec(shape, lambda b, i: (0,) * len(shape))

    lat = pl.BlockSpec((None, TOK_TILE, 512), lambda b, i: (b, jnp.minimum(i, nl - 1), 0))
    cxt = pl.BlockSpec((None, TOK_TILE, 512), lambda b, i: (b, 0, 0))
    return pl.pallas_call(
        functools.partial(_merge_kernel, nl=nl),
        grid=(bsz, lt // TOK_TILE),
        in_specs=[tok(512, 0), tok(512, 0), tok(512, C_GZ // 512), lat, cxt, lat, cxt,
                  tok(d, C_GATE // d), tok(d, C_GATE // d + 1), tok(d, C_GATE // d + 2), tok(d, 0),
                  const((1, LANES)), const((3, 512, d)), const((d, d)),
                  _mod_spec(d, nl, 2), const((1, d)), const((1, d)),
                  _mod_spec(d, nl, 3), _mod_spec(d, nl, 4),
                  const((d, LANES)), const((1, LANES))],
        out_specs=[tok(d, 0), tok(d, 0), tok(LANES, 0)],
        out_shape=[jax.ShapeDtypeStruct((bsz, lt, d), F32),
                   jax.ShapeDtypeStruct((bsz, lt, d), BF16),
                   jax.ShapeDtypeStruct((bsz, lt, LANES), F32)],
        compiler_params=_cparams(("parallel", "parallel")),
        name="merge",
    )(o_f, o_b, p, yb, yb_ctx, yc, yc_ctx, p, p, p, x, gnw.reshape(1, LANES), wb, wo,
      modsel, ln_g.reshape(1, d), ln_b.reshape(1, d), modsel, modsel, wr, br)


def _ffn_kernel(be_ref, nu_ref, x_ref, w1_ref, b1_ref, w2_ref, b2_ref, o_ref, w1c, w2c):
    i = pl.program_id(0)

    @pl.when(i < nu_ref[0])
    def _():
        changed = jnp.logical_or(i == 0, be_ref[i] != be_ref[jnp.maximum(i - 1, 0)])

        @pl.when(changed)
        def _():
            w1c[...] = w1_ref[...].astype(BF16)
            w2c[...] = w2_ref[...].astype(BF16)

        x = x_ref[...].astype(BF16)
        y = None
        for c0 in range(0, D_EXPERT, FFN_COL_CHUNK):
            c1 = c0 + FFN_COL_CHUNK
            hg = _dot(x, w1c[:, c0:c1]) + b1_ref[:, c0:c1]
            hl = _dot(x, w1c[:, D_EXPERT + c0:D_EXPERT + c1]) + b1_ref[:, D_EXPERT + c0:D_EXPERT + c1]
            x_glu = jnp.minimum(hg, SWIGLU_LIMIT)
            x_lin = jnp.clip(hl, -SWIGLU_LIMIT, SWIGLU_LIMIT)
            act = x_glu * _sigmoid(SWIGLU_ALPHA * x_glu) * (x_lin + 1.0)
            part = _dot(act.astype(BF16), w2c[c0:c1, :])
            y = part if y is None else y + part
        o_ref[...] = y + b2_ref[...]

    @pl.when(i >= nu_ref[0])
    def _():
        o_ref[...] = jnp.zeros_like(o_ref)


def _ffn(blk_e, n_used, xs, w1, b1, w2, b2, layer):
    n_rows, d = xs.shape
    n_blk = n_rows // MOE_ROWS
    depth, n_exp, _, de2 = w1.shape
    gs = pltpu.PrefetchScalarGridSpec(
        num_scalar_prefetch=2,
        grid=(n_blk,),
        in_specs=[pl.BlockSpec((MOE_ROWS, d), lambda i, be, nu: (i, 0)),
                  pl.BlockSpec((None, None, d, de2), lambda i, be, nu: (layer, be[i], 0, 0)),
                  pl.BlockSpec((None, None, 1, de2), lambda i, be, nu: (layer, be[i], 0, 0)),
                  pl.BlockSpec((None, None, de2 // 2, d), lambda i, be, nu: (layer, be[i], 0, 0)),
                  pl.BlockSpec((None, None, 1, d), lambda i, be, nu: (layer, be[i], 0, 0))],
        out_specs=pl.BlockSpec((MOE_ROWS, d), lambda i, be, nu: (i, 0)),
        scratch_shapes=[pltpu.VMEM((d, de2), BF16), pltpu.VMEM((de2 // 2, d), BF16)],
    )
    return pl.pallas_call(
        _ffn_kernel,
        grid_spec=gs,
        out_shape=jax.ShapeDtypeStruct((n_rows, d), F32),
        compiler_params=_cparams(("arbitrary",)),
        name="ffn",
    )(blk_e, n_used, xs, w1, b1.reshape(depth, n_exp, 1, de2), w2, b2.reshape(depth, n_exp, 1, d))


def _route_kernel(lg_ref, rc_ref, rt_ref, n8_ref):
    l = lg_ref[...]
    tt = l.shape[0]
    lane = lax.broadcasted_iota(jnp.int32, l.shape, 1)
    picks, vals = [], []
    for _ in range(TOP_K):
        m = jnp.max(l, -1, keepdims=True)
        idx = jnp.min(jnp.where(l == m, lane, LANES), -1, keepdims=True)
        oh = lane == idx
        picks.append(oh)
        vals.append(m)
        l = jnp.where(oh, -jnp.inf, l)
    sel = sum(jnp.where(oh, 1.0, 0.0) for oh in picks)
    cnt = jnp.sum(sel, 0, keepdims=True)
    n8 = jnp.floor((cnt + 7.0) * 0.125) * 8.0
    r = lax.broadcasted_iota(jnp.int32, (LANES, LANES), 0)
    c = lax.broadcasted_iota(jnp.int32, (LANES, LANES), 1)
    off8 = _dot_f32(jnp.broadcast_to(n8, (8, LANES)), jnp.where(r < c, 1.0, 0.0))[0:1]
    tr = lax.broadcasted_iota(jnp.int32, (tt, tt), 0)
    tc = lax.broadcasted_iota(jnp.int32, (tt, tt), 1)
    rank = _dot(jnp.where(tc < tr, 1.0, 0.0).astype(BF16), sel.astype(BF16))
    slot = off8 + rank
    es = [jnp.exp(v - vals[0]) for v in vals]
    den = sum(es)
    rc = jnp.zeros(l.shape, F32)
    for kk in range(TOP_K):
        loc = jnp.sum(jnp.where(picks[kk], slot, 0.0), -1, keepdims=True)
        rc = jnp.where(lane == kk, loc, rc)
        rc = jnp.where(lane == TOP_K + kk, es[kk] / den, rc)
    rc_ref[...] = rc
    rt_ref[...] = rc.T[0:8, :]
    n8_ref[...] = jnp.broadcast_to(n8, (8, LANES))


def _route(logits):
    t = logits.shape[0]
    nt = t // MOE_TILE
    return pl.pallas_call(
        _route_kernel,
        grid=(nt,),
        in_specs=[pl.BlockSpec((MOE_TILE, LANES), lambda i: (i, 0))],
        out_specs=[pl.BlockSpec((MOE_TILE, LANES), lambda i: (i, 0)),
                   pl.BlockSpec((None, 8, MOE_TILE), lambda i: (i, 0, 0)),
                   pl.BlockSpec((None, 8, LANES), lambda i: (i, 0, 0))],
        out_shape=[jax.ShapeDtypeStruct((t, LANES), F32),
                   jax.ShapeDtypeStruct((nt, 8, MOE_TILE), F32),
                   jax.ShapeDtypeStruct((nt, 8, LANES), F32)],
        compiler_params=_cparams(("parallel",)),
        name="moe_route",
    )(logits)


def _run_copies(n, pieces, make_copy, wait):
    done = jnp.int32(0)
    for size in pieces:
        take = (n & size) != 0

        @pl.when(take)
        def _():
            cp = make_copy(done, size)
            if wait:
                cp.wait()
            else:
                cp.start()

        done = done + jnp.where(take, size, 0)


def _tile_runs(gs_ref, n8_ref, tile, make_copy, wait):
    def body(e, local):
        n = n8_ref[tile * N_EXPERTS + e]
        g = gs_ref[tile * N_EXPERTS + e]
        _run_copies(n, MOE_RUN_PIECES,
                    lambda done, size: make_copy(pl.multiple_of(local + done, 8),
                                                 pl.multiple_of(g + done, 8), size), wait)
        return local + n
    lax.fori_loop(0, N_EXPERTS, body, jnp.int32(0))


def _dispatch_kernel(gs_ref, n8_ref, go_ref, gn_ref, h_ref, rt_ref, xs_ref, gbuf, zbuf, sems):
    i = pl.program_id(0)
    slot = i % 2
    sem = sems.at[0]

    @pl.when(i == 0)
    def _():
        zbuf[...] = jnp.zeros_like(zbuf)
        zrows = zbuf.shape[0]
        for wait in (False, True):
            def body(e, carry):
                _run_copies(gn_ref[e], MOE_GAP_PIECES,
                            lambda done, size: pltpu.make_async_copy(
                                zbuf.at[pl.ds(0, size), :],
                                xs_ref.at[pl.ds(pl.multiple_of(go_ref[e] + done, 8), size), :], sem), wait)
                return carry
            lax.fori_loop(0, N_EXPERTS, body, 0)

            def tail(j, carry):
                cp = pltpu.make_async_copy(
                    zbuf, xs_ref.at[pl.ds(pl.multiple_of(go_ref[N_EXPERTS] + j * zrows, 8), zrows), :], sem)
                if wait:
                    cp.wait()
                else:
                    cp.start()
                return carry
            lax.fori_loop(0, gn_ref[N_EXPERTS], tail, 0)

    loc = rt_ref[0:TOP_K, :].astype(jnp.int32)
    row = lax.broadcasted_iota(jnp.int32, (gbuf.shape[1], loc.shape[1]), 0)
    hit = row == loc[0:1, :]
    for kk in range(1, TOP_K):
        hit = jnp.logical_or(hit, row == loc[kk:kk + 1, :])
    gbuf[slot] = _dot(jnp.where(hit, 1.0, 0.0).astype(BF16), h_ref[...])

    def runs(tile, sl, wait):
        _tile_runs(gs_ref, n8_ref, tile, lambda local, g, size: pltpu.make_async_copy(
            gbuf.at[sl, pl.ds(local, size), :], xs_ref.at[pl.ds(g, size), :], sems.at[sl]), wait)

    runs(i, slot, False)

    @pl.when(i > 0)
    def _():
        runs(i - 1, 1 - slot, True)

    @pl.when(i == pl.num_programs(0) - 1)
    def _():
        runs(i, slot, True)


def _dispatch(gstart, n8, gap_off, gap_n, h2, rt, n_rows):
    t, d = h2.shape
    nt = t // MOE_TILE
    gs = pltpu.PrefetchScalarGridSpec(
        num_scalar_prefetch=4,
        grid=(nt,),
        in_specs=[pl.BlockSpec((MOE_TILE, d), lambda i, *_: (i, 0)),
                  pl.BlockSpec((None, 8, MOE_TILE), lambda i, *_: (i, 0, 0))],
        out_specs=pl.BlockSpec(memory_space=pl.ANY),
        scratch_shapes=[pltpu.VMEM((2, MOE_GBUF_ROWS, d), F32), pltpu.VMEM((MOE_ROWS // 2, d), F32),
                        pltpu.SemaphoreType.DMA((2,))],
    )
    return pl.pallas_call(
        _dispatch_kernel,
        grid_spec=gs,
        out_shape=jax.ShapeDtypeStruct((n_rows, d), F32),
        compiler_params=_cparams(("arbitrary",)),
        name="moe_dispatch",
    )(gstart, n8, gap_off, gap_n, h2, rt)


def _combine_kernel(gs_ref, n8_ref, ys_ref, rc_ref, x_ref, g_ref, lg_ref, lb_ref, *refs, has_h):
    if has_h:
        sh_ref, sc_ref, xo_ref, h_ref, ybuf, sems = refs
    else:
        xo_ref, ybuf, sems = refs
    i = pl.program_id(0)
    slot = i % 2

    def runs(tile, sl, wait):
        _tile_runs(gs_ref, n8_ref, tile, lambda local, g, size: pltpu.make_async_copy(
            ys_ref.at[pl.ds(g, size), :], ybuf.at[sl, pl.ds(local, size), :], sems.at[sl]), wait)

    @pl.when(i == 0)
    def _():
        ybuf[...] = jnp.zeros_like(ybuf)
        runs(i, slot, False)

    runs(i, slot, True)

    @pl.when(i + 1 < pl.num_programs(0))
    def _():
        runs(i + 1, 1 - slot, False)

    rc = rc_ref[...]
    col = lax.broadcasted_iota(jnp.int32, (rc.shape[0], ybuf.shape[1]), 1)
    wgt = jnp.zeros(col.shape, F32)
    for kk in range(TOP_K):
        wgt = jnp.where(col == rc[:, kk:kk + 1].astype(jnp.int32), rc[:, TOP_K + kk:TOP_K + kk + 1], wgt)
    y = ybuf[slot]
    w_hi = wgt.astype(BF16)
    w_lo = (wgt - w_hi.astype(F32)).astype(BF16)
    y_hi = y.astype(BF16)
    y_lo = (y - y_hi.astype(F32)).astype(BF16)
    f = _dot(w_hi, y_hi) + _dot(w_hi, y_lo) + _dot(w_lo, y_hi)
    x = _ln(DN_ALPHA * x_ref[...] + g_ref[...] * f) * lg_ref[...] + lb_ref[...]
    xo_ref[...] = x
    if has_h:
        h_ref[...] = (_ln(x) * (1.0 + sc_ref[...]) + sh_ref[...]).astype(BF16)


def _combine(gstart, n8, ys, rc, x, nl, tiles_per_batch, gate, ln_g, ln_b, shift=None):
    t, d = x.shape
    nt = t // MOE_TILE
    has_h = shift is not None

    def mod(k):
        return pl.BlockSpec((None, None, 1, d),
                            lambda i, *_: (i // tiles_per_batch, (i % tiles_per_batch) // nl, 0, k))

    tok = pl.BlockSpec((MOE_TILE, d), lambda i, *_: (i, 0))
    vec = pl.BlockSpec((1, d), lambda i, *_: (0, 0))
    in_specs = [pl.BlockSpec(memory_space=pl.ANY), pl.BlockSpec((MOE_TILE, LANES), lambda i, *_: (i, 0)),
                tok, mod(gate[1]), vec, vec]
    args = [ys, rc, x, gate[0], ln_g.reshape(1, d), ln_b.reshape(1, d)]
    out_specs, out_shape = [tok], [jax.ShapeDtypeStruct((t, d), F32)]
    if has_h:
        in_specs += [mod(shift[1]), mod(shift[1] + 1)]
        args += [shift[0], shift[0]]
        out_specs.append(tok)
        out_shape.append(jax.ShapeDtypeStruct((t, d), BF16))
    gs = pltpu.PrefetchScalarGridSpec(
        num_scalar_prefetch=2,
        grid=(nt,),
        in_specs=in_specs,
        out_specs=out_specs,
        scratch_shapes=[pltpu.VMEM((2, MOE_GBUF_ROWS, d), F32), pltpu.SemaphoreType.DMA((2,))],
    )
    return pl.pallas_call(
        functools.partial(_combine_kernel, has_h=has_h),
        grid_spec=gs,
        out_shape=out_shape,
        compiler_params=_cparams(("arbitrary",)),
        name="moe_combine",
    )(gstart, n8, *args)


def _moe(h2, logits, w1, b1, w2, b2, layer, **post):
    t, d = h2.shape
    nt = t // MOE_TILE
    rc, rt, n8f = _route(logits)
    n8 = n8f[:, 0, :N_EXPERTS].astype(jnp.int32)
    e_rows = jnp.sum(n8, 0)
    e_pad = (e_rows + MOE_ROWS - 1) // MOE_ROWS * MOE_ROWS
    e_end = jnp.cumsum(e_pad)
    e_start = e_end - e_pad
    gstart = (e_start[None, :] + jnp.cumsum(n8, 0) - n8).reshape(-1)
    n_rows = -(-(t * TOP_K + nt * N_EXPERTS * 7 + N_EXPERTS * (MOE_ROWS - 1)) // MOE_ROWS) * MOE_ROWS
    n_used = (e_end[-1:] // MOE_ROWS).astype(jnp.int32)
    blk = jnp.minimum(jnp.arange(n_rows // MOE_ROWS, dtype=jnp.int32), n_used - 1) * MOE_ROWS
    blk_e = jnp.minimum(jnp.sum(blk[:, None] >= e_end[None, :], axis=1), N_EXPERTS - 1).astype(jnp.int32)
    n8_flat = n8.reshape(-1)
    gap_off = jnp.concatenate([e_start + e_rows, e_end[-1:]]).astype(jnp.int32)
    gap_n = jnp.concatenate([e_pad - e_rows, (n_rows - e_end[-1:]) // (MOE_ROWS // 2)]).astype(jnp.int32)
    xs = _dispatch(gstart.astype(jnp.int32), n8_flat, gap_off, gap_n, h2, rt, n_rows)
    ys = _ffn(blk_e, n_used, xs, w1, b1, w2, b2, layer)
    return _combine(gstart, n8_flat, ys, rc, **post)


def _rope_tables(s_lat, lc):
    t = jnp.arange(s_lat)
    row = (t // GRID_W).astype(F32)
    col = (t % GRID_W).astype(F32)
    n_freq = DIFF_D // 4
    inv = ROPE_BASE ** (-jnp.arange(n_freq, dtype=F32) / n_freq)
    ang = jnp.concatenate([row[:, None] * inv, col[:, None] * inv], -1)
    ang = jnp.concatenate([ang, ang, ang, ang], -1)
    sign = jnp.where((jnp.arange(LANES) % DIFF_D) < DIFF_D // 2, -1.0, 1.0)
    cos = jnp.concatenate([jnp.cos(ang), jnp.ones((lc, LANES), F32)], 0)
    sin = jnp.concatenate([jnp.sin(ang) * sign, jnp.zeros((lc, LANES), F32)], 0)
    return cos.astype(F32), sin.astype(F32)


def kernel(x, c, ctx, c_ctx, w_ada, b_ada, w_in, conv_w, gdn_a_log, gdn_dt_bias, gdn_norm_w, diff_lambda, diff_norm_w, na_rpb, w_branch, w_out, ln_g, ln_b, w_router, b_router, w_exp1, b_exp1, w_exp2, b_exp2):
    bsz, s_lat, d = x.shape
    lc = ctx.shape[1]
    depth = w_ada.shape[0]
    assert lc == TOK_TILE == MOE_TILE and s_lat % (NA_QROWS * GRID_W) == 0 and d == 1024 and depth == 4
    lt = s_lat + lc
    nl = s_lat // TOK_TILE
    rows = s_lat // GRID_W
    assert rows >= 3 * NA_QROWS

    xs = jnp.concatenate([x, ctx], axis=1)
    cmat = jnp.zeros((8, d), F32).at[:bsz].set(c).at[bsz].set(c_ctx)
    mod = _ada(cmat, w_ada, b_ada)
    cos, sin = _rope_tables(s_lat, lc)

    w_main = jnp.concatenate([w_in[:, :, :2048], w_in[:, :, 2064:]], axis=2).astype(BF16)
    w_ab = jnp.pad(w_in[:, :, 2048:2064], ((0, 0), (0, 0), (0, LANES - 16))).astype(BF16)
    wb16 = w_branch.astype(BF16)
    wo16 = w_out.astype(BF16)
    wr_pad = jnp.pad(w_router, ((0, 0), (0, 0), (0, LANES - N_EXPERTS)))
    br_pad = jnp.pad(b_router, ((0, 0), (0, LANES - N_EXPERTS)), constant_values=NEG_INF).reshape(depth, 1, LANES)
    lane_pad = LANES - 4 * GDN_H
    na_bias = _na_bias_tables(na_rpb, rows)

    modsels = [jnp.stack([mod[l, :bsz], jnp.broadcast_to(mod[l, bsz], (bsz, 6 * d))], axis=1)[:, :, None, :]
               for l in range(depth)]
    h1 = _modulate(xs, nl, modsels[0], 0)
    for l in range(depth):
        lam_init = 0.8 - 0.6 * math.exp(-0.3 * l)
        modsel = modsels[l]
        tm = 512 if (bsz * lt) % 512 == 0 else TOK_TILE
        h1f = h1.reshape(bsz * lt, d)
        p = _matmul(h1f, w_main, l, tm, C_AB // 4, BF16).reshape(bsz, lt, C_AB)
        pab = _matmul(h1f, w_ab, l, tm, LANES, F32).reshape(bsz, lt, LANES)

        def head_row(v2):
            z4 = jnp.zeros((GDN_H,), F32)
            return jnp.pad(jnp.concatenate([v2[0], z4, v2[1], z4]), (0, lane_pad)).reshape(1, LANES)
        qkv_n, gcs, gct = _gdn_prep(p, pab, conv_w[l], head_row(gdn_a_log[l]), head_row(gdn_dt_bias[l]), nl)
        o_f, o_b = _gdn_scan(*_gdn_intra(qkv_n, gcs, gct), s_lat // GDN_CHUNK)

        a = _attn_prep(p, cos, sin)
        lv = diff_lambda[l].astype(F32)
        lam = jnp.exp(jnp.sum(lv[0] * lv[1])) - jnp.exp(jnp.sum(lv[2] * lv[3])) + lam_init
        lam_row = jnp.full((1, LANES), lam, F32)
        tq = min(1024, s_lat)
        tkv = lt // 3 if (lt // 3) % TOK_TILE == 0 else TOK_TILE
        yb = _diff_attn(a, lam_row, diff_norm_w[l], lam_init, tq=tq, tkv=tkv, q0=0, nq=s_lat // tq,
                        k0=0, nkv=lt // tkv)
        yb_ctx = _diff_attn(a, lam_row, diff_norm_w[l], lam_init, tq=lc, tkv=lc, q0=s_lat // lc, nq=1,
                            k0=s_lat // lc, nkv=1)
        yc = _na_attn(a, na_bias, l, s_lat, lc)
        yc_ctx = _na_ctx_attn(a, s_lat, lc)

        xs, h2, logits = _merge(o_f, o_b, p, yb, yb_ctx, yc, yc_ctx, xs, gdn_norm_w[l], wb16[l], wo16[l],
                                modsel, nl, ln_g[l, 0], ln_b[l, 0], wr_pad[l], br_pad[l])
        res = _moe(h2.reshape(bsz * lt, d), logits.reshape(bsz * lt, LANES), w_exp1, b_exp1, w_exp2, b_exp2, l,
                   x=xs.reshape(bsz * lt, d), nl=nl, tiles_per_batch=lt // MOE_TILE, gate=(modsel, 5),
                   ln_g=ln_g[l, 1], ln_b=ln_b[l, 1], shift=(modsels[l + 1], 0) if l + 1 < depth else None)
        xs = res[0].reshape(bsz, lt, d)
        if l + 1 < depth:
            h1 = res[1].reshape(bsz, lt, d)
    return xs[:, :s_lat]
```
